```python
import math
import jax, jax.numpy as jnp
from jax import lax
import numpy as np

D_MODEL = 2048
BATCH = 1
SEQ = 16384
DEPTH = 2

GRID_W = 64
CTX_LEN = 256
EPS = 1e-6
NEG_INF = -1e30
ROPE_BASE = 10000.0
BLOCK = 128
A_HEADS = 8
A_KV_HEADS = 2
A_HEAD_DIM = 128
WINDOW = 128
B_HEADS = 8
B_Q_RANK = 384
B_KV_RANK = 256
B_NOPE_DIM = 128
B_ROPE_DIM = 64
B_V_DIM = 128
D_FF = 5632
N_EXPERTS = 8
TOP_K = 2
A_WIDTH = A_HEADS * A_HEAD_DIM
A_KV_WIDTH = A_KV_HEADS * A_HEAD_DIM
B_WIDTH = B_HEADS * B_V_DIM
Q_WIDTH = A_WIDTH + B_Q_RANK
KV_WIDTH = 2 * A_KV_WIDTH + B_KV_RANK + B_ROPE_DIM
IN_WIDTH = Q_WIDTH + KV_WIDTH
MIX_WIDTH = A_WIDTH + B_WIDTH

kernel_name = "hybrid_swa_mla_moe_dit_block"


def rms_norm(x, g):
    xf = x.astype(jnp.float32)
    y = xf * lax.rsqrt(jnp.mean(xf * xf, axis=-1, keepdims=True) + EPS)
    return (y * g.astype(jnp.float32)).astype(x.dtype)


def modulate(h, shift, scale):
    return h * (1 + scale) + shift


def axial_rope_tables(rows, rot_dim, dtype):
    row = jnp.repeat(jnp.arange(rows, dtype=jnp.float32), GRID_W)
    col = jnp.tile(jnp.arange(GRID_W, dtype=jnp.float32), rows)
    axis_dim = rot_dim // 2
    inv_freq = ROPE_BASE ** (-jnp.arange(0, axis_dim, 2, dtype=jnp.float32) / axis_dim)
    ang_r = row[:, None] * inv_freq[None, :]
    ang_c = col[:, None] * inv_freq[None, :]
    ang = jnp.concatenate([ang_r, ang_r, ang_c, ang_c], axis=-1)
    return jnp.cos(ang).astype(dtype), jnp.sin(ang).astype(dtype)


def apply_axial_rope(t, cos, sin):
    t1, t2, t3, t4 = jnp.split(t, 4, axis=-1)
    rot = jnp.concatenate([-t2, t1, -t4, t3], axis=-1)
    return t * cos[:, None, :] + rot * sin[:, None, :]


def split_q(pq):
    return pq[..., :A_WIDTH], pq[..., A_WIDTH:]


def split_kv(pkv):
    o1 = A_KV_WIDTH
    o2 = 2 * A_KV_WIDTH
    o3 = o2 + B_KV_RANK
    return pkv[..., :o1], pkv[..., o1:o2], pkv[..., o2:o3], pkv[..., o3:]


def mla_queries(cq, q_norm_g, w_uq):
    b, n, _ = cq.shape
    q = (rms_norm(cq, q_norm_g) @ w_uq).reshape(b, n, B_HEADS, B_NOPE_DIM + B_ROPE_DIM)
    return q[..., :B_NOPE_DIM], q[..., B_NOPE_DIM:]


def mla_keys_values(ckv, kv_norm_g, w_ukv):
    b, n, _ = ckv.shape
    kv = (rms_norm(ckv, kv_norm_g) @ w_ukv).reshape(b, n, B_HEADS, B_NOPE_DIM + B_V_DIM)
    return kv[..., :B_NOPE_DIM], kv[..., B_NOPE_DIM:]


def windowed_gqa_latent(q, k, v, k_ctx, v_ctx, sink):
    b, s, hq, d = q.shape
    hkv = k.shape[2]
    g = hq // hkv
    nb = s // BLOCK
    c = k_ctx.shape[1]
    nk = 3 * BLOCK
    scale = 1.0 / math.sqrt(d)
    qb = q.reshape(b, nb, BLOCK, hkv, g, d)

    def band(t):
        tp = jnp.pad(t, ((0, 0), (BLOCK, BLOCK), (0, 0), (0, 0)))
        return jnp.concatenate([tp[:, o:o + s].reshape(b, nb, BLOCK, hkv, d) for o in (0, BLOCK, 2 * BLOCK)], axis=2)

    kb, vb = band(k), band(v)
    q_idx = jnp.arange(BLOCK)[:, None]
    k_idx = jnp.arange(nk)[None, :]
    in_window = jnp.abs(k_idx - BLOCK - q_idx) <= WINDOW
    k_pos = jnp.arange(nb)[:, None] * BLOCK - BLOCK + jnp.arange(nk)[None, :]
    in_range = (k_pos >= 0) & (k_pos < s)
    mask = in_window[None] & in_range[:, None, :]
    s_loc = jnp.einsum('bnqhgd,bnkhd->bnhgqk', qb, kb).astype(jnp.float32) * scale
    s_loc = jnp.where(mask[None, :, None, None], s_loc, NEG_INF)
    s_ctx = jnp.einsum('bnqhgd,bchd->bnhgqc', qb, k_ctx).astype(jnp.float32) * scale
    s_sink = jnp.broadcast_to(sink.astype(jnp.float32).reshape(1, 1, hkv, g, 1, 1), s_loc.shape[:-1] + (1,))
    p = jax.nn.softmax(jnp.concatenate([s_loc, s_ctx, s_sink], axis=-1), axis=-1).astype(v.dtype)
    o = (jnp.einsum('bnhgqk,bnkhd->bnqhgd', p[..., :nk], vb)
         + jnp.einsum('bnhgqc,bchd->bnqhgd', p[..., nk:nk + c], v_ctx))
    return o.reshape(b, s, hq * d)


def context_gqa(q, k, v, sink):
    b, c, hq, d = q.shape
    hkv = k.shape[2]
    g = hq // hkv
    scale = 1.0 / math.sqrt(d)
    qg = q.reshape(b, c, hkv, g, d)
    logits = jnp.einsum('bqhgd,bkhd->bhgqk', qg, k).astype(jnp.float32) * scale
    s_sink = jnp.broadcast_to(sink.astype(jnp.float32).reshape(1, hkv, g, 1, 1), logits.shape[:-1] + (1,))
    p = jax.nn.softmax(jnp.concatenate([logits, s_sink], axis=-1), axis=-1)[..., :c].astype(v.dtype)
    o = jnp.einsum('bhgqk,bkhd->bqhgd', p, v)
    return o.reshape(b, c, hq * d)


def mla_attend(q_nope, q_rope, k_nope, k_rope, v):
    b, n, h, _ = q_nope.shape
    nb = n // BLOCK
    scale = 1.0 / math.sqrt(B_NOPE_DIM + B_ROPE_DIM)

    def to_blocks(t):
        return jnp.moveaxis(t.reshape((b, nb, BLOCK) + t.shape[2:]), 1, 0)

    def one_block(qs):
        qn, qr = qs
        logits = (jnp.einsum('bqhd,bkhd->bhqk', qn, k_nope)
                  + jnp.einsum('bqhr,bkr->bhqk', qr, k_rope)).astype(jnp.float32) * scale
        p = jax.nn.softmax(logits, axis=-1).astype(v.dtype)
        return jnp.einsum('bhqk,bkhd->bqhd', p, v)

    o = lax.map(one_block, (to_blocks(q_nope), to_blocks(q_rope)))
    return jnp.moveaxis(o, 0, 1).reshape(b, n, h * B_V_DIM)


def mix_out(o_a, o_b, grp_a_g, grp_b_g, w_o):
    return jnp.concatenate([rms_norm(o_a, grp_a_g), rms_norm(o_b, grp_b_g)], axis=-1) @ w_o


def swiglu(h, w1, w3, w2):
    return (jax.nn.silu(h @ w1) * (h @ w3)) @ w2


def moe_swiglu(h, w_router, w1, w3, w2):
    logits = jnp.einsum('bnd,de->bne', h, w_router).astype(jnp.float32)
    top_logit, top_idx = lax.top_k(logits, TOP_K)
    top_w = jax.nn.softmax(top_logit, axis=-1)
    combine = jnp.einsum('bnk,bnke->bne', top_w, jax.nn.one_hot(top_idx, N_EXPERTS, dtype=jnp.float32)).astype(h.dtype)
    out = jnp.zeros_like(h)
    for e in range(N_EXPERTS):
        out = out + combine[..., e:e + 1] * swiglu(h, w1[e], w3[e], w2[e])
    return out


def setup_inputs(seed: int = 0) -> dict:
    key = jax.random.key(seed)
    ks = iter(jax.random.split(key, 40))
    n_dense = (DEPTH + 1) // 2
    n_moe = DEPTH // 2

    def nrm(shape, scale):
        return jax.random.normal(next(ks), shape, jnp.float32) * scale

    def gain(shape):
        return 1.0 + nrm(shape, 0.05)

    return {
        "x": nrm((BATCH, SEQ, D_MODEL), 1.0),
        "c": nrm((BATCH, D_MODEL), 1.0),
        "ctx": nrm((BATCH, CTX_LEN, D_MODEL), 1.0),
        "c_ctx": nrm((D_MODEL,), 1.0),
        "w_ada": nrm((DEPTH, D_MODEL, 6 * D_MODEL), 0.2 * D_MODEL ** -0.5),
        "b_ada": nrm((DEPTH, 6 * D_MODEL), 0.02),
        "pre_attn_g": gain((DEPTH, D_MODEL)),
        "post_attn_g": gain((DEPTH, D_MODEL)),
        "pre_ffn_g": gain((DEPTH, D_MODEL)),
        "post_ffn_g": gain((DEPTH, D_MODEL)),
        "w_in": nrm((DEPTH, D_MODEL, IN_WIDTH), D_MODEL ** -0.5),
        "attn_sink": nrm((DEPTH, A_HEADS), 0.5),
        "q_norm_g": gain((DEPTH, B_Q_RANK)),
        "kv_norm_g": gain((DEPTH, B_KV_RANK)),
        "w_uq": nrm((DEPTH, B_Q_RANK, B_HEADS * (B_NOPE_DIM + B_ROPE_DIM)), B_Q_RANK ** -0.5),
        "w_ukv": nrm((DEPTH, B_KV_RANK, B_HEADS * (B_NOPE_DIM + B_V_DIM)), B_KV_RANK ** -0.5),
        "grp_a_g": gain((DEPTH, A_WIDTH)),
        "grp_b_g": gain((DEPTH, B_WIDTH)),
        "w_o": nrm((DEPTH, MIX_WIDTH, D_MODEL), MIX_WIDTH ** -0.5),
        "ffn_w1": nrm((n_dense, D_MODEL, D_FF), D_MODEL ** -0.5),
        "ffn_w3": nrm((n_dense, D_MODEL, D_FF), D_MODEL ** -0.5),
        "ffn_w2": nrm((n_dense, D_FF, D_MODEL), D_FF ** -0.5),
        "router_w": nrm((n_moe, D_MODEL, N_EXPERTS), D_MODEL ** -0.5),
        "moe_w1": nrm((n_moe, N_EXPERTS, D_MODEL, D_FF), D_MODEL ** -0.5),
        "moe_w3": nrm((n_moe, N_EXPERTS, D_MODEL, D_FF), D_MODEL ** -0.5),
        "moe_w2": nrm((n_moe, N_EXPERTS, D_FF, D_MODEL), D_FF ** -0.5),
    }


def reference(x, c, ctx, c_ctx, w_ada, b_ada, pre_attn_g, post_attn_g, pre_ffn_g, post_ffn_g, w_in, attn_sink,
              q_norm_g, kv_norm_g, w_uq, w_ukv, grp_a_g, grp_b_g, w_o, ffn_w1, ffn_w3, ffn_w2, router_w,
              moe_w1, moe_w3, moe_w2):
    b, s, _ = x.shape
    n_ctx = ctx.shape[1]
    rows = s // GRID_W
    cos_a, sin_a = axial_rope_tables(rows, A_HEAD_DIM, x.dtype)
    cos_b, sin_b = axial_rope_tables(rows, B_ROPE_DIM, x.dtype)
    xc = ctx
    for i in range(DEPTH):
        last = i == DEPTH - 1
        mod = jax.nn.silu(c) @ w_ada[i] + b_ada[i]
        mod_c = jax.nn.silu(c_ctx) @ w_ada[i] + b_ada[i]
        sh_a, sc_a, g_a, sh_f, sc_f, g_f = jnp.split(mod[:, None, :], 6, axis=-1)
        csh_a, csc_a, cg_a, csh_f, csc_f, cg_f = jnp.split(mod_c, 6, axis=-1)

        h = modulate(rms_norm(x, pre_attn_g[i]), sh_a, sc_a)
        hc = modulate(rms_norm(xc, pre_attn_g[i]), csh_a, csc_a)
        proj = h @ w_in[i]
        qa, cq = split_q(proj[..., :Q_WIDTH])
        ka, va, ckv, kr = split_kv(proj[..., Q_WIDTH:])
        ka_c, va_c, ckv_c, kr_c = split_kv(hc @ w_in[i][:, Q_WIDTH:])
        qa = apply_axial_rope(qa.reshape(b, s, A_HEADS, A_HEAD_DIM), cos_a, sin_a)
        ka = apply_axial_rope(ka.reshape(b, s, A_KV_HEADS, A_HEAD_DIM), cos_a, sin_a)
        va = va.reshape(b, s, A_KV_HEADS, A_HEAD_DIM)
        ka_c = ka_c.reshape(b, n_ctx, A_KV_HEADS, A_HEAD_DIM)
        va_c = va_c.reshape(b, n_ctx, A_KV_HEADS, A_HEAD_DIM)
        o_a = windowed_gqa_latent(qa, ka, va, ka_c, va_c, attn_sink[i])
        qn, qr = mla_queries(cq, q_norm_g[i], w_uq[i])
        qr = apply_axial_rope(qr, cos_b, sin_b)
        kn, vb = mla_keys_values(ckv, kv_norm_g[i], w_ukv[i])
        kr = apply_axial_rope(kr[:, :, None, :], cos_b, sin_b)[:, :, 0]
        kn_c, vb_c = mla_keys_values(ckv_c, kv_norm_g[i], w_ukv[i])
        o_b = mla_attend(qn, qr, jnp.concatenate([kn_c, kn], axis=1), jnp.concatenate([kr_c, kr], axis=1),
                         jnp.concatenate([vb_c, vb], axis=1))
        x_next = x + g_a * rms_norm(mix_out(o_a, o_b, grp_a_g[i], grp_b_g[i], w_o[i]), post_attn_g[i])
        if not last:
            qa_c, cq_c = split_q(hc @ w_in[i][:, :Q_WIDTH])
            o_a_c = context_gqa(qa_c.reshape(b, n_ctx, A_HEADS, A_HEAD_DIM), ka_c, va_c, attn_sink[i])
            qn_c, qr_c = mla_queries(cq_c, q_norm_g[i], w_uq[i])
            o_b_c = mla_attend(qn_c, qr_c, kn_c, kr_c, vb_c)
            xc = xc + cg_a * rms_norm(mix_out(o_a_c, o_b_c, grp_a_g[i], grp_b_g[i], w_o[i]), post_attn_g[i])
        x = x_next

        h = modulate(rms_norm(x, pre_ffn_g[i]), sh_f, sc_f)
        if not last:
            hc = modulate(rms_norm(xc, pre_ffn_g[i]), csh_f, csc_f)
            h = jnp.concatenate([hc, h], axis=1)
        if i % 2 == 0:
            j = i // 2
            y = swiglu(h, ffn_w1[j], ffn_w3[j], ffn_w2[j])
        else:
            j = i // 2
            y = moe_swiglu(h, router_w[j], moe_w1[j], moe_w3[j], moe_w2[j])
        y = rms_norm(y, post_ffn_g[i])
        if last:
            x = x + g_f * y
        else:
            xc = xc + cg_f * y[:, :n_ctx]
            x = x + g_f * y[:, n_ctx:]
    return x
```

```python
import functools
import math

import jax
import jax.numpy as jnp
from jax import lax
from jax.experimental import pallas as pl
from jax.experimental.pallas import tpu as pltpu

GRID_W = 64
EPS = 1e-6
NEG_INF = -1e30
ROPE_BASE = 10000.0
A_HEADS = 8
A_KV_HEADS = 2
A_HEAD_DIM = 128
WINDOW = 128
B_HEADS = 8
B_Q_RANK = 384
B_KV_RANK = 256
B_NOPE_DIM = 128
B_ROPE_DIM = 64
B_V_DIM = 128
N_EXPERTS = 8
A_WIDTH = A_HEADS * A_HEAD_DIM
A_KV_WIDTH = A_KV_HEADS * A_HEAD_DIM
B_WIDTH = B_HEADS * B_V_DIM
B_QK_PAD = 256

LOG2E = math.log2(math.e)
LANE = 128
VMEM_LIMIT_BYTES = 60 * 1024 * 1024

F32 = jnp.float32
BF16 = jnp.bfloat16
NT_DIMS = (((1,), (1,)), ((), ()))


def _params(sem, vmem=VMEM_LIMIT_BYTES):
    return pltpu.CompilerParams(dimension_semantics=sem, vmem_limit_bytes=vmem)


def _const_spec(shape):
    nd = len(shape)
    return pl.BlockSpec(shape, lambda *_: (0,) * nd, pipeline_mode=pl.Buffered(1))


def _rms(x):
    return x * lax.rsqrt(jnp.mean(x * x, axis=-1, keepdims=True) + EPS)


def _norm_mod(x, g, mod_ref, k_shift, is_ctx):
    d = x.shape[-1]
    shift = jnp.where(is_ctx, mod_ref[1:2, k_shift * d:(k_shift + 1) * d], mod_ref[0:1, k_shift * d:(k_shift + 1) * d])
    scale = jnp.where(is_ctx, mod_ref[1:2, (k_shift + 1) * d:(k_shift + 2) * d],
                      mod_ref[0:1, (k_shift + 1) * d:(k_shift + 2) * d])
    return _rms(x) * g * (1.0 + scale) + shift


def _is_ctx_rows(tile_index, tm, n_lat):
    row = tile_index * tm + lax.broadcasted_iota(jnp.int32, (tm, 1), 0)
    return row >= n_lat


def _ada_kernel(c_ref, w_ref, b_ref, o_ref):
    c = c_ref[...]
    a = c * (1.0 / (1.0 + jnp.exp(-c)))
    o_ref[...] = jnp.dot(a, w_ref[...], preferred_element_type=F32, precision=lax.Precision.HIGHEST) + b_ref[...]


def _ada(cc, w_ada, b_ada):
    depth, d, n = w_ada.shape
    tn = n // 8
    return pl.pallas_call(
        _ada_kernel,
        grid=(depth, n // tn),
        in_specs=[pl.BlockSpec((8, d), lambda l, j: (0, 0)),
                  pl.BlockSpec((None, d, tn), lambda l, j: (l, 0, j)),
                  pl.BlockSpec((None, 1, tn), lambda l, j: (l, 0, j))],
        out_specs=pl.BlockSpec((None, 8, tn), lambda l, j: (l, 0, j)),
        out_shape=jax.ShapeDtypeStruct((depth, 8, n), F32),
        compiler_params=_params(("arbitrary", "arbitrary")),
        name="ada_mod",
    )(cc, w_ada, b_ada.reshape(depth, 1, n))


def _rope(t, cos, sin_m, sin_p, quarter):
    n = t.shape[-1]
    return t * cos + pltpu.roll(t, n - quarter, 1) * sin_m + pltpu.roll(t, quarter, 1) * sin_p


def _proj_kernel(x_ref, mod_ref, g_ref, win_ref, qg_ref, kvg_ref, wuq_ref, wukn_ref, wuvt_ref, rope_ref,
                 qa_ref, ka_ref, va_ref, qb_ref, kb_ref, vbt_ref, *, tm, n_lat, scale_a, scale_b):
    i = pl.program_id(0)
    is_ctx = _is_ctx_rows(i, tm, n_lat)
    h = _norm_mod(x_ref[...], g_ref[...], mod_ref, 0, is_ctx).astype(BF16)
    proj = jnp.dot(h, win_ref[...], preferred_element_type=F32)
    cos_a, sinm_a, sinp_a = rope_ref[0], rope_ref[1], rope_ref[2]
    cos_b, sinm_b, sinp_b = rope_ref[3], rope_ref[4], rope_ref[5]
    qa_q = A_HEAD_DIM // 4
    b_q = B_ROPE_DIM // 4
    cos_as, sinm_as, sinp_as = cos_a * scale_a, sinm_a * scale_a, sinp_a * scale_a
    for hh in range(A_HEADS):
        t = proj[:, hh * LANE:(hh + 1) * LANE]
        qa_ref[:, hh * LANE:(hh + 1) * LANE] = _rope(t, cos_as, sinm_as, sinp_as, qa_q).astype(BF16)
    o = A_WIDTH
    for hh in range(A_KV_HEADS):
        t = proj[:, o + hh * LANE:o + (hh + 1) * LANE]
        ka_ref[:, hh * LANE:(hh + 1) * LANE] = _rope(t, cos_a, sinm_a, sinp_a, qa_q).astype(BF16)
    o += A_KV_WIDTH
    va_ref[...] = proj[:, o:o + A_KV_WIDTH].astype(BF16)
    o += A_KV_WIDTH
    cq = (_rms(proj[:, o:o + B_Q_RANK]) * qg_ref[...]).astype(BF16)
    o += B_Q_RANK
    qm = jnp.dot(cq, wuq_ref[...], preferred_element_type=F32)
    cos_bs, sinm_bs, sinp_bs = cos_b * scale_b, sinm_b * scale_b, sinp_b * scale_b
    for hh in range(B_HEADS):
        b0 = hh * B_QK_PAD
        qb_ref[hh, :, 0:LANE] = (qm[:, b0:b0 + LANE] * scale_b).astype(BF16)
        qb_ref[hh, :, LANE:2 * LANE] = _rope(qm[:, b0 + LANE:b0 + 2 * LANE], cos_bs, sinm_bs, sinp_bs, b_q).astype(BF16)
    ckv = (_rms(proj[:, o:o + B_KV_RANK]) * kvg_ref[...]).astype(BF16)
    o += B_KV_RANK
    kr = _rope(proj[:, o:o + LANE], cos_b, sinm_b, sinp_b, b_q).astype(BF16)
    kn = jnp.dot(ckv, wukn_ref[...], preferred_element_type=F32)
    vt = lax.dot_general(wuvt_ref[...], ckv, NT_DIMS, preferred_element_type=F32)
    for hh in range(B_HEADS):
        kb_ref[hh, :, 0:LANE] = kn[:, hh * LANE:(hh + 1) * LANE].astype(BF16)
        kb_ref[hh, :, LANE:2 * LANE] = kr
        vbt_ref[hh] = vt[hh * B_V_DIM:(hh + 1) * B_V_DIM, :].astype(BF16)


def _proj(xs, mod, g, win, qg, kvg, wuq, wukn, wuvt, rope_tab, *, n_lat, tm):
    t_all, d = xs.shape
    row = lambda i: (i, 0)
    kern = functools.partial(_proj_kernel, tm=tm, n_lat=n_lat, scale_a=LOG2E / math.sqrt(A_HEAD_DIM),
                             scale_b=LOG2E / math.sqrt(B_NOPE_DIM + B_ROPE_DIM))
    return pl.pallas_call(
        kern,
        grid=(t_all // tm,),
        in_specs=[pl.BlockSpec((tm, d), row), _const_spec(mod.shape), _const_spec(g.shape), _const_spec(win.shape),
                  _const_spec(qg.shape), _const_spec(kvg.shape), _const_spec(wuq.shape), _const_spec(wukn.shape),
                  _const_spec(wuvt.shape), pl.BlockSpec((6, tm, LANE), lambda i: (0, i, 0))],
        out_specs=[pl.BlockSpec((tm, A_WIDTH), row), pl.BlockSpec((tm, A_KV_WIDTH), row),
                   pl.BlockSpec((tm, A_KV_WIDTH), row),
                   pl.BlockSpec((B_HEADS, tm, B_QK_PAD), lambda i: (0, i, 0)),
                   pl.BlockSpec((B_HEADS, tm, B_QK_PAD), lambda i: (0, i, 0)),
                   pl.BlockSpec((B_HEADS, B_V_DIM, tm), lambda i: (0, 0, i))],
        out_shape=[jax.ShapeDtypeStruct((t_all, A_WIDTH), BF16), jax.ShapeDtypeStruct((t_all, A_KV_WIDTH), BF16),
                   jax.ShapeDtypeStruct((t_all, A_KV_WIDTH), BF16),
                   jax.ShapeDtypeStruct((B_HEADS, t_all, B_QK_PAD), BF16),
                   jax.ShapeDtypeStruct((B_HEADS, t_all, B_QK_PAD), BF16),
                   jax.ShapeDtypeStruct((B_HEADS, B_V_DIM, t_all), BF16)],
        compiler_params=_params(("arbitrary",)),
        name="pre_attn_proj",
    )(xs, mod, g, win, qg, kvg, wuq, wukn, wuvt, rope_tab)


def _attn_a_kernel(q_ref, k_ref, v_ref, sink_ref, o_ref, *, tq, n_lat, n_ctx):
    i = pl.program_id(0)
    n_lat_tiles = n_lat // tq
    win = tq + 2 * WINDOW
    grp = A_HEADS // A_KV_HEADS
    kc = k_ref[n_lat:n_lat + n_ctx, :]
    vc = v_ref[n_lat:n_lat + n_ctx, :]

    def head_out(hh, kw, vw, mask):
        g = hh // grp
        gs = slice(g * A_HEAD_DIM, (g + 1) * A_HEAD_DIM)
        q = q_ref[:, hh * A_HEAD_DIM:(hh + 1) * A_HEAD_DIM]
        sink = sink_ref[hh:hh + 1, 0:1]
        s_ctx = lax.dot_general(q, kc[:, gs], NT_DIMS, preferred_element_type=F32)
        m = jnp.maximum(jnp.max(s_ctx, axis=-1, keepdims=True), sink)
        if kw is not None:
            s_loc = lax.dot_general(q, kw[:, gs], NT_DIMS, preferred_element_type=F32)
            s_loc = jnp.where(mask, s_loc, NEG_INF)
            m = jnp.maximum(m, jnp.max(s_loc, axis=-1, keepdims=True))
        p_ctx = jnp.exp2(s_ctx - m)
        den = jnp.sum(p_ctx, axis=-1, keepdims=True) + jnp.exp2(sink - m)
        acc = jnp.dot(p_ctx.astype(BF16), vc[:, gs], preferred_element_type=F32)
        if kw is not None:
            p_loc = jnp.exp2(s_loc - m)
            den = den + jnp.sum(p_loc, axis=-1, keepdims=True)
            acc = acc + jnp.dot(p_loc.astype(BF16), vw[:, gs], preferred_element_type=F32)
        o_ref[:, hh * A_HEAD_DIM:(hh + 1) * A_HEAD_DIM] = (acc * (1.0 / den)).astype(o_ref.dtype)

    @pl.when(i < n_lat_tiles)
    def _latent():
        q0 = i * tq
        ws = pl.multiple_of(jnp.clip(q0 - WINDOW, 0, n_lat - win), WINDOW)
        kw = k_ref[pl.ds(ws, win), :]
        vw = v_ref[pl.ds(ws, win), :]
        qpos = q0 + lax.broadcasted_iota(jnp.int32, (tq, win), 0)
        kpos = ws + lax.broadcasted_iota(jnp.int32, (tq, win), 1)
        mask = jnp.abs(qpos - kpos) <= WINDOW
        for hh in range(A_HEADS):
            head_out(hh, kw, vw, mask)

    @pl.when(i >= n_lat_tiles)
    def _context():
        for hh in range(A_HEADS):
            head_out(hh, None, None, None)


def _attn_a(qa, ka, va, sink_tab, *, n_lat, n_ctx, tq):
    t_all = qa.shape[0]
    kern = functools.partial(_attn_a_kernel, tq=tq, n_lat=n_lat, n_ctx=n_ctx)
    return pl.pallas_call(
        kern,
        grid=(t_all // tq,),
        in_specs=[pl.BlockSpec((tq, A_WIDTH), lambda i: (i, 0)), _const_spec(ka.shape), _const_spec(va.shape),
                  _const_spec(sink_tab.shape)],
        out_specs=pl.BlockSpec((tq, A_WIDTH), lambda i: (i, 0)),
        out_shape=jax.ShapeDtypeStruct((t_all, A_WIDTH), BF16),
        compiler_params=_params(("arbitrary",)),
        name="attn_a",
    )(qa, ka, va, sink_tab)


def _mla_kernel(*refs, nk, has_prev):
    if has_prev:
        q_ref, k_ref, vt_ref, _, o_ref, acc_ref, m_ref, l_ref = refs
    else:
        q_ref, k_ref, vt_ref, o_ref, acc_ref, m_ref, l_ref = refs
    j = pl.program_id(1)

    @pl.when(j == 0)
    def _init():
        m_ref[...] = jnp.full(m_ref.shape, NEG_INF, F32)
        l_ref[...] = jnp.zeros(l_ref.shape, F32)
        acc_ref[...] = jnp.zeros(acc_ref.shape, F32)

    def head(hh, carry):
        st = lax.dot_general(k_ref[hh], q_ref[hh], NT_DIMS, preferred_element_type=F32)
        m_old = m_ref[hh]
        m_new = jnp.maximum(m_old, jnp.max(st, axis=0, keepdims=True))
        alpha = jnp.exp2(m_old - m_new)
        p = jnp.exp2(st - m_new)
        l_ref[hh] = alpha * l_ref[hh] + jnp.sum(p, axis=0, keepdims=True)
        pv = jnp.dot(vt_ref[hh], p.astype(BF16), preferred_element_type=F32)
        acc_ref[hh] = alpha * acc_ref[hh] + pv
        m_ref[hh] = m_new
        return carry

    lax.fori_loop(0, B_HEADS, head, 0)

    @pl.when(j == nk - 1)
    def _finish():
        for hh in range(B_HEADS):
            o = acc_ref[hh] * (1.0 / l_ref[hh])
            o_ref[:, hh * B_V_DIM:(hh + 1) * B_V_DIM] = o.T.astype(o_ref.dtype)


def _mla(qb, kb, vbt, prev, *, q_row0, n_q, k_row0, n_k, tq, tk):
    t_all = qb.shape[1]
    nq, nk = n_q // tq, n_k // tk
    qo, ko = q_row0 // tq, k_row0 // tk
    kern = functools.partial(_mla_kernel, nk=nk, has_prev=prev is not None)
    in_specs = [pl.BlockSpec((B_HEADS, tq, B_QK_PAD), lambda i, j: (0, qo + i, 0)),
                pl.BlockSpec((B_HEADS, tk, B_QK_PAD), lambda i, j: (0, ko + j, 0)),
                pl.BlockSpec((B_HEADS, B_V_DIM, tk), lambda i, j: (0, 0, ko + j))]
    args = [qb, kb, vbt]
    aliases = {}
    if prev is not None:
        in_specs.append(pl.BlockSpec(memory_space=pl.ANY))
        args.append(prev)
        aliases = {3: 0}
    return pl.pallas_call(
        kern,
        grid=(nq, nk),
        in_specs=in_specs,
        out_specs=pl.BlockSpec((tq, B_WIDTH), lambda i, j: (qo + i, 0)),
        out_shape=jax.ShapeDtypeStruct((t_all, B_WIDTH), BF16),
        scratch_shapes=[pltpu.VMEM((B_HEADS, B_V_DIM, tq), F32), pltpu.VMEM((B_HEADS, 1, tq), F32),
                        pltpu.VMEM((B_HEADS, 1, tq), F32)],
        input_output_aliases=aliases,
        compiler_params=_params(("arbitrary", "arbitrary")),
        name="mla_attn",
    )(*args)


def _out_kernel(oa_ref, ob_ref, x_ref, mod_ref, ga_ref, gb_ref, wo_ref, pg_ref, o_ref, *, tm, n_lat):
    i = pl.program_id(0)
    is_ctx = _is_ctx_rows(i, tm, n_lat)
    d = x_ref.shape[-1]
    a = (_rms(oa_ref[...].astype(F32)) * ga_ref[...]).astype(BF16)
    b = (_rms(ob_ref[...].astype(F32)) * gb_ref[...]).astype(BF16)
    y = (jnp.dot(a, wo_ref[0:A_WIDTH, :], preferred_element_type=F32)
         + jnp.dot(b, wo_ref[A_WIDTH:A_WIDTH + B_WIDTH, :], preferred_element_type=F32))
    gate = jnp.where(is_ctx, mod_ref[1:2, 2 * d:3 * d], mod_ref[0:1, 2 * d:3 * d])
    o_ref[...] = x_ref[...] + gate * (_rms(y) * pg_ref[...])


def _attn_out(oa, ob, xs, mod, ga, gb, wo, pg, *, n_lat, tm):
    t_all, d = xs.shape
    row = lambda i: (i, 0)
    kern = functools.partial(_out_kernel, tm=tm, n_lat=n_lat)
    return pl.pallas_call(
        kern,
        grid=(t_all // tm,),
        in_specs=[pl.BlockSpec((tm, A_WIDTH), row), pl.BlockSpec((tm, B_WIDTH), row), pl.BlockSpec((tm, d), row),
                  _const_spec(mod.shape), _const_spec(ga.shape), _const_spec(gb.shape), _const_spec(wo.shape),
                  _const_spec(pg.shape)],
        out_specs=pl.BlockSpec((tm, d), row),
        out_shape=jax.ShapeDtypeStruct((t_all, d), F32),
        compiler_params=_params(("arbitrary",)),
        name="attn_out_mix",
    )(oa, ob, xs, mod, ga, gb, wo, pg)


def _swiglu_step(x_ref, mod_ref, g_ref, w1_ref, w3_ref, w2_ref, o_ref, h_ref, is_ctx):
    j = pl.program_id(1)

    @pl.when(j == 0)
    def _prologue():
        h_ref[...] = _norm_mod(x_ref[...], g_ref[...], mod_ref, 3, is_ctx).astype(BF16)

    h = h_ref[...]
    a = jnp.dot(h, w1_ref[...], preferred_element_type=F32)
    b = jnp.dot(h, w3_ref[...], preferred_element_type=F32)
    u = (a * (1.0 / (1.0 + jnp.exp(-a))) * b).astype(BF16)
    y = jnp.dot(u, w2_ref[...], preferred_element_type=F32)

    @pl.when(j == 0)
    def _first():
        o_ref[...] = y

    @pl.when(j > 0)
    def _rest():
        o_ref[...] += y


def _ffn_dense_kernel(x_ref, mod_ref, g_ref, pg_ref, w1_ref, w3_ref, w2_ref, o_ref, h_ref, *, tm, n_lat, nj):
    i = pl.program_id(0)
    is_ctx = _is_ctx_rows(i, tm, n_lat)
    _swiglu_step(x_ref, mod_ref, g_ref, w1_ref, w3_ref, w2_ref, o_ref, h_ref, is_ctx)

    @pl.when(pl.program_id(1) == nj - 1)
    def _epilogue():
        d = x_ref.shape[-1]
        gate = jnp.where(is_ctx, mod_ref[1:2, 5 * d:6 * d], mod_ref[0:1, 5 * d:6 * d])
        o_ref[...] = x_ref[...] + gate * (_rms(o_ref[...]) * pg_ref[...])


def _ffn_dense(xs, mod, g, pg, w1, w3, w2, *, n_lat, tm, tf):
    t_all, d = xs.shape
    dff = w1.shape[1]
    nj = dff // tf
    kern = functools.partial(_ffn_dense_kernel, tm=tm, n_lat=n_lat, nj=nj)
    return pl.pallas_call(
        kern,
        grid=(t_all // tm, nj),
        in_specs=[pl.BlockSpec((tm, d), lambda i, j: (i, 0)), _const_spec(mod.shape), _const_spec(g.shape),
                  _const_spec(pg.shape),
                  pl.BlockSpec((d, tf), lambda i, j: (0, j)), pl.BlockSpec((d, tf), lambda i, j: (0, j)),
                  pl.BlockSpec((tf, d), lambda i, j: (j, 0))],
        out_specs=pl.BlockSpec((tm, d), lambda i, j: (i, 0)),
        out_shape=jax.ShapeDtypeStruct((t_all, d), F32),
        scratch_shapes=[pltpu.VMEM((tm, d), BF16)],
        compiler_params=_params(("arbitrary", "arbitrary")),
        name="ffn_dense",
    )(xs, mod, g, pg, w1, w3, w2)


def _ffn_moe_kernel(te_ref, nu_ref, x_ref, mod_ref, g_ref, w1_ref, w3_ref, w2_ref, o_ref, h_ref):
    used = pl.program_id(0) < nu_ref[0]

    @pl.when(used)
    def _used():
        _swiglu_step(x_ref, mod_ref, g_ref, w1_ref, w3_ref, w2_ref, o_ref, h_ref, False)

    @pl.when(jnp.logical_not(used) & (pl.program_id(1) == 0))
    def _idle():
        o_ref[...] = jnp.zeros(o_ref.shape, o_ref.dtype)


def _ffn_moe(xg, tile_expert, n_used, mod, g, w1, w3, w2, *, tm, tf):
    p_rows, d = xg.shape
    dff = w1.shape[2]
    nj = dff // tf

    def t_eff(t, nu):
        return jnp.minimum(t, nu[0] - 1)

    def row_map(t, j, te, nu):
        return (t_eff(t, nu), 0)

    def j_eff(t, j, nu):
        return jnp.where(t < nu[0], j, nj - 1)

    grid_spec = pltpu.PrefetchScalarGridSpec(
        num_scalar_prefetch=2,
        grid=(p_rows // tm, nj),
        in_specs=[pl.BlockSpec((tm, d), row_map),
                  pl.BlockSpec(mod.shape, lambda t, j, te, nu: (0, 0), pipeline_mode=pl.Buffered(1)),
                  pl.BlockSpec(g.shape, lambda t, j, te, nu: (0, 0), pipeline_mode=pl.Buffered(1)),
                  pl.BlockSpec((None, d, tf), lambda t, j, te, nu: (te[t_eff(t, nu)], 0, j_eff(t, j, nu))),
                  pl.BlockSpec((None, d, tf), lambda t, j, te, nu: (te[t_eff(t, nu)], 0, j_eff(t, j, nu))),
                  pl.BlockSpec((None, tf, d), lambda t, j, te, nu: (te[t_eff(t, nu)], j_eff(t, j, nu), 0))],
        out_specs=pl.BlockSpec((tm, d), lambda t, j, te, nu: (t, 0)),
        scratch_shapes=[pltpu.VMEM((tm, d), BF16)],
    )
    return pl.pallas_call(
        _ffn_moe_kernel,
        grid_spec=grid_spec,
        out_shape=jax.ShapeDtypeStruct((p_rows, d), F32),
        compiler_params=_params(("arbitrary", "arbitrary")),
        name="ffn_moe",
    )(tile_expert, n_used, xg, mod, g, w1, w3, w2)


def _route_kernel(x_ref, mod_ref, g_ref, rw_ref, o_ref):
    h = _norm_mod(x_ref[...], g_ref[...], mod_ref, 3, False)
    logits = jnp.dot(h, rw_ref[...], preferred_element_type=F32, precision=lax.Precision.HIGHEST)
    lane = lax.broadcasted_iota(jnp.int32, logits.shape, 1)
    lg = jnp.where(lane < N_EXPERTS, logits, -jnp.inf)
    m1 = jnp.max(lg, axis=-1, keepdims=True)
    i1 = jnp.min(jnp.where(lg == m1, lane, LANE), axis=-1, keepdims=True)
    lg2 = jnp.where(lane == i1, -jnp.inf, lg)
    m2 = jnp.max(lg2, axis=-1, keepdims=True)
    i2 = jnp.min(jnp.where(lg2 == m2, lane, LANE), axis=-1, keepdims=True)
    e = jnp.exp(m2 - m1)
    w1 = 1.0 / (1.0 + e)
    w2 = e * w1
    o_ref[...] = jnp.where(lane == 0, i1.astype(F32),
                           jnp.where(lane == 1, i2.astype(F32), jnp.where(lane == 2, w1, jnp.where(lane == 3, w2, 0.0))))


def _route(xs, mod, g, rw_pad, *, n_lat, tm):
    d = xs.shape[1]
    return pl.pallas_call(
        _route_kernel,
        grid=(n_lat // tm,),
        in_specs=[pl.BlockSpec((tm, d), lambda i: (i, 0)), _const_spec(mod.shape), _const_spec(g.shape),
                  _const_spec(rw_pad.shape)],
        out_specs=pl.BlockSpec((tm, LANE), lambda i: (i, 0)),
        out_shape=jax.ShapeDtypeStruct((n_lat, LANE), F32),
        compiler_params=_params(("arbitrary",)),
        name="moe_route",
    )(xs, mod, g, rw_pad)


def _dispatch_kernel(dest_ref, x_hbm, zero_hbm, o_hbm, sem, *, tm):
    del zero_hbm
    i = pl.program_id(0)

    def row_copy(r, k):
        return pltpu.make_async_copy(x_hbm.at[pl.ds(i * tm + r, 1)], o_hbm.at[pl.ds(dest_ref[0, k, r], 1)], sem)

    def start(r, c):
        row_copy(r, 0).start()
        row_copy(r, 1).start()
        return c

    def wait(r, c):
        row_copy(r, 0).wait()
        row_copy(r, 1).wait()
        return c

    lax.fori_loop(0, tm, start, 0)
    lax.fori_loop(0, tm, wait, 0)


def _dispatch(xs, dest, p_rows, *, n_lat, tm):
    d = xs.shape[1]
    kern = functools.partial(_dispatch_kernel, tm=tm)
    return pl.pallas_call(
        kern,
        grid=(n_lat // tm,),
        in_specs=[pl.BlockSpec((1, 2, tm), lambda i: (i, 0, 0), memory_space=pltpu.SMEM),
                  pl.BlockSpec(memory_space=pl.ANY), pl.BlockSpec(memory_space=pl.ANY)],
        out_specs=pl.BlockSpec(memory_space=pl.ANY),
        out_shape=jax.ShapeDtypeStruct((p_rows, d), F32),
        scratch_shapes=[pltpu.SemaphoreType.DMA(())],
        input_output_aliases={2: 0},
        compiler_params=_params(("arbitrary",)),
        name="moe_dispatch",
    )(dest, xs, jnp.zeros((p_rows, d), F32))


def _combine_kernel(dest_ref, rt_ref, x_ref, mod_ref, pg_ref, y_hbm, o_ref, buf, sem, *, tm):
    def row_copy(r, k):
        return pltpu.make_async_copy(y_hbm.at[pl.ds(dest_ref[0, k, r], 1)], buf.at[k, pl.ds(r, 1)], sem)

    def start(r, c):
        row_copy(r, 0).start()
        row_copy(r, 1).start()
        return c

    def wait(r, c):
        row_copy(r, 0).wait()
        row_copy(r, 1).wait()
        return c

    lax.fori_loop(0, tm, start, 0)
    lax.fori_loop(0, tm, wait, 0)
    d = x_ref.shape[-1]
    rt = rt_ref[...]
    y = rt[:, 2:3] * buf[0] + rt[:, 3:4] * buf[1]
    o_ref[...] = x_ref[...] + mod_ref[0:1, 5 * d:6 * d] * (_rms(y) * pg_ref[...])


def _combine(yg, dest, route, xs, mod, pg, *, n_lat, tm):
    d = xs.shape[1]
    kern = functools.partial(_combine_kernel, tm=tm)
    return pl.pallas_call(
        kern,
        grid=(n_lat // tm,),
        in_specs=[pl.BlockSpec((1, 2, tm), lambda i: (i, 0, 0), memory_space=pltpu.SMEM),
                  pl.BlockSpec((tm, LANE), lambda i: (i, 0)), pl.BlockSpec((tm, d), lambda i: (i, 0)),
                  _const_spec(mod.shape), _const_spec(pg.shape), pl.BlockSpec(memory_space=pl.ANY)],
        out_specs=pl.BlockSpec((tm, d), lambda i: (i, 0)),
        out_shape=jax.ShapeDtypeStruct((n_lat, d), F32),
        scratch_shapes=[pltpu.VMEM((2, tm, d), F32), pltpu.SemaphoreType.DMA(())],
        compiler_params=_params(("arbitrary",)),
        name="moe_combine",
    )(dest, route, xs, mod, pg, yg)


def _rope_tables(n_lat, n_ctx):
    rows = n_lat // GRID_W
    row = jnp.repeat(jnp.arange(rows, dtype=F32), GRID_W)
    col = jnp.tile(jnp.arange(GRID_W, dtype=F32), rows)

    def tabs(rot_dim):
        axis_dim = rot_dim // 2
        inv_freq = ROPE_BASE ** (-jnp.arange(0, axis_dim, 2, dtype=F32) / axis_dim)
        ang_r = row[:, None] * inv_freq[None, :]
        ang_c = col[:, None] * inv_freq[None, :]
        ang = jnp.concatenate([ang_r, ang_r, ang_c, ang_c], axis=-1)
        cos, sin = jnp.cos(ang), jnp.sin(ang)
        quarter = rot_dim // 4
        lower = (jnp.arange(rot_dim) % (2 * quarter)) < quarter
        sin_m = jnp.where(lower[None, :], -sin, 0.0)
        sin_p = jnp.where(lower[None, :], 0.0, sin)
        pad = LANE - rot_dim
        cos = jnp.pad(cos, ((0, n_ctx), (0, pad)), constant_values=1.0)
        sin_m = jnp.pad(sin_m, ((0, n_ctx), (0, pad)))
        sin_p = jnp.pad(sin_p, ((0, n_ctx), (0, pad)))
        return [cos, sin_m, sin_p]

    return jnp.stack(tabs(A_HEAD_DIM) + tabs(B_ROPE_DIM))


def _layer_weights(i, w_in, w_uq, w_ukv, w_o):
    d = w_in.shape[1]
    q_w = A_WIDTH + B_Q_RANK
    wi = w_in[i]
    qa, cq = wi[:, :A_WIDTH], wi[:, A_WIDTH:q_w]
    o1, o2, o3 = q_w + A_KV_WIDTH, q_w + 2 * A_KV_WIDTH, q_w + 2 * A_KV_WIDTH + B_KV_RANK
    ka, va, ckv, kr = wi[:, q_w:o1], wi[:, o1:o2], wi[:, o2:o3], wi[:, o3:]
    win = jnp.concatenate([qa, ka, va, cq, ckv, kr, jnp.zeros((d, LANE - B_ROPE_DIM), F32)], axis=1).astype(BF16)
    uq = w_uq[i].reshape(B_Q_RANK, B_HEADS, B_NOPE_DIM + B_ROPE_DIM)
    uq = jnp.pad(uq, ((0, 0), (0, 0), (0, B_QK_PAD - B_NOPE_DIM - B_ROPE_DIM)))
    wuq = uq.reshape(B_Q_RANK, B_HEADS * B_QK_PAD).astype(BF16)
    ukv = w_ukv[i].reshape(B_KV_RANK, B_HEADS, B_NOPE_DIM + B_V_DIM)
    wukn = ukv[:, :, :B_NOPE_DIM].reshape(B_KV_RANK, B_HEADS * B_NOPE_DIM).astype(BF16)
    wuvt = ukv[:, :, B_NOPE_DIM:].reshape(B_KV_RANK, B_HEADS * B_V_DIM).T.astype(BF16)
    return win, wuq, wukn, wuvt, w_o[i].astype(BF16)


def _routing_slots(route, n_lat, tm_e, n_tiles):
    e12 = route[:, 0:2].astype(jnp.int32).T.reshape(-1)
    onehot = (e12[:, None] == jnp.arange(N_EXPERTS)[None, :]).astype(jnp.int32)
    csum = jnp.cumsum(onehot, axis=0)
    rank = jnp.sum(csum * onehot, axis=1) - 1
    counts = csum[-1]
    tiles_per = (counts + tm_e - 1) // tm_e
    tile_end = jnp.cumsum(tiles_per)
    start = (tile_end - tiles_per) * tm_e
    dest = (jnp.sum(start[None, :] * onehot, axis=1) + rank).astype(jnp.int32)
    n_used = tile_end[-1].astype(jnp.int32).reshape(1)
    t_idx = jnp.arange(n_tiles, dtype=jnp.int32)
    tile_expert = jnp.minimum(jnp.sum((t_idx[:, None] >= tile_end[None, :]).astype(jnp.int32), axis=1),
                              N_EXPERTS - 1).astype(jnp.int32)
    return dest.reshape(2, n_lat), tile_expert, n_used


def _tile(n, candidates):
    for c in candidates:
        if n % c == 0:
            return c
    raise ValueError(f"no supported tile size for {n} rows")


def kernel(x, c, ctx, c_ctx, w_ada, b_ada, pre_attn_g, post_attn_g, pre_ffn_g, post_ffn_g, w_in, attn_sink, q_norm_g,
           kv_norm_g, w_uq, w_ukv, grp_a_g, grp_b_g, w_o, ffn_w1, ffn_w3, ffn_w2, router_w, moe_w1, moe_w3, moe_w2):
    b, n_lat, d = x.shape
    n_ctx = ctx.shape[1]
    depth = w_ada.shape[0]
    assert b == 1 and c.shape[0] == 1, "single-sequence kernel"
    t_all = n_lat + n_ctx
    tm = _tile(t_all, (640, 256, 128))
    tq_a = 256
    tq_b = _tile(n_lat, (1024, 512, 256))
    tk_b = tm
    tf = 512
    tm_r = _tile(n_lat, (512, 256))
    assert n_ctx == tq_a and n_lat % tq_a == 0 and n_lat >= tq_a + 2 * WINDOW

    xs = jnp.concatenate([x[0], ctx[0]], axis=0)
    cc = jnp.zeros((8, d), F32).at[0].set(c[0]).at[1].set(c_ctx)
    mods = _ada(cc, w_ada, b_ada)
    rope_tab = _rope_tables(n_lat, n_ctx)
    row2 = lambda v: v.reshape(1, -1)

    for i in range(depth):
        last = i == depth - 1
        mod = mods[i]
        win, wuq, wukn, wuvt, wo = _layer_weights(i, w_in, w_uq, w_ukv, w_o)
        qa, ka, va, qb, kb, vbt = _proj(xs, mod, row2(pre_attn_g[i]), win, row2(q_norm_g[i]), row2(kv_norm_g[i]),
                                        wuq, wukn, wuvt, rope_tab, n_lat=n_lat, tm=tm)
        sink_tab = jnp.broadcast_to((attn_sink[i] * LOG2E)[:, None], (A_HEADS, LANE))
        oa = _attn_a(qa, ka, va, sink_tab, n_lat=n_lat, n_ctx=n_ctx, tq=tq_a)
        ob = _mla(qb, kb, vbt, jnp.zeros((t_all, B_WIDTH), BF16), q_row0=0, n_q=n_lat, k_row0=0, n_k=t_all,
                  tq=tq_b, tk=tk_b)
        ob = _mla(qb, kb, vbt, ob, q_row0=n_lat, n_q=n_ctx, k_row0=n_lat, n_k=n_ctx, tq=n_ctx, tk=n_ctx)
        xs = _attn_out(oa, ob, xs, mod, row2(grp_a_g[i]), row2(grp_b_g[i]), wo, row2(post_attn_g[i]),
                       n_lat=n_lat, tm=tm)
        jj = i // 2
        if i % 2 == 0:
            xs = _ffn_dense(xs, mod, row2(pre_ffn_g[i]), row2(post_ffn_g[i]), ffn_w1[jj].astype(BF16),
                            ffn_w3[jj].astype(BF16), ffn_w2[jj].astype(BF16), n_lat=n_lat, tm=tm, tf=tf)
        else:
            n_rows = t_all if not last else n_lat
            assert last, "expert layers other than the last would also need the context rows routed"
            tm_e = tm
            n_tiles = (2 * n_rows + N_EXPERTS * (tm_e - 1)) // tm_e
            rw_pad = jnp.pad(router_w[jj], ((0, 0), (0, LANE - N_EXPERTS)))
            route = _route(xs, mod, row2(pre_ffn_g[i]), rw_pad, n_lat=n_rows, tm=tm_r)
            dest, tile_expert, n_used = _routing_slots(route, n_rows, tm_e, n_tiles)
            dest_t = dest.reshape(2, n_rows // tm_r, tm_r).transpose(1, 0, 2)
            xg = _dispatch(xs, dest_t, n_tiles * tm_e, n_lat=n_rows, tm=tm_r)
            yg = _ffn_moe(xg, tile_expert, n_used, mod, row2(pre_ffn_g[i]), moe_w1[jj].astype(BF16),
                          moe_w3[jj].astype(BF16), moe_w2[jj].astype(BF16), tm=tm_e, tf=tf)
            xs = _combine(yg, dest_t, route, xs, mod, row2(post_ffn_g[i]), n_lat=n_rows, tm=tm_r)
    return xs[:n_lat].reshape(b, n_lat, d)
```

```python
import functools
import math

import jax
import jax.numpy as jnp
from jax import lax
from jax.experimental import pallas as pl
from jax.experimental.pallas import tpu as pltpu

GRID_W = 64
EPS = 1e-6
NEG_INF = -1e30
ROPE_BASE = 10000.0
A_HEADS = 8
A_KV_HEADS = 2
A_HEAD_DIM = 128
WINDOW = 128
B_HEADS = 8
B_Q_RANK = 384
B_KV_RANK = 256
B_NOPE_DIM = 128
B_ROPE_DIM = 64
B_V_DIM = 128
N_EXPERTS = 8
A_WIDTH = A_HEADS * A_HEAD_DIM
A_KV_WIDTH = A_KV_HEADS * A_HEAD_DIM
B_WIDTH = B_HEADS * B_V_DIM
B_QK_PAD = 256
B_VT_ROWS = B_V_DIM + 16

LOG2E = math.log2(math.e)
LANE = 128
VMEM_LIMIT_BYTES = 60 * 1024 * 1024

F32 = jnp.float32
BF16 = jnp.bfloat16
NT_DIMS = (((1,), (1,)), ((), ()))


def _params(sem, vmem=VMEM_LIMIT_BYTES):
    return pltpu.CompilerParams(dimension_semantics=sem, vmem_limit_bytes=vmem)


def _const_spec(shape):
    nd = len(shape)
    return pl.BlockSpec(shape, lambda *_: (0,) * nd, pipeline_mode=pl.Buffered(1))


def _rms(x):
    return x * lax.rsqrt(jnp.mean(x * x, axis=-1, keepdims=True) + EPS)


def _norm_mod(x, g, mod_ref, k_shift, is_ctx):
    d = x.shape[-1]
    shift = jnp.where(is_ctx, mod_ref[1:2, k_shift * d:(k_shift + 1) * d], mod_ref[0:1, k_shift * d:(k_shift + 1) * d])
    scale = jnp.where(is_ctx, mod_ref[1:2, (k_shift + 1) * d:(k_shift + 2) * d],
                      mod_ref[0:1, (k_shift + 1) * d:(k_shift + 2) * d])
    return _rms(x) * g * (1.0 + scale) + shift


def _is_ctx_rows(tile_index, tm, n_lat):
    row = tile_index * tm + lax.broadcasted_iota(jnp.int32, (tm, 1), 0)
    return row >= n_lat


def _ada_kernel(c_ref, w_ref, b_ref, o_ref):
    c = c_ref[...]
    a = c * (1.0 / (1.0 + jnp.exp(-c)))
    o_ref[...] = jnp.dot(a, w_ref[...], preferred_element_type=F32, precision=lax.Precision.HIGHEST) + b_ref[...]


def _ada(cc, w_ada, b_ada):
    depth, d, n = w_ada.shape
    tn = n // 8
    return pl.pallas_call(
        _ada_kernel,
        grid=(depth, n // tn),
        in_specs=[pl.BlockSpec((8, d), lambda l, j: (0, 0)),
                  pl.BlockSpec((None, d, tn), lambda l, j: (l, 0, j)),
                  pl.BlockSpec((None, 1, tn), lambda l, j: (l, 0, j))],
        out_specs=pl.BlockSpec((None, 8, tn), lambda l, j: (l, 0, j)),
        out_shape=jax.ShapeDtypeStruct((depth, 8, n), F32),
        compiler_params=_params(("arbitrary", "arbitrary")),
        name="ada_mod",
    )(cc, w_ada, b_ada.reshape(depth, 1, n))


def _rope(t, cos, sin_m, sin_p, quarter):
    n = t.shape[-1]
    return t * cos + pltpu.roll(t, n - quarter, 1) * sin_m + pltpu.roll(t, quarter, 1) * sin_p


def _proj_kernel(x_ref, mod_ref, g_ref, win_ref, qg_ref, kvg_ref, wuq_ref, wukn_ref, wuvt_ref, rope_ref,
                 qa_ref, ka_ref, va_ref, qb_ref, kb_ref, vbt_ref, *, tm, n_lat, scale_a, scale_b):
    i = pl.program_id(0)
    is_ctx = _is_ctx_rows(i, tm, n_lat)
    h = _norm_mod(x_ref[...], g_ref[...], mod_ref, 0, is_ctx).astype(BF16)
    proj = jnp.dot(h, win_ref[...], preferred_element_type=F32)
    cos_a, sinm_a, sinp_a = rope_ref[0], rope_ref[1], rope_ref[2]
    cos_b, sinm_b, sinp_b = rope_ref[3], rope_ref[4], rope_ref[5]
    qa_q = A_HEAD_DIM // 4
    b_q = B_ROPE_DIM // 4
    cos_as, sinm_as, sinp_as = cos_a * scale_a, sinm_a * scale_a, sinp_a * scale_a
    for hh in range(A_HEADS):
        t = proj[:, hh * LANE:(hh + 1) * LANE]
        qa_ref[:, hh * LANE:(hh + 1) * LANE] = _rope(t, cos_as, sinm_as, sinp_as, qa_q).astype(BF16)
    o = A_WIDTH
    for hh in range(A_KV_HEADS):
        t = proj[:, o + hh * LANE:o + (hh + 1) * LANE]
        ka_ref[:, hh * LANE:(hh + 1) * LANE] = _rope(t, cos_a, sinm_a, sinp_a, qa_q).astype(BF16)
    o += A_KV_WIDTH
    va_ref[...] = proj[:, o:o + A_KV_WIDTH].astype(BF16)
    o += A_KV_WIDTH
    cq = (_rms(proj[:, o:o + B_Q_RANK]) * qg_ref[...]).astype(BF16)
    o += B_Q_RANK
    qm = jnp.dot(cq, wuq_ref[...], preferred_element_type=F32)
    cos_bs, sinm_bs, sinp_bs = cos_b * scale_b, sinm_b * scale_b, sinp_b * scale_b
    for hh in range(B_HEADS):
        b0 = hh * B_QK_PAD
        qb_ref[hh, :, 0:LANE] = (qm[:, b0:b0 + LANE] * scale_b).astype(BF16)
        qb_ref[hh, :, LANE:2 * LANE] = _rope(qm[:, b0 + LANE:b0 + 2 * LANE], cos_bs, sinm_bs, sinp_bs, b_q).astype(BF16)
    ckv = (_rms(proj[:, o:o + B_KV_RANK]) * kvg_ref[...]).astype(BF16)
    o += B_KV_RANK
    kr = _rope(proj[:, o:o + LANE], cos_b, sinm_b, sinp_b, b_q).astype(BF16)
    kn = jnp.dot(ckv, wukn_ref[...], preferred_element_type=F32)
    vt = lax.dot_general(wuvt_ref[...], ckv, NT_DIMS, preferred_element_type=F32)
    for hh in range(B_HEADS):
        kb_ref[hh, :, 0:LANE] = kn[:, hh * LANE:(hh + 1) * LANE].astype(BF16)
        kb_ref[hh, :, LANE:2 * LANE] = kr
        vbt_ref[hh, 0:B_V_DIM, :] = vt[hh * B_V_DIM:(hh + 1) * B_V_DIM, :].astype(BF16)
        vbt_ref[hh, B_V_DIM:B_VT_ROWS, :] = jnp.ones((B_VT_ROWS - B_V_DIM, tm), BF16)


def _proj(xs, mod, g, win, qg, kvg, wuq, wukn, wuvt, rope_tab, *, n_lat, tm):
    t_all, d = xs.shape
    row = lambda i: (i, 0)
    kern = functools.partial(_proj_kernel, tm=tm, n_lat=n_lat, scale_a=LOG2E / math.sqrt(A_HEAD_DIM),
                             scale_b=LOG2E / math.sqrt(B_NOPE_DIM + B_ROPE_DIM))
    return pl.pallas_call(
        kern,
        grid=(t_all // tm,),
        in_specs=[pl.BlockSpec((tm, d), row), _const_spec(mod.shape), _const_spec(g.shape), _const_spec(win.shape),
                  _const_spec(qg.shape), _const_spec(kvg.shape), _const_spec(wuq.shape), _const_spec(wukn.shape),
                  _const_spec(wuvt.shape), pl.BlockSpec((6, tm, LANE), lambda i: (0, i, 0))],
        out_specs=[pl.BlockSpec((tm, A_WIDTH), row), pl.BlockSpec((tm, A_KV_WIDTH), row),
                   pl.BlockSpec((tm, A_KV_WIDTH), row),
                   pl.BlockSpec((B_HEADS, tm, B_QK_PAD), lambda i: (0, i, 0)),
                   pl.BlockSpec((B_HEADS, tm, B_QK_PAD), lambda i: (0, i, 0)),
                   pl.BlockSpec((B_HEADS, B_VT_ROWS, tm), lambda i: (0, 0, i))],
        out_shape=[jax.ShapeDtypeStruct((t_all, A_WIDTH), BF16), jax.ShapeDtypeStruct((t_all, A_KV_WIDTH), BF16),
                   jax.ShapeDtypeStruct((t_all, A_KV_WIDTH), BF16),
                   jax.ShapeDtypeStruct((B_HEADS, t_all, B_QK_PAD), BF16),
                   jax.ShapeDtypeStruct((B_HEADS, t_all, B_QK_PAD), BF16),
                   jax.ShapeDtypeStruct((B_HEADS, B_VT_ROWS, t_all), BF16)],
        compiler_params=_params(("arbitrary",)),
        name="pre_attn_proj",
    )(xs, mod, g, win, qg, kvg, wuq, wukn, wuvt, rope_tab)


def _attn_a_kernel(q_ref, k_ref, v_ref, sink_ref, o_ref, *, tq, n_lat, n_ctx):
    i = pl.program_id(0)
    n_lat_tiles = n_lat // tq
    win = tq + 2 * WINDOW
    grp = A_HEADS // A_KV_HEADS
    kc = k_ref[n_lat:n_lat + n_ctx, :]
    vc = v_ref[n_lat:n_lat + n_ctx, :]

    def head_out(hh, kw, vw, mask):
        g = hh // grp
        gs = slice(g * A_HEAD_DIM, (g + 1) * A_HEAD_DIM)
        q = q_ref[:, hh * A_HEAD_DIM:(hh + 1) * A_HEAD_DIM]
        sink = sink_ref[hh:hh + 1, 0:1]
        s_ctx = lax.dot_general(q, kc[:, gs], NT_DIMS, preferred_element_type=F32)
        m = jnp.maximum(jnp.max(s_ctx, axis=-1, keepdims=True), sink)
        if kw is not None:
            s_loc = lax.dot_general(q, kw[:, gs], NT_DIMS, preferred_element_type=F32)
            s_loc = jnp.where(mask, s_loc, NEG_INF)
            m = jnp.maximum(m, jnp.max(s_loc, axis=-1, keepdims=True))
        p_ctx = jnp.exp2(s_ctx - m)
        den = jnp.sum(p_ctx, axis=-1, keepdims=True) + jnp.exp2(sink - m)
        acc = jnp.dot(p_ctx.astype(BF16), vc[:, gs], preferred_element_type=F32)
        if kw is not None:
            p_loc = jnp.exp2(s_loc - m)
            den = den + jnp.sum(p_loc, axis=-1, keepdims=True)
            acc = acc + jnp.dot(p_loc.astype(BF16), vw[:, gs], preferred_element_type=F32)
        o_ref[:, hh * A_HEAD_DIM:(hh + 1) * A_HEAD_DIM] = (acc * (1.0 / den)).astype(o_ref.dtype)

    @pl.when(i < n_lat_tiles)
    def _latent():
        q0 = i * tq
        ws = pl.multiple_of(jnp.clip(q0 - WINDOW, 0, n_lat - win), WINDOW)
        kw = k_ref[pl.ds(ws, win), :]
        vw = v_ref[pl.ds(ws, win), :]
        qpos = q0 + lax.broadcasted_iota(jnp.int32, (tq, win), 0)
        kpos = ws + lax.broadcasted_iota(jnp.int32, (tq, win), 1)
        mask = jnp.abs(qpos - kpos) <= WINDOW
        for hh in range(A_HEADS):
            head_out(hh, kw, vw, mask)

    @pl.when(i >= n_lat_tiles)
    def _context():
        for hh in range(A_HEADS):
            head_out(hh, None, None, None)


def _attn_a(qa, ka, va, sink_tab, *, n_lat, n_ctx, tq):
    t_all = qa.shape[0]
    kern = functools.partial(_attn_a_kernel, tq=tq, n_lat=n_lat, n_ctx=n_ctx)
    return pl.pallas_call(
        kern,
        grid=(t_all // tq,),
        in_specs=[pl.BlockSpec((tq, A_WIDTH), lambda i: (i, 0)), _const_spec(ka.shape), _const_spec(va.shape),
                  _const_spec(sink_tab.shape)],
        out_specs=pl.BlockSpec((tq, A_WIDTH), lambda i: (i, 0)),
        out_shape=jax.ShapeDtypeStruct((t_all, A_WIDTH), BF16),
        compiler_params=_params(("arbitrary",)),
        name="attn_a",
    )(qa, ka, va, sink_tab)


MLA_STRIP = 256


def _col_max(st):
    rows = st.shape[0]
    groups = 4 if rows % 32 == 0 else 1
    part = st[0:rows // groups]
    for g in range(1, groups):
        part = jnp.maximum(part, st[g * (rows // groups):(g + 1) * (rows // groups)])
    return jnp.max(part, axis=0, keepdims=True)


def _mla_kernel(q_ref, k_ref, vt_ref, prev_hbm, o_ref, acc_ref, m_ref, s_ref, *, nk, tq):
    del prev_hbm
    j = pl.program_id(1)
    ns = tq // MLA_STRIP
    pipelined = ns % 2 == 0

    @pl.when(j == 0)
    def _init():
        m_ref[...] = jnp.full(m_ref.shape, NEG_INF, F32)
        acc_ref[...] = jnp.zeros(acc_ref.shape, F32)

    tk = k_ref.shape[1]
    k_half = tk // 2
    v_half = (tk // MLA_STRIP + 1) // 2 * MLA_STRIP if tk > MLA_STRIP else tk

    def scores(hh, c, slot):
        q = q_ref[hh, c * MLA_STRIP:(c + 1) * MLA_STRIP, :]
        for r0, r1 in ((0, k_half), (k_half, tk)):
            s_ref[slot, r0:r1, :] = lax.dot_general(k_ref[hh, r0:r1, :], q, NT_DIMS, preferred_element_type=F32)

    def softmax_pv(hh, c, slot):
        cs = slice(c * MLA_STRIP, (c + 1) * MLA_STRIP)
        st = s_ref[slot]
        m_old = m_ref[hh, :, cs]
        m_new = jnp.maximum(m_old, _col_max(st))
        alpha = jnp.exp2(m_old - m_new)
        p = jnp.exp2(st - m_new).astype(BF16)
        pv = jnp.dot(vt_ref[hh, :, 0:v_half], p[0:v_half], preferred_element_type=F32)
        if v_half < tk:
            pv = pv + jnp.dot(vt_ref[hh, :, v_half:tk], p[v_half:tk], preferred_element_type=F32)
        acc_ref[hh, :, cs] = alpha * acc_ref[hh, :, cs] + pv
        m_ref[hh, :, cs] = m_new

    def head(hh, carry):
        for c in range(ns):
            if not pipelined:
                scores(hh, c, 0)
            elif c + 1 < ns:
                scores(hh, c + 1, (c + 1) % 2)
            else:
                scores(jnp.minimum(hh + 1, B_HEADS - 1), 0, 0)
            softmax_pv(hh, c, c % 2 if pipelined else 0)
        return carry

    if pipelined:
        scores(0, 0, 0)
    lax.fori_loop(0, B_HEADS, head, 0, unroll=2)

    @pl.when(j == nk - 1)
    def _finish():
        for hh in range(B_HEADS):
            o = acc_ref[hh, 0:B_V_DIM, :] * (1.0 / acc_ref[hh, B_V_DIM:B_V_DIM + 1, :])
            o_ref[:, hh * B_V_DIM:(hh + 1) * B_V_DIM] = o.T.astype(o_ref.dtype)


def _mla(qb, kb, vbt, prev, *, q_row0, n_q, k_row0, n_k, tq, tk):
    t_all = qb.shape[1]
    nq, nk = n_q // tq, n_k // tk
    qo, ko = q_row0 // tq, k_row0 // tk
    kern = functools.partial(_mla_kernel, nk=nk, tq=tq)
    return pl.pallas_call(
        kern,
        grid=(nq, nk),
        in_specs=[pl.BlockSpec((B_HEADS, tq, B_QK_PAD), lambda i, j: (0, qo + i, 0)),
                  pl.BlockSpec((B_HEADS, tk, B_QK_PAD), lambda i, j: (0, ko + j, 0)),
                  pl.BlockSpec((B_HEADS, B_VT_ROWS, tk), lambda i, j: (0, 0, ko + j)),
                  pl.BlockSpec(memory_space=pl.ANY)],
        out_specs=pl.BlockSpec((tq, B_WIDTH), lambda i, j: (qo + i, 0)),
        out_shape=jax.ShapeDtypeStruct((t_all, B_WIDTH), BF16),
        scratch_shapes=[pltpu.VMEM((B_HEADS, B_VT_ROWS, tq), F32), pltpu.VMEM((B_HEADS, 1, tq), F32),
                        pltpu.VMEM((2, tk, MLA_STRIP), F32)],
        input_output_aliases={3: 0},
        compiler_params=_params(("arbitrary", "arbitrary")),
        name="mla_attn",
    )(qb, kb, vbt, prev)


def _out_kernel(oa_ref, ob_ref, x_ref, mod_ref, ga_ref, gb_ref, wo_ref, pg_ref, o_ref, *, tm, n_lat):
    i = pl.program_id(0)
    is_ctx = _is_ctx_rows(i, tm, n_lat)
    d = x_ref.shape[-1]
    a = (_rms(oa_ref[...].astype(F32)) * ga_ref[...]).astype(BF16)
    b = (_rms(ob_ref[...].astype(F32)) * gb_ref[...]).astype(BF16)
    y = (jnp.dot(a, wo_ref[0:A_WIDTH, :], preferred_element_type=F32)
         + jnp.dot(b, wo_ref[A_WIDTH:A_WIDTH + B_WIDTH, :], preferred_element_type=F32))
    gate = jnp.where(is_ctx, mod_ref[1:2, 2 * d:3 * d], mod_ref[0:1, 2 * d:3 * d])
    o_ref[...] = x_ref[...] + gate * (_rms(y) * pg_ref[...])


def _attn_out(oa, ob, xs, mod, ga, gb, wo, pg, *, n_lat, tm):
    t_all, d = xs.shape
    row = lambda i: (i, 0)
    kern = functools.partial(_out_kernel, tm=tm, n_lat=n_lat)
    return pl.pallas_call(
        kern,
        grid=(t_all // tm,),
        in_specs=[pl.BlockSpec((tm, A_WIDTH), row), pl.BlockSpec((tm, B_WIDTH), row), pl.BlockSpec((tm, d), row),
                  _const_spec(mod.shape), _const_spec(ga.shape), _const_spec(gb.shape), _const_spec(wo.shape),
                  _const_spec(pg.shape)],
        out_specs=pl.BlockSpec((tm, d), row),
        out_shape=jax.ShapeDtypeStruct((t_all, d), F32),
        compiler_params=_params(("arbitrary",)),
        name="attn_out_mix",
    )(oa, ob, xs, mod, ga, gb, wo, pg)


def _swiglu_step(x_ref, mod_ref, g_ref, w1_ref, w3_ref, w2_ref, o_ref, h_ref, is_ctx):
    j = pl.program_id(1)

    @pl.when(j == 0)
    def _prologue():
        h_ref[...] = _norm_mod(x_ref[...], g_ref[...], mod_ref, 3, is_ctx).astype(BF16)

    h = h_ref[...]
    a = jnp.dot(h, w1_ref[...], preferred_element_type=F32)
    b = jnp.dot(h, w3_ref[...], preferred_element_type=F32)
    u = (a * (1.0 / (1.0 + jnp.exp(-a))) * b).astype(BF16)
    y = jnp.dot(u, w2_ref[...], preferred_element_type=F32)

    @pl.when(j == 0)
    def _first():
        o_ref[...] = y

    @pl.when(j > 0)
    def _rest():
        o_ref[...] += y


def _ffn_dense_kernel(x_ref, mod_ref, g_ref, pg_ref, w1_ref, w3_ref, w2_ref, o_ref, h_ref, *, tm, n_lat, nj):
    i = pl.program_id(0)
    is_ctx = _is_ctx_rows(i, tm, n_lat)
    _swiglu_step(x_ref, mod_ref, g_ref, w1_ref, w3_ref, w2_ref, o_ref, h_ref, is_ctx)

    @pl.when(pl.program_id(1) == nj - 1)
    def _epilogue():
        d = x_ref.shape[-1]
        gate = jnp.where(is_ctx, mod_ref[1:2, 5 * d:6 * d], mod_ref[0:1, 5 * d:6 * d])
        o_ref[...] = x_ref[...] + gate * (_rms(o_ref[...]) * pg_ref[...])


def _ffn_dense(xs, mod, g, pg, w1, w3, w2, *, n_lat, tm, tf):
    t_all, d = xs.shape
    dff = w1.shape[1]
    nj = dff // tf
    kern = functools.partial(_ffn_dense_kernel, tm=tm, n_lat=n_lat, nj=nj)
    return pl.pallas_call(
        kern,
        grid=(t_all // tm, nj),
        in_specs=[pl.BlockSpec((tm, d), lambda i, j: (i, 0)), _const_spec(mod.shape), _const_spec(g.shape),
                  _const_spec(pg.shape),
                  pl.BlockSpec((d, tf), lambda i, j: (0, j)), pl.BlockSpec((d, tf), lambda i, j: (0, j)),
                  pl.BlockSpec((tf, d), lambda i, j: (j, 0))],
        out_specs=pl.BlockSpec((tm, d), lambda i, j: (i, 0)),
        out_shape=jax.ShapeDtypeStruct((t_all, d), F32),
        scratch_shapes=[pltpu.VMEM((tm, d), BF16)],
        compiler_params=_params(("arbitrary", "arbitrary")),
        name="ffn_dense",
    )(xs, mod, g, pg, w1, w3, w2)


def _ffn_moe_kernel(te_ref, nu_ref, x_ref, mod_ref, g_ref, w1_ref, w3_ref, w2_ref, o_ref, h_ref):
    used = pl.program_id(0) < nu_ref[0]

    @pl.when(used)
    def _used():
        _swiglu_step(x_ref, mod_ref, g_ref, w1_ref, w3_ref, w2_ref, o_ref, h_ref, False)

    @pl.when(jnp.logical_not(used) & (pl.program_id(1) == 0))
    def _idle():
        o_ref[...] = jnp.zeros(o_ref.shape, o_ref.dtype)


def _ffn_moe(xg, tile_expert, n_used, mod, g, w1, w3, w2, *, tm, tf):
    p_rows, d = xg.shape
    dff = w1.shape[2]
    nj = dff // tf

    def t_eff(t, nu):
        return jnp.minimum(t, nu[0] - 1)

    def row_map(t, j, te, nu):
        return (t_eff(t, nu), 0)

    def j_eff(t, j, nu):
        return jnp.where(t < nu[0], j, nj - 1)

    grid_spec = pltpu.PrefetchScalarGridSpec(
        num_scalar_prefetch=2,
        grid=(p_rows // tm, nj),
        in_specs=[pl.BlockSpec((tm, d), row_map),
                  pl.BlockSpec(mod.shape, lambda t, j, te, nu: (0, 0), pipeline_mode=pl.Buffered(1)),
                  pl.BlockSpec(g.shape, lambda t, j, te, nu: (0, 0), pipeline_mode=pl.Buffered(1)),
                  pl.BlockSpec((None, d, tf), lambda t, j, te, nu: (te[t_eff(t, nu)], 0, j_eff(t, j, nu))),
                  pl.BlockSpec((None, d, tf), lambda t, j, te, nu: (te[t_eff(t, nu)], 0, j_eff(t, j, nu))),
                  pl.BlockSpec((None, tf, d), lambda t, j, te, nu: (te[t_eff(t, nu)], j_eff(t, j, nu), 0))],
        out_specs=pl.BlockSpec((tm, d), lambda t, j, te, nu: (t, 0)),
        scratch_shapes=[pltpu.VMEM((tm, d), BF16)],
    )
    return pl.pallas_call(
        _ffn_moe_kernel,
        grid_spec=grid_spec,
        out_shape=jax.ShapeDtypeStruct((p_rows, d), F32),
        compiler_params=_params(("arbitrary", "arbitrary")),
        name="ffn_moe",
    )(tile_expert, n_used, xg, mod, g, w1, w3, w2)


def _route_kernel(x_ref, mod_ref, g_ref, rw_ref, o_ref):
    h = _norm_mod(x_ref[...], g_ref[...], mod_ref, 3, False)
    logits = jnp.dot(h, rw_ref[...], preferred_element_type=F32, precision=lax.Precision.HIGHEST)
    lane = lax.broadcasted_iota(jnp.int32, logits.shape, 1)
    lg = jnp.where(lane < N_EXPERTS, logits, -jnp.inf)
    m1 = jnp.max(lg, axis=-1, keepdims=True)
    i1 = jnp.min(jnp.where(lg == m1, lane, LANE), axis=-1, keepdims=True)
    lg2 = jnp.where(lane == i1, -jnp.inf, lg)
    m2 = jnp.max(lg2, axis=-1, keepdims=True)
    i2 = jnp.min(jnp.where(lg2 == m2, lane, LANE), axis=-1, keepdims=True)
    e = jnp.exp(m2 - m1)
    w1 = 1.0 / (1.0 + e)
    w2 = e * w1
    o_ref[...] = jnp.where(lane == 0, i1.astype(F32),
                           jnp.where(lane == 1, i2.astype(F32), jnp.where(lane == 2, w1, jnp.where(lane == 3, w2, 0.0))))


def _route(xs, mod, g, rw_pad, *, n_lat, tm):
    d = xs.shape[1]
    return pl.pallas_call(
        _route_kernel,
        grid=(n_lat // tm,),
        in_specs=[pl.BlockSpec((tm, d), lambda i: (i, 0)), _const_spec(mod.shape), _const_spec(g.shape),
                  _const_spec(rw_pad.shape)],
        out_specs=pl.BlockSpec((tm, LANE), lambda i: (i, 0)),
        out_shape=jax.ShapeDtypeStruct((n_lat, LANE), F32),
        compiler_params=_params(("arbitrary",)),
        name="moe_route",
    )(xs, mod, g, rw_pad)


DMA_UNROLL = 8


def _dispatch_kernel(nu_ref, src_ref, x_hbm, o_ref, sem, *, tm):
    used = pl.program_id(0) < nu_ref[0]

    def row_copy(r):
        return pltpu.make_async_copy(x_hbm.at[pl.ds(src_ref[0, 0, r], 1)], o_ref.at[pl.ds(r, 1)], sem)

    @pl.when(used)
    def _gather():
        lax.fori_loop(0, tm, lambda r, c: (row_copy(r).start(), c)[1], 0, unroll=DMA_UNROLL)
        lax.fori_loop(0, tm, lambda r, c: (row_copy(r).wait(), c)[1], 0, unroll=DMA_UNROLL)

    @pl.when(jnp.logical_not(used))
    def _idle():
        o_ref[...] = jnp.zeros(o_ref.shape, o_ref.dtype)


def _dispatch(xs, slot_src, n_used, *, tm):
    d = xs.shape[1]
    n_tiles = slot_src.shape[0]
    kern = functools.partial(_dispatch_kernel, tm=tm)
    grid_spec = pltpu.PrefetchScalarGridSpec(
        num_scalar_prefetch=1,
        grid=(n_tiles,),
        in_specs=[pl.BlockSpec((1, 1, tm), lambda t, nu: (t, 0, 0), memory_space=pltpu.SMEM),
                  pl.BlockSpec(memory_space=pl.ANY)],
        out_specs=pl.BlockSpec((tm, d), lambda t, nu: (t, 0)),
        scratch_shapes=[pltpu.SemaphoreType.DMA(())],
    )
    return pl.pallas_call(
        kern,
        grid_spec=grid_spec,
        out_shape=jax.ShapeDtypeStruct((n_tiles * tm, d), F32),
        compiler_params=_params(("arbitrary",)),
        name="moe_dispatch",
    )(n_used, slot_src, xs)


def _combine_kernel(dest_ref, rt_ref, x_ref, mod_ref, pg_ref, y_hbm, o_ref, buf, sem, *, tm):
    def row_copy(r, k):
        return pltpu.make_async_copy(y_hbm.at[pl.ds(dest_ref[0, k, r], 1)], buf.at[k, pl.ds(r, 1)], sem)

    def start(r, c):
        row_copy(r, 0).start()
        row_copy(r, 1).start()
        return c

    def wait(r, c):
        row_copy(r, 0).wait()
        row_copy(r, 1).wait()
        return c

    lax.fori_loop(0, tm, start, 0, unroll=DMA_UNROLL)
    lax.fori_loop(0, tm, wait, 0, unroll=DMA_UNROLL)
    d = x_ref.shape[-1]
    rt = rt_ref[...]
    y = rt[:, 2:3] * buf[0] + rt[:, 3:4] * buf[1]
    o_ref[...] = x_ref[...] + mod_ref[0:1, 5 * d:6 * d] * (_rms(y) * pg_ref[...])


def _combine(yg, dest, route, xs, mod, pg, *, n_lat, tm):
    d = xs.shape[1]
    kern = functools.partial(_combine_kernel, tm=tm)
    return pl.pallas_call(
        kern,
        grid=(n_lat // tm,),
        in_specs=[pl.BlockSpec((1, 2, tm), lambda i: (i, 0, 0), memory_space=pltpu.SMEM),
                  pl.BlockSpec((tm, LANE), lambda i: (i, 0)), pl.BlockSpec((tm, d), lambda i: (i, 0)),
                  _const_spec(mod.shape), _const_spec(pg.shape), pl.BlockSpec(memory_space=pl.ANY)],
        out_specs=pl.BlockSpec((tm, d), lambda i: (i, 0)),
        out_shape=jax.ShapeDtypeStruct((n_lat, d), F32),
        scratch_shapes=[pltpu.VMEM((2, tm, d), F32), pltpu.SemaphoreType.DMA(())],
        compiler_params=_params(("arbitrary",)),
        name="moe_combine",
    )(dest, route, xs, mod, pg, yg)


def _rope_tables(n_lat, n_ctx):
    rows = n_lat // GRID_W
    row = jnp.repeat(jnp.arange(rows, dtype=F32), GRID_W)
    col = jnp.tile(jnp.arange(GRID_W, dtype=F32), rows)

    def tabs(rot_dim):
        axis_dim = rot_dim // 2
        inv_freq = ROPE_BASE ** (-jnp.arange(0, axis_dim, 2, dtype=F32) / axis_dim)
        ang_r = row[:, None] * inv_freq[None, :]
        ang_c = col[:, None] * inv_freq[None, :]
        ang = jnp.concatenate([ang_r, ang_r, ang_c, ang_c], axis=-1)
        cos, sin = jnp.cos(ang), jnp.sin(ang)
        quarter = rot_dim // 4
        lower = (jnp.arange(rot_dim) % (2 * quarter)) < quarter
        sin_m = jnp.where(lower[None, :], -sin, 0.0)
        sin_p = jnp.where(lower[None, :], 0.0, sin)
        pad = LANE - rot_dim
        cos = jnp.pad(cos, ((0, n_ctx), (0, pad)), constant_values=1.0)
        sin_m = jnp.pad(sin_m, ((0, n_ctx), (0, pad)))
        sin_p = jnp.pad(sin_p, ((0, n_ctx), (0, pad)))
        return [cos, sin_m, sin_p]

    return jnp.stack(tabs(A_HEAD_DIM) + tabs(B_ROPE_DIM))


def _layer_weights(i, w_in, w_uq, w_ukv, w_o):
    d = w_in.shape[1]
    q_w = A_WIDTH + B_Q_RANK
    wi = w_in[i]
    qa, cq = wi[:, :A_WIDTH], wi[:, A_WIDTH:q_w]
    o1, o2, o3 = q_w + A_KV_WIDTH, q_w + 2 * A_KV_WIDTH, q_w + 2 * A_KV_WIDTH + B_KV_RANK
    ka, va, ckv, kr = wi[:, q_w:o1], wi[:, o1:o2], wi[:, o2:o3], wi[:, o3:]
    win = jnp.concatenate([qa, ka, va, cq, ckv, kr, jnp.zeros((d, LANE - B_ROPE_DIM), F32)], axis=1).astype(BF16)
    uq = w_uq[i].reshape(B_Q_RANK, B_HEADS, B_NOPE_DIM + B_ROPE_DIM)
    uq = jnp.pad(uq, ((0, 0), (0, 0), (0, B_QK_PAD - B_NOPE_DIM - B_ROPE_DIM)))
    wuq = uq.reshape(B_Q_RANK, B_HEADS * B_QK_PAD).astype(BF16)
    ukv = w_ukv[i].reshape(B_KV_RANK, B_HEADS, B_NOPE_DIM + B_V_DIM)
    wukn = ukv[:, :, :B_NOPE_DIM].reshape(B_KV_RANK, B_HEADS * B_NOPE_DIM).astype(BF16)
    wuvt = ukv[:, :, B_NOPE_DIM:].reshape(B_KV_RANK, B_HEADS * B_V_DIM).T.astype(BF16)
    return win, wuq, wukn, wuvt, w_o[i].astype(BF16)


def _routing_slots(route, n_lat, tm_e, n_tiles):
    e12 = route[:, 0:2].astype(jnp.int32).T.reshape(-1)
    onehot = (e12[:, None] == jnp.arange(N_EXPERTS)[None, :]).astype(jnp.int32)
    csum = jnp.cumsum(onehot, axis=0)
    rank = jnp.sum(csum * onehot, axis=1) - 1
    counts = csum[-1]
    tiles_per = (counts + tm_e - 1) // tm_e
    tile_end = jnp.cumsum(tiles_per)
    start = (tile_end - tiles_per) * tm_e
    dest = (jnp.sum(start[None, :] * onehot, axis=1) + rank).astype(jnp.int32)
    n_used = tile_end[-1].astype(jnp.int32).reshape(1)
    t_idx = jnp.arange(n_tiles, dtype=jnp.int32)
    tile_expert = jnp.minimum(jnp.sum((t_idx[:, None] >= tile_end[None, :]).astype(jnp.int32), axis=1),
                              N_EXPERTS - 1).astype(jnp.int32)
    token = jnp.tile(jnp.arange(n_lat, dtype=jnp.int32), 2)
    slot_src = jnp.zeros((n_tiles * tm_e,), jnp.int32).at[dest].set(token, unique_indices=True)
    return dest.reshape(2, n_lat), slot_src.reshape(n_tiles, 1, tm_e), tile_expert, n_used


def _tile(n, candidates):
    for c in candidates:
        if n % c == 0:
            return c
    raise ValueError(f"no supported tile size for {n} rows")


def kernel(x, c, ctx, c_ctx, w_ada, b_ada, pre_attn_g, post_attn_g, pre_ffn_g, post_ffn_g, w_in, attn_sink, q_norm_g,
           kv_norm_g, w_uq, w_ukv, grp_a_g, grp_b_g, w_o, ffn_w1, ffn_w3, ffn_w2, router_w, moe_w1, moe_w3, moe_w2):
    b, n_lat, d = x.shape
    n_ctx = ctx.shape[1]
    depth = w_ada.shape[0]
    assert b == 1 and c.shape[0] == 1, "single-sequence kernel"
    t_all = n_lat + n_ctx
    tm = _tile(t_all, (640, 256, 128))
    tq_a = 256
    tq_b = _tile(n_lat, (1024, 512, 256))
    tk_b = _tile(t_all, (1280, 256))
    tf = 512
    tm_r = _tile(n_lat, (512, 256))
    assert n_ctx == tq_a and n_lat % tq_a == 0 and n_lat >= tq_a + 2 * WINDOW

    xs = jnp.concatenate([x[0], ctx[0]], axis=0)
    cc = jnp.zeros((8, d), F32).at[0].set(c[0]).at[1].set(c_ctx)
    mods = _ada(cc, w_ada, b_ada)
    rope_tab = _rope_tables(n_lat, n_ctx)
    row2 = lambda v: v.reshape(1, -1)

    for i in range(depth):
        last = i == depth - 1
        mod = mods[i]
        win, wuq, wukn, wuvt, wo = _layer_weights(i, w_in, w_uq, w_ukv, w_o)
        qa, ka, va, qb, kb, vbt = _proj(xs, mod, row2(pre_attn_g[i]), win, row2(q_norm_g[i]), row2(kv_norm_g[i]),
                                        wuq, wukn, wuvt, rope_tab, n_lat=n_lat, tm=tm)
        sink_tab = jnp.broadcast_to((attn_sink[i] * LOG2E)[:, None], (A_HEADS, LANE))
        oa = _attn_a(qa, ka, va, sink_tab, n_lat=n_lat, n_ctx=n_ctx, tq=tq_a)
        ob = _mla(qb, kb, vbt, jnp.zeros((t_all, B_WIDTH), BF16), q_row0=0, n_q=n_lat, k_row0=0, n_k=t_all,
                  tq=tq_b, tk=tk_b)
        ob = _mla(qb, kb, vbt, ob, q_row0=n_lat, n_q=n_ctx, k_row0=n_lat, n_k=n_ctx, tq=n_ctx, tk=n_ctx)
        xs = _attn_out(oa, ob, xs, mod, row2(grp_a_g[i]), row2(grp_b_g[i]), wo, row2(post_attn_g[i]),
                       n_lat=n_lat, tm=tm)
        jj = i // 2
        if i % 2 == 0:
            xs = _ffn_dense(xs, mod, row2(pre_ffn_g[i]), row2(post_ffn_g[i]), ffn_w1[jj].astype(BF16),
                            ffn_w3[jj].astype(BF16), ffn_w2[jj].astype(BF16), n_lat=n_lat, tm=tm, tf=tf)
        else:
            n_rows = t_all if not last else n_lat
            assert last, "expert layers other than the last would also need the context rows routed"
            tm_e = tm
            n_tiles = (2 * n_rows + N_EXPERTS * (tm_e - 1)) // tm_e
            rw_pad = jnp.pad(router_w[jj], ((0, 0), (0, LANE - N_EXPERTS)))
            route = _route(xs, mod, row2(pre_ffn_g[i]), rw_pad, n_lat=n_rows, tm=tm_r)
            dest, slot_src, tile_expert, n_used = _routing_slots(route, n_rows, tm_e, n_tiles)
            dest_t = dest.reshape(2, n_rows // tm_r, tm_r).transpose(1, 0, 2)
            xg = _dispatch(xs, slot_src, n_used, tm=tm_e)
            yg = _ffn_moe(xg, tile_expert, n_used, mod, row2(pre_ffn_g[i]), moe_w1[jj].astype(BF16),
                          moe_w3[jj].astype(BF16), moe_w2[jj].astype(BF16), tm=tm_e, tf=tf)
            xs = _combine(yg, dest_t, route, xs, mod, row2(post_ffn_g[i]), n_lat=n_rows, tm=tm_r)
    return xs[:n_lat].reshape(b, n_lat, d)
```

```python
import functools
import math

import jax
import jax.numpy as jnp
from jax import lax
from jax.experimental import pallas as pl
from jax.experimental.pallas import tpu as pltpu

GRID_W = 64
EPS = 1e-6
NEG_INF = -1e30
ROPE_BASE = 10000.0
A_HEADS = 8
A_KV_HEADS = 2
A_HEAD_DIM = 128
WINDOW = 128
B_HEADS = 8
B_Q_RANK = 384
B_KV_RANK = 256
B_NOPE_DIM = 128
B_ROPE_DIM = 64
B_V_DIM = 128
N_EXPERTS = 8
A_WIDTH = A_HEADS * A_HEAD_DIM
A_KV_WIDTH = A_KV_HEADS * A_HEAD_DIM
B_WIDTH = B_HEADS * B_V_DIM
B_QK_PAD = 256
B_VT_ROWS = B_V_DIM + 16

LOG2E = math.log2(math.e)
LANE = 128
VMEM_LIMIT_BYTES = 60 * 1024 * 1024

F32 = jnp.float32
BF16 = jnp.bfloat16
NT_DIMS = (((1,), (1,)), ((), ()))


def _params(sem, vmem=VMEM_LIMIT_BYTES):
    return pltpu.CompilerParams(dimension_semantics=sem, vmem_limit_bytes=vmem)


def _const_spec(shape):
    nd = len(shape)
    return pl.BlockSpec(shape, lambda *_: (0,) * nd, pipeline_mode=pl.Buffered(1))


def _rms(x):
    return x * lax.rsqrt(jnp.mean(x * x, axis=-1, keepdims=True) + EPS)


ROW_CHUNK = 16
ROW_UNROLL = 8


def _row_chunks(tm, body):
    def step(r, carry):
        body(pl.ds(pl.multiple_of(r * ROW_CHUNK, ROW_CHUNK), ROW_CHUNK), r * ROW_CHUNK)
        return carry

    lax.fori_loop(0, tm // ROW_CHUNK, step, 0, unroll=ROW_UNROLL)


def _mod_row(tile_row0, r0, n_lat):
    return jnp.where(tile_row0 + r0 >= n_lat, 1, 0)


def _norm_mod_rows(x_ref, g_ref, mod_ref, k_shift, tile_row0, n_lat, h_ref, gs_ref):
    tm, d = x_ref.shape
    gs_ref[...] = g_ref[...] * (1.0 + mod_ref[:, (k_shift + 1) * d:(k_shift + 2) * d])

    def body(rows, r0):
        mrow = _mod_row(tile_row0, r0, n_lat)
        h = _rms(x_ref[rows, :]) * gs_ref[pl.ds(mrow, 1), :]
        h_ref[rows, :] = (h + mod_ref[pl.ds(mrow, 1), k_shift * d:(k_shift + 1) * d]).astype(BF16)

    _row_chunks(tm, body)


def _resid_rows(o_ref, y_ref, x_ref, pg_ref, mod_ref, k_gate, tile_row0, n_lat, gs_ref):
    tm, d = x_ref.shape
    gs_ref[...] = pg_ref[...] * mod_ref[:, k_gate * d:(k_gate + 1) * d]

    def body(rows, r0):
        mrow = _mod_row(tile_row0, r0, n_lat)
        o_ref[rows, :] = x_ref[rows, :] + gs_ref[pl.ds(mrow, 1), :] * _rms(y_ref[rows, :])

    _row_chunks(tm, body)


def _rms_gain_rows(x_ref, g_ref, h_ref):
    def body(rows, r0):
        h_ref[rows, :] = (_rms(x_ref[rows, :].astype(F32)) * g_ref[...]).astype(BF16)

    _row_chunks(x_ref.shape[0], body)


def _ada_kernel(c_ref, w_ref, b_ref, o_ref):
    c = c_ref[...]
    a = c * (1.0 / (1.0 + jnp.exp(-c)))
    o_ref[...] = jnp.dot(a, w_ref[...], preferred_element_type=F32, precision=lax.Precision.HIGHEST) + b_ref[...]


def _ada(cc, w_ada, b_ada):
    depth, d, n = w_ada.shape
    tn = n // 8
    return pl.pallas_call(
        _ada_kernel,
        grid=(depth, n // tn),
        in_specs=[pl.BlockSpec((8, d), lambda l, j: (0, 0)),
                  pl.BlockSpec((None, d, tn), lambda l, j: (l, 0, j)),
                  pl.BlockSpec((None, 1, tn), lambda l, j: (l, 0, j))],
        out_specs=pl.BlockSpec((None, 8, tn), lambda l, j: (l, 0, j)),
        out_shape=jax.ShapeDtypeStruct((depth, 8, n), F32),
        compiler_params=_params(("arbitrary", "arbitrary")),
        name="ada_mod",
    )(cc, w_ada, b_ada.reshape(depth, 1, n))


def _rope(t, cos, sin_m, sin_p, quarter):
    n = t.shape[-1]
    return t * cos + pltpu.roll(t, n - quarter, 1) * sin_m + pltpu.roll(t, quarter, 1) * sin_p


def _proj_kernel(x_ref, mod_ref, g_ref, win_ref, qg_ref, kvg_ref, wuq_ref, wukn_ref, wuvt_ref, rope_ref,
                 qa_ref, ka_ref, va_ref, qb_ref, kb_ref, vbt_ref, h_ref, gs_ref, *, tm, n_lat, scale_a, scale_b):
    _norm_mod_rows(x_ref, g_ref, mod_ref, 0, pl.program_id(0) * tm, n_lat, h_ref, gs_ref)
    proj = jnp.dot(h_ref[...], win_ref[...], preferred_element_type=F32)
    cos_a, sinm_a, sinp_a = rope_ref[0], rope_ref[1], rope_ref[2]
    cos_b, sinm_b, sinp_b = rope_ref[3], rope_ref[4], rope_ref[5]
    qa_q = A_HEAD_DIM // 4
    b_q = B_ROPE_DIM // 4
    cos_as, sinm_as, sinp_as = cos_a * scale_a, sinm_a * scale_a, sinp_a * scale_a
    for hh in range(A_HEADS):
        t = proj[:, hh * LANE:(hh + 1) * LANE]
        qa_ref[:, hh * LANE:(hh + 1) * LANE] = _rope(t, cos_as, sinm_as, sinp_as, qa_q).astype(BF16)
    o = A_WIDTH
    for hh in range(A_KV_HEADS):
        t = proj[:, o + hh * LANE:o + (hh + 1) * LANE]
        ka_ref[:, hh * LANE:(hh + 1) * LANE] = _rope(t, cos_a, sinm_a, sinp_a, qa_q).astype(BF16)
    o += A_KV_WIDTH
    va_ref[...] = proj[:, o:o + A_KV_WIDTH].astype(BF16)
    o += A_KV_WIDTH
    cq = (_rms(proj[:, o:o + B_Q_RANK]) * qg_ref[...]).astype(BF16)
    o += B_Q_RANK
    qm = jnp.dot(cq, wuq_ref[...], preferred_element_type=F32)
    cos_bs, sinm_bs, sinp_bs = cos_b * scale_b, sinm_b * scale_b, sinp_b * scale_b
    for hh in range(B_HEADS):
        b0 = hh * B_QK_PAD
        qb_ref[hh, :, 0:LANE] = (qm[:, b0:b0 + LANE] * scale_b).astype(BF16)
        qb_ref[hh, :, LANE:2 * LANE] = _rope(qm[:, b0 + LANE:b0 + 2 * LANE], cos_bs, sinm_bs, sinp_bs, b_q).astype(BF16)
    ckv = (_rms(proj[:, o:o + B_KV_RANK]) * kvg_ref[...]).astype(BF16)
    o += B_KV_RANK
    kr = _rope(proj[:, o:o + LANE], cos_b, sinm_b, sinp_b, b_q).astype(BF16)
    kn = jnp.dot(ckv, wukn_ref[...], preferred_element_type=F32)
    vt = lax.dot_general(wuvt_ref[...], ckv, NT_DIMS, preferred_element_type=F32)
    for hh in range(B_HEADS):
        kb_ref[hh, :, 0:LANE] = kn[:, hh * LANE:(hh + 1) * LANE].astype(BF16)
        kb_ref[hh, :, LANE:2 * LANE] = kr
        vbt_ref[hh, 0:B_V_DIM, :] = vt[hh * B_V_DIM:(hh + 1) * B_V_DIM, :].astype(BF16)
        vbt_ref[hh, B_V_DIM:B_VT_ROWS, :] = jnp.ones((B_VT_ROWS - B_V_DIM, tm), BF16)


def _proj(xs, mod, g, win, qg, kvg, wuq, wukn, wuvt, rope_tab, *, n_lat, tm):
    t_all, d = xs.shape
    row = lambda i: (i, 0)
    kern = functools.partial(_proj_kernel, tm=tm, n_lat=n_lat, scale_a=LOG2E / math.sqrt(A_HEAD_DIM),
                             scale_b=LOG2E / math.sqrt(B_NOPE_DIM + B_ROPE_DIM))
    return pl.pallas_call(
        kern,
        grid=(t_all // tm,),
        in_specs=[pl.BlockSpec((tm, d), row), _const_spec(mod.shape), _const_spec(g.shape), _const_spec(win.shape),
                  _const_spec(qg.shape), _const_spec(kvg.shape), _const_spec(wuq.shape), _const_spec(wukn.shape),
                  _const_spec(wuvt.shape), pl.BlockSpec((6, tm, LANE), lambda i: (0, i, 0))],
        out_specs=[pl.BlockSpec((tm, A_WIDTH), row), pl.BlockSpec((tm, A_KV_WIDTH), row),
                   pl.BlockSpec((tm, A_KV_WIDTH), row),
                   pl.BlockSpec((B_HEADS, tm, B_QK_PAD), lambda i: (0, i, 0)),
                   pl.BlockSpec((B_HEADS, tm, B_QK_PAD), lambda i: (0, i, 0)),
                   pl.BlockSpec((B_HEADS, B_VT_ROWS, tm), lambda i: (0, 0, i))],
        out_shape=[jax.ShapeDtypeStruct((t_all, A_WIDTH), BF16), jax.ShapeDtypeStruct((t_all, A_KV_WIDTH), BF16),
                   jax.ShapeDtypeStruct((t_all, A_KV_WIDTH), BF16),
                   jax.ShapeDtypeStruct((B_HEADS, t_all, B_QK_PAD), BF16),
                   jax.ShapeDtypeStruct((B_HEADS, t_all, B_QK_PAD), BF16),
                   jax.ShapeDtypeStruct((B_HEADS, B_VT_ROWS, t_all), BF16)],
        scratch_shapes=[pltpu.VMEM((tm, d), BF16), pltpu.VMEM((8, d), F32)],
        compiler_params=_params(("arbitrary",)),
        name="pre_attn_proj",
    )(xs, mod, g, win, qg, kvg, wuq, wukn, wuvt, rope_tab)


def _attn_a_kernel(q_ref, k_ref, v_ref, sink_ref, o_ref, *, tq, n_lat, n_ctx):
    i = pl.program_id(0)
    n_lat_tiles = n_lat // tq
    win = tq + 2 * WINDOW
    grp = A_HEADS // A_KV_HEADS
    kc = k_ref[n_lat:n_lat + n_ctx, :]
    vc = v_ref[n_lat:n_lat + n_ctx, :]
    tn_dims = (((0,), (0,)), ((), ()))

    def group_out(g, kw, vw, mask):
        gs = slice(g * A_HEAD_DIM, (g + 1) * A_HEAD_DIM)
        heads = range(g * grp, (g + 1) * grp)
        q = jnp.concatenate([q_ref[:, hh * A_HEAD_DIM:(hh + 1) * A_HEAD_DIM] for hh in heads], axis=0)
        sink = sink_ref[g:g + 1, :]
        s_ctx = lax.dot_general(kc[:, gs], q, NT_DIMS, preferred_element_type=F32)
        m = jnp.maximum(_col_max(s_ctx), sink)
        if kw is not None:
            s_loc = lax.dot_general(kw[:, gs], q, NT_DIMS, preferred_element_type=F32)
            s_loc = jnp.concatenate([jnp.where(mask, s_loc[:, a * tq:(a + 1) * tq], NEG_INF) for a in range(grp)],
                                    axis=1)
            m = jnp.maximum(m, _col_max(s_loc))
        p_ctx = jnp.exp2(s_ctx - m)
        den = _col_sum(p_ctx) + jnp.exp2(sink - m)
        acc = lax.dot_general(vc[:, gs], p_ctx.astype(BF16), tn_dims, preferred_element_type=F32)
        if kw is not None:
            p_loc = jnp.exp2(s_loc - m)
            den = den + _col_sum(p_loc)
            acc = acc + lax.dot_general(vw[:, gs], p_loc.astype(BF16), tn_dims, preferred_element_type=F32)
        o = acc * (1.0 / den)
        for a, hh in enumerate(heads):
            o_ref[:, hh * A_HEAD_DIM:(hh + 1) * A_HEAD_DIM] = o[:, a * tq:(a + 1) * tq].T.astype(o_ref.dtype)

    @pl.when(i < n_lat_tiles)
    def _latent():
        q0 = i * tq
        ws = pl.multiple_of(jnp.clip(q0 - WINDOW, 0, n_lat - win), WINDOW)
        kw = k_ref[pl.ds(ws, win), :]
        vw = v_ref[pl.ds(ws, win), :]
        kpos = ws + lax.broadcasted_iota(jnp.int32, (win, tq), 0)
        qpos = q0 + lax.broadcasted_iota(jnp.int32, (win, tq), 1)
        mask = jnp.abs(qpos - kpos) <= WINDOW
        for g in range(A_KV_HEADS):
            group_out(g, kw, vw, mask)

    @pl.when(i >= n_lat_tiles)
    def _context():
        for g in range(A_KV_HEADS):
            group_out(g, None, None, None)


def _attn_a(qa, ka, va, sink_tab, *, n_lat, n_ctx, tq):
    t_all = qa.shape[0]
    kern = functools.partial(_attn_a_kernel, tq=tq, n_lat=n_lat, n_ctx=n_ctx)
    return pl.pallas_call(
        kern,
        grid=(t_all // tq,),
        in_specs=[pl.BlockSpec((tq, A_WIDTH), lambda i: (i, 0)), _const_spec(ka.shape), _const_spec(va.shape),
                  _const_spec(sink_tab.shape)],
        out_specs=pl.BlockSpec((tq, A_WIDTH), lambda i: (i, 0)),
        out_shape=jax.ShapeDtypeStruct((t_all, A_WIDTH), BF16),
        compiler_params=_params(("arbitrary",)),
        name="attn_a",
    )(qa, ka, va, sink_tab)


MLA_STRIP = 256


def _col_fold(st, pair_op):
    rows = st.shape[0]
    groups = 4 if rows % 32 == 0 else 1
    part = st[0:rows // groups]
    for g in range(1, groups):
        part = pair_op(part, st[g * (rows // groups):(g + 1) * (rows // groups)])
    return part


def _col_max(st):
    return jnp.max(_col_fold(st, jnp.maximum), axis=0, keepdims=True)


def _col_sum(st):
    return jnp.sum(_col_fold(st, jnp.add), axis=0, keepdims=True)


def _mla_kernel(q_ref, k_ref, vt_ref, prev_hbm, o_ref, acc_ref, m_ref, s_ref, *, nk, tq):
    del prev_hbm
    j = pl.program_id(1)
    ns = tq // MLA_STRIP
    pipelined = ns % 2 == 0

    @pl.when(j == 0)
    def _init():
        m_ref[...] = jnp.full(m_ref.shape, NEG_INF, F32)
        acc_ref[...] = jnp.zeros(acc_ref.shape, F32)

    tk = k_ref.shape[1]
    k_half = tk // 2
    v_half = (tk // MLA_STRIP + 1) // 2 * MLA_STRIP if tk > MLA_STRIP else tk

    def scores(hh, c, slot):
        q = q_ref[hh, c * MLA_STRIP:(c + 1) * MLA_STRIP, :]
        for r0, r1 in ((0, k_half), (k_half, tk)):
            s_ref[slot, r0:r1, :] = lax.dot_general(k_ref[hh, r0:r1, :], q, NT_DIMS, preferred_element_type=F32)

    def softmax_pv(hh, c, slot):
        cs = slice(c * MLA_STRIP, (c + 1) * MLA_STRIP)
        st = s_ref[slot]
        m_old = m_ref[hh, :, cs]
        m_new = jnp.maximum(m_old, _col_max(st))
        alpha = jnp.exp2(m_old - m_new)
        p = jnp.exp2(st - m_new).astype(BF16)
        pv = jnp.dot(vt_ref[hh, :, 0:v_half], p[0:v_half], preferred_element_type=F32)
        if v_half < tk:
            pv = pv + jnp.dot(vt_ref[hh, :, v_half:tk], p[v_half:tk], preferred_element_type=F32)
        acc_ref[hh, :, cs] = alpha * acc_ref[hh, :, cs] + pv
        m_ref[hh, :, cs] = m_new

    def head(hh, carry):
        for c in range(ns):
            if not pipelined:
                scores(hh, c, 0)
            elif c + 1 < ns:
                scores(hh, c + 1, (c + 1) % 2)
            else:
                scores(jnp.minimum(hh + 1, B_HEADS - 1), 0, 0)
            softmax_pv(hh, c, c % 2 if pipelined else 0)
        return carry

    if pipelined:
        scores(0, 0, 0)
    lax.fori_loop(0, B_HEADS, head, 0, unroll=2)

    @pl.when(j == nk - 1)
    def _finish():
        for hh in range(B_HEADS):
            o = acc_ref[hh, 0:B_V_DIM, :] * (1.0 / acc_ref[hh, B_V_DIM:B_V_DIM + 1, :])
            o_ref[:, hh * B_V_DIM:(hh + 1) * B_V_DIM] = o.T.astype(o_ref.dtype)


def _mla(qb, kb, vbt, prev, *, q_row0, n_q, k_row0, n_k, tq, tk):
    t_all = qb.shape[1]
    nq, nk = n_q // tq, n_k // tk
    qo, ko = q_row0 // tq, k_row0 // tk
    kern = functools.partial(_mla_kernel, nk=nk, tq=tq)
    return pl.pallas_call(
        kern,
        grid=(nq, nk),
        in_specs=[pl.BlockSpec((B_HEADS, tq, B_QK_PAD), lambda i, j: (0, qo + i, 0)),
                  pl.BlockSpec((B_HEADS, tk, B_QK_PAD), lambda i, j: (0, ko + j, 0)),
                  pl.BlockSpec((B_HEADS, B_VT_ROWS, tk), lambda i, j: (0, 0, ko + j)),
                  pl.BlockSpec(memory_space=pl.ANY)],
        out_specs=pl.BlockSpec((tq, B_WIDTH), lambda i, j: (qo + i, 0)),
        out_shape=jax.ShapeDtypeStruct((t_all, B_WIDTH), BF16),
        scratch_shapes=[pltpu.VMEM((B_HEADS, B_VT_ROWS, tq), F32), pltpu.VMEM((B_HEADS, 1, tq), F32),
                        pltpu.VMEM((2, tk, MLA_STRIP), F32)],
        input_output_aliases={3: 0},
        compiler_params=_params(("arbitrary", "arbitrary")),
        name="mla_attn",
    )(qb, kb, vbt, prev)


def _out_kernel(oa_ref, ob_ref, x_ref, mod_ref, ga_ref, gb_ref, wo_ref, pg_ref, o_ref, a_ref, b_ref, y_ref, gs_ref,
                *, tm, n_lat):
    _rms_gain_rows(oa_ref, ga_ref, a_ref)
    _rms_gain_rows(ob_ref, gb_ref, b_ref)
    y_ref[...] = (jnp.dot(a_ref[...], wo_ref[0:A_WIDTH, :], preferred_element_type=F32)
                  + jnp.dot(b_ref[...], wo_ref[A_WIDTH:A_WIDTH + B_WIDTH, :], preferred_element_type=F32))
    _resid_rows(o_ref, y_ref, x_ref, pg_ref, mod_ref, 2, pl.program_id(0) * tm, n_lat, gs_ref)


def _attn_out(oa, ob, xs, mod, ga, gb, wo, pg, *, n_lat, tm):
    t_all, d = xs.shape
    row = lambda i: (i, 0)
    kern = functools.partial(_out_kernel, tm=tm, n_lat=n_lat)
    return pl.pallas_call(
        kern,
        grid=(t_all // tm,),
        in_specs=[pl.BlockSpec((tm, A_WIDTH), row), pl.BlockSpec((tm, B_WIDTH), row), pl.BlockSpec((tm, d), row),
                  _const_spec(mod.shape), _const_spec(ga.shape), _const_spec(gb.shape), _const_spec(wo.shape),
                  _const_spec(pg.shape)],
        out_specs=pl.BlockSpec((tm, d), row),
        out_shape=jax.ShapeDtypeStruct((t_all, d), F32),
        scratch_shapes=[pltpu.VMEM((tm, A_WIDTH), BF16), pltpu.VMEM((tm, B_WIDTH), BF16), pltpu.VMEM((tm, d), F32),
                        pltpu.VMEM((8, d), F32)],
        compiler_params=_params(("arbitrary",)),
        name="attn_out_mix",
    )(oa, ob, xs, mod, ga, gb, wo, pg)


def _swiglu_step(x_ref, mod_ref, g_ref, w1_ref, w3_ref, w2_ref, o_ref, h_ref, gs_ref, tile_row0, n_lat):
    j = pl.program_id(1)

    @pl.when(j == 0)
    def _prologue():
        _norm_mod_rows(x_ref, g_ref, mod_ref, 3, tile_row0, n_lat, h_ref, gs_ref)
        o_ref[...] = jnp.zeros(o_ref.shape, o_ref.dtype)

    h = h_ref[...]
    a = jnp.dot(h, w1_ref[...], preferred_element_type=F32)
    b = jnp.dot(h, w3_ref[...], preferred_element_type=F32)
    u = (a * (1.0 / (1.0 + jnp.exp(-a))) * b).astype(BF16)
    o_ref[...] += jnp.dot(u, w2_ref[...], preferred_element_type=F32)


def _ffn_dense_kernel(x_ref, mod_ref, g_ref, pg_ref, w1_ref, w3_ref, w2_ref, o_ref, h_ref, gs_ref, acc_ref, *, tm,
                      n_lat, nj):
    tile_row0 = pl.program_id(0) * tm
    _swiglu_step(x_ref, mod_ref, g_ref, w1_ref, w3_ref, w2_ref, acc_ref, h_ref, gs_ref, tile_row0, n_lat)

    @pl.when(pl.program_id(1) == nj - 1)
    def _epilogue():
        _resid_rows(o_ref, acc_ref, x_ref, pg_ref, mod_ref, 5, tile_row0, n_lat, gs_ref)


def _ffn_dense(xs, mod, g, pg, w1, w3, w2, *, n_lat, tm, tf):
    t_all, d = xs.shape
    dff = w1.shape[1]
    nj = dff // tf
    kern = functools.partial(_ffn_dense_kernel, tm=tm, n_lat=n_lat, nj=nj)
    return pl.pallas_call(
        kern,
        grid=(t_all // tm, nj),
        in_specs=[pl.BlockSpec((tm, d), lambda i, j: (i, 0)), _const_spec(mod.shape), _const_spec(g.shape),
                  _const_spec(pg.shape),
                  pl.BlockSpec((d, tf), lambda i, j: (0, j)), pl.BlockSpec((d, tf), lambda i, j: (0, j)),
                  pl.BlockSpec((tf, d), lambda i, j: (j, 0))],
        out_specs=pl.BlockSpec((tm, d), lambda i, j: (i, 0)),
        out_shape=jax.ShapeDtypeStruct((t_all, d), F32),
        scratch_shapes=[pltpu.VMEM((tm, d), BF16), pltpu.VMEM((8, d), F32), pltpu.VMEM((tm, d), F32)],
        compiler_params=_params(("arbitrary", "arbitrary")),
        name="ffn_dense",
    )(xs, mod, g, pg, w1, w3, w2)


def _ffn_moe_kernel(te_ref, nu_ref, x_ref, mod_ref, g_ref, w1_ref, w3_ref, w2_ref, o_ref, h_ref, gs_ref):
    used = pl.program_id(0) < nu_ref[0]

    @pl.when(used)
    def _used():
        _swiglu_step(x_ref, mod_ref, g_ref, w1_ref, w3_ref, w2_ref, o_ref, h_ref, gs_ref, 0, x_ref.shape[0])

    @pl.when(jnp.logical_not(used) & (pl.program_id(1) == 0))
    def _idle():
        o_ref[...] = jnp.zeros(o_ref.shape, o_ref.dtype)


def _ffn_moe(xg, tile_expert, n_used, mod, g, w1, w3, w2, *, tm, tf):
    p_rows, d = xg.shape
    dff = w1.shape[2]
    nj = dff // tf

    def t_eff(t, nu):
        return jnp.minimum(t, nu[0] - 1)

    def row_map(t, j, te, nu):
        return (t_eff(t, nu), 0)

    def j_eff(t, j, nu):
        return jnp.where(t < nu[0], j, nj - 1)

    grid_spec = pltpu.PrefetchScalarGridSpec(
        num_scalar_prefetch=2,
        grid=(p_rows // tm, nj),
        in_specs=[pl.BlockSpec((tm, d), row_map),
                  pl.BlockSpec(mod.shape, lambda t, j, te, nu: (0, 0), pipeline_mode=pl.Buffered(1)),
                  pl.BlockSpec(g.shape, lambda t, j, te, nu: (0, 0), pipeline_mode=pl.Buffered(1)),
                  pl.BlockSpec((None, d, tf), lambda t, j, te, nu: (te[t_eff(t, nu)], 0, j_eff(t, j, nu))),
                  pl.BlockSpec((None, d, tf), lambda t, j, te, nu: (te[t_eff(t, nu)], 0, j_eff(t, j, nu))),
                  pl.BlockSpec((None, tf, d), lambda t, j, te, nu: (te[t_eff(t, nu)], j_eff(t, j, nu), 0))],
        out_specs=pl.BlockSpec((tm, d), lambda t, j, te, nu: (t, 0)),
        scratch_shapes=[pltpu.VMEM((tm, d), BF16), pltpu.VMEM((8, d), F32)],
    )
    return pl.pallas_call(
        _ffn_moe_kernel,
        grid_spec=grid_spec,
        out_shape=jax.ShapeDtypeStruct((p_rows, d), F32),
        compiler_params=_params(("arbitrary", "arbitrary")),
        name="ffn_moe",
    )(tile_expert, n_used, xg, mod, g, w1, w3, w2)


def _route_kernel(x_ref, mod_ref, g_ref, rw_ref, o_ref):
    d = x_ref.shape[-1]
    h = _rms(x_ref[...]) * g_ref[...] * (1.0 + mod_ref[0:1, 4 * d:5 * d]) + mod_ref[0:1, 3 * d:4 * d]
    logits = jnp.dot(h, rw_ref[...], preferred_element_type=F32, precision=lax.Precision.HIGHEST)
    lane = lax.broadcasted_iota(jnp.int32, logits.shape, 1)
    lg = jnp.where(lane < N_EXPERTS, logits, -jnp.inf)
    m1 = jnp.max(lg, axis=-1, keepdims=True)
    i1 = jnp.min(jnp.where(lg == m1, lane, LANE), axis=-1, keepdims=True)
    lg2 = jnp.where(lane == i1, -jnp.inf, lg)
    m2 = jnp.max(lg2, axis=-1, keepdims=True)
    i2 = jnp.min(jnp.where(lg2 == m2, lane, LANE), axis=-1, keepdims=True)
    e = jnp.exp(m2 - m1)
    w1 = 1.0 / (1.0 + e)
    w2 = e * w1
    o_ref[...] = jnp.where(lane == 0, i1.astype(F32),
                           jnp.where(lane == 1, i2.astype(F32), jnp.where(lane == 2, w1, jnp.where(lane == 3, w2, 0.0))))


def _route(xs, mod, g, rw_pad, *, n_lat, tm):
    d = xs.shape[1]
    return pl.pallas_call(
        _route_kernel,
        grid=(n_lat // tm,),
        in_specs=[pl.BlockSpec((tm, d), lambda i: (i, 0)), _const_spec(mod.shape), _const_spec(g.shape),
                  _const_spec(rw_pad.shape)],
        out_specs=pl.BlockSpec((tm, LANE), lambda i: (i, 0)),
        out_shape=jax.ShapeDtypeStruct((n_lat, LANE), F32),
        compiler_params=_params(("arbitrary",)),
        name="moe_route",
    )(xs, mod, g, rw_pad)


DMA_UNROLL = 8


def _dispatch_kernel(nu_ref, src_ref, x_hbm, o_ref, sem, *, tm):
    used = pl.program_id(0) < nu_ref[0]

    def row_copy(r):
        return pltpu.make_async_copy(x_hbm.at[pl.ds(src_ref[0, 0, r], 1)], o_ref.at[pl.ds(r, 1)], sem)

    @pl.when(used)
    def _gather():
        lax.fori_loop(0, tm, lambda r, c: (row_copy(r).start(), c)[1], 0, unroll=DMA_UNROLL)
        lax.fori_loop(0, tm, lambda r, c: (row_copy(r).wait(), c)[1], 0, unroll=DMA_UNROLL)

    @pl.when(jnp.logical_not(used))
    def _idle():
        o_ref[...] = jnp.zeros(o_ref.shape, o_ref.dtype)


def _dispatch(xs, slot_src, n_used, *, tm):
    d = xs.shape[1]
    n_tiles = slot_src.shape[0]
    kern = functools.partial(_dispatch_kernel, tm=tm)
    grid_spec = pltpu.PrefetchScalarGridSpec(
        num_scalar_prefetch=1,
        grid=(n_tiles,),
        in_specs=[pl.BlockSpec((1, 1, tm), lambda t, nu: (t, 0, 0), memory_space=pltpu.SMEM),
                  pl.BlockSpec(memory_space=pl.ANY)],
        out_specs=pl.BlockSpec((tm, d), lambda t, nu: (t, 0)),
        scratch_shapes=[pltpu.SemaphoreType.DMA(())],
    )
    return pl.pallas_call(
        kern,
        grid_spec=grid_spec,
        out_shape=jax.ShapeDtypeStruct((n_tiles * tm, d), F32),
        compiler_params=_params(("arbitrary",)),
        name="moe_dispatch",
    )(n_used, slot_src, xs)


def _combine_kernel(dest_ref, rt_ref, x_ref, mod_ref, pg_ref, y_hbm, o_ref, buf, sem, *, tm):
    def row_copy(r, k):
        return pltpu.make_async_copy(y_hbm.at[pl.ds(dest_ref[0, k, r], 1)], buf.at[k, pl.ds(r, 1)], sem)

    def start(r, c):
        row_copy(r, 0).start()
        row_copy(r, 1).start()
        return c

    def wait(r, c):
        row_copy(r, 0).wait()
        row_copy(r, 1).wait()
        return c

    lax.fori_loop(0, tm, start, 0, unroll=DMA_UNROLL)
    lax.fori_loop(0, tm, wait, 0, unroll=DMA_UNROLL)
    d = x_ref.shape[-1]
    gate = mod_ref[0:1, 5 * d:6 * d] * pg_ref[...]

    def body(rows, r0):
        y = rt_ref[rows, 2:3] * buf[0, rows, :] + rt_ref[rows, 3:4] * buf[1, rows, :]
        o_ref[rows, :] = x_ref[rows, :] + gate * _rms(y)

    _row_chunks(tm, body)


def _combine(yg, dest, route, xs, mod, pg, *, n_lat, tm):
    d = xs.shape[1]
    kern = functools.partial(_combine_kernel, tm=tm)
    return pl.pallas_call(
        kern,
        grid=(n_lat // tm,),
        in_specs=[pl.BlockSpec((1, 2, tm), lambda i: (i, 0, 0), memory_space=pltpu.SMEM),
                  pl.BlockSpec((tm, LANE), lambda i: (i, 0)), pl.BlockSpec((tm, d), lambda i: (i, 0)),
                  _const_spec(mod.shape), _const_spec(pg.shape), pl.BlockSpec(memory_space=pl.ANY)],
        out_specs=pl.BlockSpec((tm, d), lambda i: (i, 0)),
        out_shape=jax.ShapeDtypeStruct((n_lat, d), F32),
        scratch_shapes=[pltpu.VMEM((2, tm, d), F32), pltpu.SemaphoreType.DMA(())],
        compiler_params=_params(("arbitrary",)),
        name="moe_combine",
    )(dest, route, xs, mod, pg, yg)


def _rope_tables(n_lat, n_ctx):
    rows = n_lat // GRID_W
    row = jnp.repeat(jnp.arange(rows, dtype=F32), GRID_W)
    col = jnp.tile(jnp.arange(GRID_W, dtype=F32), rows)

    def tabs(rot_dim):
        axis_dim = rot_dim // 2
        inv_freq = ROPE_BASE ** (-jnp.arange(0, axis_dim, 2, dtype=F32) / axis_dim)
        ang_r = row[:, None] * inv_freq[None, :]
        ang_c = col[:, None] * inv_freq[None, :]
        ang = jnp.concatenate([ang_r, ang_r, ang_c, ang_c], axis=-1)
        cos, sin = jnp.cos(ang), jnp.sin(ang)
        quarter = rot_dim // 4
        lower = (jnp.arange(rot_dim) % (2 * quarter)) < quarter
        sin_m = jnp.where(lower[None, :], -sin, 0.0)
        sin_p = jnp.where(lower[None, :], 0.0, sin)
        pad = LANE - rot_dim
        cos = jnp.pad(cos, ((0, n_ctx), (0, pad)), constant_values=1.0)
        sin_m = jnp.pad(sin_m, ((0, n_ctx), (0, pad)))
        sin_p = jnp.pad(sin_p, ((0, n_ctx), (0, pad)))
        return [cos, sin_m, sin_p]

    return jnp.stack(tabs(A_HEAD_DIM) + tabs(B_ROPE_DIM))


def _layer_weights(i, w_in, w_uq, w_ukv, w_o):
    d = w_in.shape[1]
    q_w = A_WIDTH + B_Q_RANK
    wi = w_in[i]
    qa, cq = wi[:, :A_WIDTH], wi[:, A_WIDTH:q_w]
    o1, o2, o3 = q_w + A_KV_WIDTH, q_w + 2 * A_KV_WIDTH, q_w + 2 * A_KV_WIDTH + B_KV_RANK
    ka, va, ckv, kr = wi[:, q_w:o1], wi[:, o1:o2], wi[:, o2:o3], wi[:, o3:]
    win = jnp.concatenate([qa, ka, va, cq, ckv, kr, jnp.zeros((d, LANE - B_ROPE_DIM), F32)], axis=1).astype(BF16)
    uq = w_uq[i].reshape(B_Q_RANK, B_HEADS, B_NOPE_DIM + B_ROPE_DIM)
    uq = jnp.pad(uq, ((0, 0), (0, 0), (0, B_QK_PAD - B_NOPE_DIM - B_ROPE_DIM)))
    wuq = uq.reshape(B_Q_RANK, B_HEADS * B_QK_PAD).astype(BF16)
    ukv = w_ukv[i].reshape(B_KV_RANK, B_HEADS, B_NOPE_DIM + B_V_DIM)
    wukn = ukv[:, :, :B_NOPE_DIM].reshape(B_KV_RANK, B_HEADS * B_NOPE_DIM).astype(BF16)
    wuvt = ukv[:, :, B_NOPE_DIM:].reshape(B_KV_RANK, B_HEADS * B_V_DIM).T.astype(BF16)
    return win, wuq, wukn, wuvt, w_o[i].astype(BF16)


def _routing_slots(route, n_lat, tm_e, n_tiles):
    e12 = route[:, 0:2].astype(jnp.int32).T.reshape(-1)
    onehot = (e12[:, None] == jnp.arange(N_EXPERTS)[None, :]).astype(jnp.int32)
    csum = jnp.cumsum(onehot, axis=0)
    rank = jnp.sum(csum * onehot, axis=1) - 1
    counts = csum[-1]
    tiles_per = (counts + tm_e - 1) // tm_e
    tile_end = jnp.cumsum(tiles_per)
    start = (tile_end - tiles_per) * tm_e
    dest = (jnp.sum(start[None, :] * onehot, axis=1) + rank).astype(jnp.int32)
    n_used = tile_end[-1].astype(jnp.int32).reshape(1)
    t_idx = jnp.arange(n_tiles, dtype=jnp.int32)
    tile_expert = jnp.minimum(jnp.sum((t_idx[:, None] >= tile_end[None, :]).astype(jnp.int32), axis=1),
                              N_EXPERTS - 1).astype(jnp.int32)
    token = jnp.tile(jnp.arange(n_lat, dtype=jnp.int32), 2)
    slot_src = jnp.zeros((n_tiles * tm_e,), jnp.int32).at[dest].set(token, unique_indices=True)
    return dest.reshape(2, n_lat), slot_src.reshape(n_tiles, 1, tm_e), tile_expert, n_used


def _tile(n, candidates):
    for c in candidates:
        if n % c == 0:
            return c
    raise ValueError(f"no supported tile size for {n} rows")


def kernel(x, c, ctx, c_ctx, w_ada, b_ada, pre_attn_g, post_attn_g, pre_ffn_g, post_ffn_g, w_in, attn_sink, q_norm_g,
           kv_norm_g, w_uq, w_ukv, grp_a_g, grp_b_g, w_o, ffn_w1, ffn_w3, ffn_w2, router_w, moe_w1, moe_w3, moe_w2):
    b, n_lat, d = x.shape
    n_ctx = ctx.shape[1]
    depth = w_ada.shape[0]
    assert b == 1 and c.shape[0] == 1, "single-sequence kernel"
    t_all = n_lat + n_ctx
    tm = _tile(t_all, (640, 256, 128))
    tq_a = 256
    tq_b = _tile(n_lat, (1024, 512, 256))
    tk_b = _tile(t_all, (1280, 256))
    tf = 512
    tm_r = _tile(n_lat, (512, 256))
    assert n_ctx == tq_a and n_lat % tq_a == 0 and n_lat >= tq_a + 2 * WINDOW

    xs = jnp.concatenate([x[0], ctx[0]], axis=0)
    cc = jnp.zeros((8, d), F32).at[0].set(c[0]).at[1].set(c_ctx)
    mods = _ada(cc, w_ada, b_ada)
    rope_tab = _rope_tables(n_lat, n_ctx)
    row2 = lambda v: v.reshape(1, -1)

    for i in range(depth):
        last = i == depth - 1
        mod = mods[i]
        win, wuq, wukn, wuvt, wo = _layer_weights(i, w_in, w_uq, w_ukv, w_o)
        qa, ka, va, qb, kb, vbt = _proj(xs, mod, row2(pre_attn_g[i]), win, row2(q_norm_g[i]), row2(kv_norm_g[i]),
                                        wuq, wukn, wuvt, rope_tab, n_lat=n_lat, tm=tm)
        sink_tab = jnp.repeat((attn_sink[i] * LOG2E).reshape(A_KV_HEADS, A_HEADS // A_KV_HEADS), tq_a, axis=1)
        oa = _attn_a(qa, ka, va, sink_tab, n_lat=n_lat, n_ctx=n_ctx, tq=tq_a)
        ob = _mla(qb, kb, vbt, jnp.zeros((t_all, B_WIDTH), BF16), q_row0=0, n_q=n_lat, k_row0=0, n_k=t_all,
                  tq=tq_b, tk=tk_b)
        ob = _mla(qb, kb, vbt, ob, q_row0=n_lat, n_q=n_ctx, k_row0=n_lat, n_k=n_ctx, tq=n_ctx, tk=n_ctx)
        xs = _attn_out(oa, ob, xs, mod, row2(grp_a_g[i]), row2(grp_b_g[i]), wo, row2(post_attn_g[i]),
                       n_lat=n_lat, tm=tm)
        jj = i // 2
        if i % 2 == 0:
            xs = _ffn_dense(xs, mod, row2(pre_ffn_g[i]), row2(post_ffn_g[i]), ffn_w1[jj].astype(BF16),
                            ffn_w3[jj].astype(BF16), ffn_w2[jj].astype(BF16), n_lat=n_lat, tm=tm, tf=tf)
        else:
            n_rows = t_all if not last else n_lat
            assert last, "expert layers other than the last would also need the context rows routed"
            tm_e = tm
            n_tiles = (2 * n_rows + N_EXPERTS * (tm_e - 1)) // tm_e
            rw_pad = jnp.pad(router_w[jj], ((0, 0), (0, LANE - N_EXPERTS)))
            route = _route(xs, mod, row2(pre_ffn_g[i]), rw_pad, n_lat=n_rows, tm=tm_r)
            dest, slot_src, tile_expert, n_used = _routing_slots(route, n_rows, tm_e, n_tiles)
            dest_t = dest.reshape(2, n_rows // tm_r, tm_r).transpose(1, 0, 2)
            xg = _dispatch(xs, slot_src, n_used, tm=tm_e)
            yg = _ffn_moe(xg, tile_expert, n_used, mod, row2(pre_ffn_g[i]), moe_w1[jj].astype(BF16),
                          moe_w3[jj].astype(BF16), moe_w2[jj].astype(BF16), tm=tm_e, tf=tf)
            xs = _combine(yg, dest_t, route, xs, mod, row2(post_ffn_g[i]), n_lat=n_rows, tm=tm_r)
    return xs[:n_lat].reshape(b, n_lat, d)
```

```python
import functools
import math

import jax
import jax.numpy as jnp
from jax import lax
from jax.experimental import pallas as pl
from jax.experimental.pallas import tpu as pltpu

GRID_W = 64
EPS = 1e-6
NEG_INF = -1e30
ROPE_BASE = 10000.0
A_HEADS = 8
A_KV_HEADS = 2
A_HEAD_DIM = 128
WINDOW = 128
B_HEADS = 8
B_Q_RANK = 384
B_KV_RANK = 256
B_NOPE_DIM = 128
B_ROPE_DIM = 64
B_V_DIM = 128
N_EXPERTS = 8
A_WIDTH = A_HEADS * A_HEAD_DIM
A_KV_WIDTH = A_KV_HEADS * A_HEAD_DIM
B_WIDTH = B_HEADS * B_V_DIM
B_QK_PAD = 256
B_VT_ROWS = B_V_DIM + 16

LOG2E = math.log2(math.e)
LANE = 128
VMEM_LIMIT_BYTES = 60 * 1024 * 1024

F32 = jnp.float32
BF16 = jnp.bfloat16
NT_DIMS = (((1,), (1,)), ((), ()))


def _params(sem, vmem=VMEM_LIMIT_BYTES):
    return pltpu.CompilerParams(dimension_semantics=sem, vmem_limit_bytes=vmem)


def _const_spec(shape):
    nd = len(shape)
    return pl.BlockSpec(shape, lambda *_: (0,) * nd, pipeline_mode=pl.Buffered(1))


def _rms(x):
    return x * lax.rsqrt(jnp.mean(x * x, axis=-1, keepdims=True) + EPS)


DMA_UNROLL = 8
ROW_CHUNK = 16
ROW_UNROLL = 8


def _row_chunks(tm, body):
    def step(r, carry):
        body(pl.ds(pl.multiple_of(r * ROW_CHUNK, ROW_CHUNK), ROW_CHUNK), r * ROW_CHUNK)
        return carry

    lax.fori_loop(0, tm // ROW_CHUNK, step, 0, unroll=ROW_UNROLL)


def _mod_row(tile_row0, r0, n_lat):
    return jnp.where(tile_row0 + r0 >= n_lat, 1, 0)


def _norm_mod_rows(x_ref, g_ref, mod_ref, k_shift, tile_row0, n_lat, h_ref, gs_ref):
    tm, d = x_ref.shape
    gs_ref[...] = g_ref[...] * (1.0 + mod_ref[:, (k_shift + 1) * d:(k_shift + 2) * d])

    def body(rows, r0):
        mrow = _mod_row(tile_row0, r0, n_lat)
        h = _rms(x_ref[rows, :]) * gs_ref[pl.ds(mrow, 1), :]
        h_ref[rows, :] = (h + mod_ref[pl.ds(mrow, 1), k_shift * d:(k_shift + 1) * d]).astype(BF16)

    _row_chunks(tm, body)


def _resid_rows(o_ref, y_ref, x_ref, pg_ref, mod_ref, k_gate, tile_row0, n_lat, gs_ref):
    tm, d = x_ref.shape
    gs_ref[...] = pg_ref[...] * mod_ref[:, k_gate * d:(k_gate + 1) * d]

    def body(rows, r0):
        mrow = _mod_row(tile_row0, r0, n_lat)
        o_ref[rows, :] = x_ref[rows, :] + gs_ref[pl.ds(mrow, 1), :] * _rms(y_ref[rows, :])

    _row_chunks(tm, body)


def _is_ctx_rows(tile_index, tm, n_lat):
    row = tile_index * tm + lax.broadcasted_iota(jnp.int32, (tm, 1), 0)
    return row >= n_lat


def _mod_select(mod_ref, k, d, is_ctx):
    return jnp.where(is_ctx, mod_ref[1:2, k * d:(k + 1) * d], mod_ref[0:1, k * d:(k + 1) * d])


def _ada_kernel(c_ref, w_ref, b_ref, o_ref):
    c = c_ref[...]
    a = c * (1.0 / (1.0 + jnp.exp(-c)))
    o_ref[...] = jnp.dot(a, w_ref[...], preferred_element_type=F32, precision=lax.Precision.HIGHEST) + b_ref[...]


def _ada(cc, w_ada, b_ada):
    depth, d, n = w_ada.shape
    tn = n // 8
    return pl.pallas_call(
        _ada_kernel,
        grid=(depth, n // tn),
        in_specs=[pl.BlockSpec((8, d), lambda l, j: (0, 0)),
                  pl.BlockSpec((None, d, tn), lambda l, j: (l, 0, j)),
                  pl.BlockSpec((None, 1, tn), lambda l, j: (l, 0, j))],
        out_specs=pl.BlockSpec((None, 8, tn), lambda l, j: (l, 0, j)),
        out_shape=jax.ShapeDtypeStruct((depth, 8, n), F32),
        compiler_params=_params(("arbitrary", "arbitrary")),
        name="ada_mod",
    )(cc, w_ada, b_ada.reshape(depth, 1, n))


def _rope(t, cos, sin_m, sin_p, quarter):
    n = t.shape[-1]
    return t * cos + pltpu.roll(t, n - quarter, 1) * sin_m + pltpu.roll(t, quarter, 1) * sin_p


def _proj_kernel(x_ref, mod_ref, g_ref, win_ref, qg_ref, kvg_ref, wuq_ref, wukn_ref, wuvt_ref, rope_ref,
                 qa_ref, ka_ref, va_ref, qb_ref, kb_ref, vbt_ref, *, tm, n_lat, scale_a, scale_b):
    d = x_ref.shape[-1]
    is_ctx = _is_ctx_rows(pl.program_id(0), tm, n_lat)
    h = _rms(x_ref[...]) * g_ref[...] * (1.0 + _mod_select(mod_ref, 1, d, is_ctx)) + _mod_select(mod_ref, 0, d, is_ctx)
    proj = jnp.dot(h.astype(BF16), win_ref[...], preferred_element_type=F32)
    cos_a, sinm_a, sinp_a = rope_ref[0], rope_ref[1], rope_ref[2]
    cos_b, sinm_b, sinp_b = rope_ref[3], rope_ref[4], rope_ref[5]
    qa_q = A_HEAD_DIM // 4
    b_q = B_ROPE_DIM // 4
    cos_as, sinm_as, sinp_as = cos_a * scale_a, sinm_a * scale_a, sinp_a * scale_a
    for hh in range(A_HEADS):
        t = proj[:, hh * LANE:(hh + 1) * LANE]
        qa_ref[:, hh * LANE:(hh + 1) * LANE] = _rope(t, cos_as, sinm_as, sinp_as, qa_q).astype(BF16)
    o = A_WIDTH
    for hh in range(A_KV_HEADS):
        t = proj[:, o + hh * LANE:o + (hh + 1) * LANE]
        ka_ref[:, hh * LANE:(hh + 1) * LANE] = _rope(t, cos_a, sinm_a, sinp_a, qa_q).astype(BF16)
    o += A_KV_WIDTH
    va_ref[...] = proj[:, o:o + A_KV_WIDTH].astype(BF16)
    o += A_KV_WIDTH
    cq = (_rms(proj[:, o:o + B_Q_RANK]) * qg_ref[...]).astype(BF16)
    o += B_Q_RANK
    qm = jnp.dot(cq, wuq_ref[...], preferred_element_type=F32)
    cos_bs, sinm_bs, sinp_bs = cos_b * scale_b, sinm_b * scale_b, sinp_b * scale_b
    for hh in range(B_HEADS):
        b0 = hh * B_QK_PAD
        qb_ref[hh, :, 0:LANE] = (qm[:, b0:b0 + LANE] * scale_b).astype(BF16)
        qb_ref[hh, :, LANE:2 * LANE] = _rope(qm[:, b0 + LANE:b0 + 2 * LANE], cos_bs, sinm_bs, sinp_bs, b_q).astype(BF16)
    ckv = (_rms(proj[:, o:o + B_KV_RANK]) * kvg_ref[...]).astype(BF16)
    o += B_KV_RANK
    kr = _rope(proj[:, o:o + LANE], cos_b, sinm_b, sinp_b, b_q).astype(BF16)
    kn = jnp.dot(ckv, wukn_ref[...], preferred_element_type=F32)
    vt = lax.dot_general(wuvt_ref[...], ckv, NT_DIMS, preferred_element_type=F32)
    for hh in range(B_HEADS):
        kb_ref[hh, :, 0:LANE] = kn[:, hh * LANE:(hh + 1) * LANE].astype(BF16)
        kb_ref[hh, :, LANE:2 * LANE] = kr
        vbt_ref[hh, 0:B_V_DIM, :] = vt[hh * B_V_DIM:(hh + 1) * B_V_DIM, :].astype(BF16)
        vbt_ref[hh, B_V_DIM:B_VT_ROWS, :] = jnp.ones((B_VT_ROWS - B_V_DIM, tm), BF16)


def _proj(xs, mod, g, win, qg, kvg, wuq, wukn, wuvt, rope_tab, *, n_lat, tm):
    t_all, d = xs.shape
    row = lambda i: (i, 0)
    kern = functools.partial(_proj_kernel, tm=tm, n_lat=n_lat, scale_a=LOG2E / math.sqrt(A_HEAD_DIM),
                             scale_b=LOG2E / math.sqrt(B_NOPE_DIM + B_ROPE_DIM))
    return pl.pallas_call(
        kern,
        grid=(t_all // tm,),
        in_specs=[pl.BlockSpec((tm, d), row), _const_spec(mod.shape), _const_spec(g.shape), _const_spec(win.shape),
                  _const_spec(qg.shape), _const_spec(kvg.shape), _const_spec(wuq.shape), _const_spec(wukn.shape),
                  _const_spec(wuvt.shape), pl.BlockSpec((6, tm, LANE), lambda i: (0, i, 0))],
        out_specs=[pl.BlockSpec((tm, A_WIDTH), row), pl.BlockSpec((tm, A_KV_WIDTH), row),
                   pl.BlockSpec((tm, A_KV_WIDTH), row),
                   pl.BlockSpec((B_HEADS, tm, B_QK_PAD), lambda i: (0, i, 0)),
                   pl.BlockSpec((B_HEADS, tm, B_QK_PAD), lambda i: (0, i, 0)),
                   pl.BlockSpec((B_HEADS, B_VT_ROWS, tm), lambda i: (0, 0, i))],
        out_shape=[jax.ShapeDtypeStruct((t_all, A_WIDTH), BF16), jax.ShapeDtypeStruct((t_all, A_KV_WIDTH), BF16),
                   jax.ShapeDtypeStruct((t_all, A_KV_WIDTH), BF16),
                   jax.ShapeDtypeStruct((B_HEADS, t_all, B_QK_PAD), BF16),
                   jax.ShapeDtypeStruct((B_HEADS, t_all, B_QK_PAD), BF16),
                   jax.ShapeDtypeStruct((B_HEADS, B_VT_ROWS, t_all), BF16)],
        compiler_params=_params(("arbitrary",)),
        name="pre_attn_proj",
    )(xs, mod, g, win, qg, kvg, wuq, wukn, wuvt, rope_tab)


def _attn_a_kernel(q_ref, k_ref, v_ref, sink_ref, o_ref, *, tq, n_lat, n_ctx):
    i = pl.program_id(0)
    n_lat_tiles = n_lat // tq
    win = tq + 2 * WINDOW
    grp = A_HEADS // A_KV_HEADS
    kc = k_ref[n_lat:n_lat + n_ctx, :]
    vc = v_ref[n_lat:n_lat + n_ctx, :]
    tn_dims = (((0,), (0,)), ((), ()))

    def group_out(g, kw, vw, mask):
        gs = slice(g * A_HEAD_DIM, (g + 1) * A_HEAD_DIM)
        heads = range(g * grp, (g + 1) * grp)
        q = jnp.concatenate([q_ref[:, hh * A_HEAD_DIM:(hh + 1) * A_HEAD_DIM] for hh in heads], axis=0)
        sink = sink_ref[g:g + 1, :]
        s_ctx = lax.dot_general(kc[:, gs], q, NT_DIMS, preferred_element_type=F32)
        m = jnp.maximum(_col_max(s_ctx), sink)
        if kw is not None:
            s_loc = lax.dot_general(kw[:, gs], q, NT_DIMS, preferred_element_type=F32)
            s_loc = jnp.concatenate([jnp.where(mask, s_loc[:, a * tq:(a + 1) * tq], NEG_INF) for a in range(grp)],
                                    axis=1)
            m = jnp.maximum(m, _col_max(s_loc))
        p_ctx = jnp.exp2(s_ctx - m)
        den = _col_sum(p_ctx) + jnp.exp2(sink - m)
        acc = lax.dot_general(vc[:, gs], p_ctx.astype(BF16), tn_dims, preferred_element_type=F32)
        if kw is not None:
            p_loc = jnp.exp2(s_loc - m)
            den = den + _col_sum(p_loc)
            acc = acc + lax.dot_general(vw[:, gs], p_loc.astype(BF16), tn_dims, preferred_element_type=F32)
        o = acc * (1.0 / den)
        for a, hh in enumerate(heads):
            o_ref[:, hh * A_HEAD_DIM:(hh + 1) * A_HEAD_DIM] = o[:, a * tq:(a + 1) * tq].T.astype(o_ref.dtype)

    @pl.when(i < n_lat_tiles)
    def _latent():
        q0 = i * tq
        ws = pl.multiple_of(jnp.clip(q0 - WINDOW, 0, n_lat - win), WINDOW)
        kw = k_ref[pl.ds(ws, win), :]
        vw = v_ref[pl.ds(ws, win), :]
        kpos = ws + lax.broadcasted_iota(jnp.int32, (win, tq), 0)
        qpos = q0 + lax.broadcasted_iota(jnp.int32, (win, tq), 1)
        mask = jnp.abs(qpos - kpos) <= WINDOW
        for g in range(A_KV_HEADS):
            group_out(g, kw, vw, mask)

    @pl.when(i >= n_lat_tiles)
    def _context():
        for g in range(A_KV_HEADS):
            group_out(g, None, None, None)


def _attn_a(qa, ka, va, sink_tab, *, n_lat, n_ctx, tq):
    t_all = qa.shape[0]
    kern = functools.partial(_attn_a_kernel, tq=tq, n_lat=n_lat, n_ctx=n_ctx)
    return pl.pallas_call(
        kern,
        grid=(t_all // tq,),
        in_specs=[pl.BlockSpec((tq, A_WIDTH), lambda i: (i, 0)), _const_spec(ka.shape), _const_spec(va.shape),
                  _const_spec(sink_tab.shape)],
        out_specs=pl.BlockSpec((tq, A_WIDTH), lambda i: (i, 0)),
        out_shape=jax.ShapeDtypeStruct((t_all, A_WIDTH), BF16),
        compiler_params=_params(("arbitrary",)),
        name="attn_a",
    )(qa, ka, va, sink_tab)


MLA_STRIP = 256


def _col_fold(st, pair_op):
    rows = st.shape[0]
    groups = 4 if rows % 32 == 0 else 1
    part = st[0:rows // groups]
    for g in range(1, groups):
        part = pair_op(part, st[g * (rows // groups):(g + 1) * (rows // groups)])
    return part


def _col_max(st):
    return jnp.max(_col_fold(st, jnp.maximum), axis=0, keepdims=True)


def _col_sum(st):
    return jnp.sum(_col_fold(st, jnp.add), axis=0, keepdims=True)


def _mla_kernel(q_ref, k_ref, vt_ref, prev_hbm, o_ref, acc_ref, m_ref, s_ref, *, nk, tq):
    del prev_hbm
    j = pl.program_id(1)
    ns = tq // MLA_STRIP
    pipelined = ns % 2 == 0

    @pl.when(j == 0)
    def _init():
        m_ref[...] = jnp.full(m_ref.shape, NEG_INF, F32)
        acc_ref[...] = jnp.zeros(acc_ref.shape, F32)

    tk = k_ref.shape[1]
    k_half = tk // 2
    v_half = (tk // MLA_STRIP + 1) // 2 * MLA_STRIP if tk > MLA_STRIP else tk

    def scores(hh, c, slot):
        q = q_ref[hh, c * MLA_STRIP:(c + 1) * MLA_STRIP, :]
        for r0, r1 in ((0, k_half), (k_half, tk)):
            s_ref[slot, r0:r1, :] = lax.dot_general(k_ref[hh, r0:r1, :], q, NT_DIMS, preferred_element_type=F32)

    def softmax_pv(hh, c, slot):
        cs = slice(c * MLA_STRIP, (c + 1) * MLA_STRIP)
        st = s_ref[slot]
        m_old = m_ref[hh, :, cs]
        m_new = jnp.maximum(m_old, _col_max(st))
        alpha = jnp.exp2(m_old - m_new)
        p = jnp.exp2(st - m_new).astype(BF16)
        pv = jnp.dot(vt_ref[hh, :, 0:v_half], p[0:v_half], preferred_element_type=F32)
        if v_half < tk:
            pv = pv + jnp.dot(vt_ref[hh, :, v_half:tk], p[v_half:tk], preferred_element_type=F32)
        acc_ref[hh, :, cs] = alpha * acc_ref[hh, :, cs] + pv
        m_ref[hh, :, cs] = m_new

    def head(hh, carry):
        for c in range(ns):
            if not pipelined:
                scores(hh, c, 0)
            elif c + 1 < ns:
                scores(hh, c + 1, (c + 1) % 2)
            else:
                scores(jnp.minimum(hh + 1, B_HEADS - 1), 0, 0)
            softmax_pv(hh, c, c % 2 if pipelined else 0)
        return carry

    if pipelined:
        scores(0, 0, 0)
    lax.fori_loop(0, B_HEADS, head, 0, unroll=2)

    @pl.when(j == nk - 1)
    def _finish():
        for hh in range(B_HEADS):
            o = acc_ref[hh, 0:B_V_DIM, :] * (1.0 / acc_ref[hh, B_V_DIM:B_V_DIM + 1, :])
            o_ref[:, hh * B_V_DIM:(hh + 1) * B_V_DIM] = o.T.astype(o_ref.dtype)


def _mla(qb, kb, vbt, prev, *, q_row0, n_q, k_row0, n_k, tq, tk):
    t_all = qb.shape[1]
    nq, nk = n_q // tq, n_k // tk
    qo, ko = q_row0 // tq, k_row0 // tk
    kern = functools.partial(_mla_kernel, nk=nk, tq=tq)
    return pl.pallas_call(
        kern,
        grid=(nq, nk),
        in_specs=[pl.BlockSpec((B_HEADS, tq, B_QK_PAD), lambda i, j: (0, qo + i, 0)),
                  pl.BlockSpec((B_HEADS, tk, B_QK_PAD), lambda i, j: (0, ko + j, 0)),
                  pl.BlockSpec((B_HEADS, B_VT_ROWS, tk), lambda i, j: (0, 0, ko + j)),
                  pl.BlockSpec(memory_space=pl.ANY)],
        out_specs=pl.BlockSpec((tq, B_WIDTH), lambda i, j: (qo + i, 0)),
        out_shape=jax.ShapeDtypeStruct((t_all, B_WIDTH), BF16),
        scratch_shapes=[pltpu.VMEM((B_HEADS, B_VT_ROWS, tq), F32), pltpu.VMEM((B_HEADS, 1, tq), F32),
                        pltpu.VMEM((2, tk, MLA_STRIP), F32)],
        input_output_aliases={3: 0},
        compiler_params=_params(("arbitrary", "arbitrary")),
        name="mla_attn",
    )(qb, kb, vbt, prev)


def _out_kernel(oa_ref, ob_ref, x_ref, mod_ref, ga_ref, gb_ref, wo_ref, pg_ref, o_ref, *, tm, n_lat):
    d = x_ref.shape[-1]
    is_ctx = _is_ctx_rows(pl.program_id(0), tm, n_lat)
    a = (_rms(oa_ref[...].astype(F32)) * ga_ref[...]).astype(BF16)
    b = (_rms(ob_ref[...].astype(F32)) * gb_ref[...]).astype(BF16)
    y = (jnp.dot(a, wo_ref[0:A_WIDTH, :], preferred_element_type=F32)
         + jnp.dot(b, wo_ref[A_WIDTH:A_WIDTH + B_WIDTH, :], preferred_element_type=F32))
    o_ref[...] = x_ref[...] + _mod_select(mod_ref, 2, d, is_ctx) * (_rms(y) * pg_ref[...])


def _attn_out(oa, ob, xs, mod, ga, gb, wo, pg, *, n_lat, tm):
    t_all, d = xs.shape
    row = lambda i: (i, 0)
    kern = functools.partial(_out_kernel, tm=tm, n_lat=n_lat)
    return pl.pallas_call(
        kern,
        grid=(t_all // tm,),
        in_specs=[pl.BlockSpec((tm, A_WIDTH), row), pl.BlockSpec((tm, B_WIDTH), row), pl.BlockSpec((tm, d), row),
                  _const_spec(mod.shape), _const_spec(ga.shape), _const_spec(gb.shape), _const_spec(wo.shape),
                  _const_spec(pg.shape)],
        out_specs=pl.BlockSpec((tm, d), row),
        out_shape=jax.ShapeDtypeStruct((t_all, d), F32),
        compiler_params=_params(("arbitrary",)),
        name="attn_out_mix",
    )(oa, ob, xs, mod, ga, gb, wo, pg)


def _swiglu_step(x_ref, mod_ref, g_ref, w1_ref, w3_ref, w2_ref, o_ref, h_ref, gs_ref, tile_row0, n_lat,
                 alongside=None):
    j = pl.program_id(1)

    @pl.when(j == 0)
    def _prologue():
        _norm_mod_rows(x_ref, g_ref, mod_ref, 3, tile_row0, n_lat, h_ref, gs_ref)
        o_ref[...] = jnp.zeros(o_ref.shape, o_ref.dtype)

    if alongside is not None:
        alongside()
    h = h_ref[...]
    a = jnp.dot(h, w1_ref[...], preferred_element_type=F32)
    b = jnp.dot(h, w3_ref[...], preferred_element_type=F32)
    u = (a * (1.0 / (1.0 + jnp.exp(-a))) * b).astype(BF16)
    o_ref[...] += jnp.dot(u, w2_ref[...], preferred_element_type=F32)


def _ffn_dense_kernel(x_ref, mod_ref, g_ref, pg_ref, w1_ref, w3_ref, w2_ref, o_ref, h_ref, gs_ref, acc_ref, *, tm,
                      n_lat, nj):
    tile_row0 = pl.program_id(0) * tm
    _swiglu_step(x_ref, mod_ref, g_ref, w1_ref, w3_ref, w2_ref, acc_ref, h_ref, gs_ref, tile_row0, n_lat)

    @pl.when(pl.program_id(1) == nj - 1)
    def _epilogue():
        _resid_rows(o_ref, acc_ref, x_ref, pg_ref, mod_ref, 5, tile_row0, n_lat, gs_ref)


def _ffn_dense(xs, mod, g, pg, w1, w3, w2, *, n_lat, tm, tf):
    t_all, d = xs.shape
    dff = w1.shape[1]
    nj = dff // tf
    kern = functools.partial(_ffn_dense_kernel, tm=tm, n_lat=n_lat, nj=nj)
    return pl.pallas_call(
        kern,
        grid=(t_all // tm, nj),
        in_specs=[pl.BlockSpec((tm, d), lambda i, j: (i, 0)), _const_spec(mod.shape), _const_spec(g.shape),
                  _const_spec(pg.shape),
                  pl.BlockSpec((d, tf), lambda i, j: (0, j)), pl.BlockSpec((d, tf), lambda i, j: (0, j)),
                  pl.BlockSpec((tf, d), lambda i, j: (j, 0))],
        out_specs=pl.BlockSpec((tm, d), lambda i, j: (i, 0)),
        out_shape=jax.ShapeDtypeStruct((t_all, d), F32),
        scratch_shapes=[pltpu.VMEM((tm, d), BF16), pltpu.VMEM((8, d), F32), pltpu.VMEM((tm, d), F32)],
        compiler_params=_params(("arbitrary", "arbitrary")),
        name="ffn_dense",
    )(xs, mod, g, pg, w1, w3, w2)


def _ffn_moe_kernel(te_ref, nu_ref, cur_ref, nxt_ref, x_hbm, mod_ref, g_ref, w1_ref, w3_ref, w2_ref, o_ref, h_ref,
                    gs_ref, xbuf, sem, *, tm, nj):
    t, j = pl.program_id(0), pl.program_id(1)
    n_used = nu_ref[0]
    used = t < n_used
    slot = t % 2
    rows_per_step = tm // nj

    def row_copy(src_ref, r, s):
        return pltpu.make_async_copy(x_hbm.at[pl.ds(src_ref[0, 0, r], 1)], xbuf.at[s, pl.ds(r, 1)], sem.at[s])

    @pl.when((t == 0) & (j == 0))
    def _first_tile():
        lax.fori_loop(0, tm, lambda r, c: (row_copy(cur_ref, r, 0).start(), c)[1], 0, unroll=DMA_UNROLL)

    @pl.when((t <= n_used) & (j == 0))
    def _rows_arrived():
        lax.fori_loop(0, tm, lambda r, c: (row_copy(cur_ref, r, slot).wait(), c)[1], 0, unroll=DMA_UNROLL)

    @pl.when(used)
    def _used():
        def request_next_tile():
            for k in range(rows_per_step):
                row_copy(nxt_ref, j * rows_per_step + k, 1 - slot).start()

        _swiglu_step(xbuf.at[slot], mod_ref, g_ref, w1_ref, w3_ref, w2_ref, o_ref, h_ref, gs_ref, 0, tm,
                     alongside=request_next_tile)

    @pl.when(jnp.logical_not(used) & (j == 0))
    def _idle():
        o_ref[...] = jnp.zeros(o_ref.shape, o_ref.dtype)


def _ffn_moe(xs, slot_src, tile_expert, n_used, mod, g, w1, w3, w2, *, tm, tf):
    d = xs.shape[1]
    n_tiles = slot_src.shape[0]
    dff = w1.shape[2]
    nj = dff // tf
    assert tm % nj == 0, "the next tile's row requests are spread evenly over the d_ff chunks"

    def t_eff(t, nu):
        return jnp.minimum(t, nu[0] - 1)

    def j_eff(t, j, nu):
        return jnp.where(t < nu[0], j, nj - 1)

    smem_rows = lambda fn: pl.BlockSpec((1, 1, tm), fn, memory_space=pltpu.SMEM)
    grid_spec = pltpu.PrefetchScalarGridSpec(
        num_scalar_prefetch=2,
        grid=(n_tiles, nj),
        in_specs=[smem_rows(lambda t, j, te, nu: (t, 0, 0)),
                  smem_rows(lambda t, j, te, nu: (jnp.minimum(t + 1, n_tiles - 1), 0, 0)),
                  pl.BlockSpec(memory_space=pl.ANY),
                  pl.BlockSpec(mod.shape, lambda t, j, te, nu: (0, 0), pipeline_mode=pl.Buffered(1)),
                  pl.BlockSpec(g.shape, lambda t, j, te, nu: (0, 0), pipeline_mode=pl.Buffered(1)),
                  pl.BlockSpec((None, d, tf), lambda t, j, te, nu: (te[t_eff(t, nu)], 0, j_eff(t, j, nu))),
                  pl.BlockSpec((None, d, tf), lambda t, j, te, nu: (te[t_eff(t, nu)], 0, j_eff(t, j, nu))),
                  pl.BlockSpec((None, tf, d), lambda t, j, te, nu: (te[t_eff(t, nu)], j_eff(t, j, nu), 0))],
        out_specs=pl.BlockSpec((tm, d), lambda t, j, te, nu: (t, 0)),
        scratch_shapes=[pltpu.VMEM((tm, d), BF16), pltpu.VMEM((8, d), F32), pltpu.VMEM((2, tm, d), F32),
                        pltpu.SemaphoreType.DMA((2,))],
    )
    return pl.pallas_call(
        functools.partial(_ffn_moe_kernel, tm=tm, nj=nj),
        grid_spec=grid_spec,
        out_shape=jax.ShapeDtypeStruct((n_tiles * tm, d), F32),
        compiler_params=_params(("arbitrary", "arbitrary")),
        name="ffn_moe",
    )(tile_expert, n_used, slot_src, slot_src, xs, mod, g, w1, w3, w2)


def _route_kernel(x_ref, mod_ref, g_ref, rw_ref, o_ref):
    d = x_ref.shape[-1]
    h = _rms(x_ref[...]) * g_ref[...] * (1.0 + mod_ref[0:1, 4 * d:5 * d]) + mod_ref[0:1, 3 * d:4 * d]
    logits = jnp.dot(h, rw_ref[...], preferred_element_type=F32, precision=lax.Precision.HIGHEST)
    lane = lax.broadcasted_iota(jnp.int32, logits.shape, 1)
    lg = jnp.where(lane < N_EXPERTS, logits, -jnp.inf)
    m1 = jnp.max(lg, axis=-1, keepdims=True)
    i1 = jnp.min(jnp.where(lg == m1, lane, LANE), axis=-1, keepdims=True)
    lg2 = jnp.where(lane == i1, -jnp.inf, lg)
    m2 = jnp.max(lg2, axis=-1, keepdims=True)
    i2 = jnp.min(jnp.where(lg2 == m2, lane, LANE), axis=-1, keepdims=True)
    e = jnp.exp(m2 - m1)
    w1 = 1.0 / (1.0 + e)
    w2 = e * w1
    o_ref[...] = jnp.where(lane == 0, i1.astype(F32),
                           jnp.where(lane == 1, i2.astype(F32), jnp.where(lane == 2, w1, jnp.where(lane == 3, w2, 0.0))))


def _route(xs, mod, g, rw_pad, *, n_lat, tm):
    d = xs.shape[1]
    return pl.pallas_call(
        _route_kernel,
        grid=(n_lat // tm,),
        in_specs=[pl.BlockSpec((tm, d), lambda i: (i, 0)), _const_spec(mod.shape), _const_spec(g.shape),
                  _const_spec(rw_pad.shape)],
        out_specs=pl.BlockSpec((tm, LANE), lambda i: (i, 0)),
        out_shape=jax.ShapeDtypeStruct((n_lat, LANE), F32),
        compiler_params=_params(("arbitrary",)),
        name="moe_route",
    )(xs, mod, g, rw_pad)


def _combine_kernel(dest_ref, rt_ref, x_ref, mod_ref, pg_ref, y_hbm, o_ref, buf, sem, *, tm):
    def row_copy(r, k):
        return pltpu.make_async_copy(y_hbm.at[pl.ds(dest_ref[0, k, r], 1)], buf.at[k, pl.ds(r, 1)], sem)

    def start(r, c):
        row_copy(r, 0).start()
        row_copy(r, 1).start()
        return c

    def wait(r, c):
        row_copy(r, 0).wait()
        row_copy(r, 1).wait()
        return c

    lax.fori_loop(0, tm, start, 0, unroll=DMA_UNROLL)
    lax.fori_loop(0, tm, wait, 0, unroll=DMA_UNROLL)
    d = x_ref.shape[-1]
    rt = rt_ref[...]
    y = rt[:, 2:3] * buf[0] + rt[:, 3:4] * buf[1]
    o_ref[...] = x_ref[...] + mod_ref[0:1, 5 * d:6 * d] * (_rms(y) * pg_ref[...])


def _combine(yg, dest, route, xs, mod, pg, *, n_lat, tm):
    d = xs.shape[1]
    kern = functools.partial(_combine_kernel, tm=tm)
    return pl.pallas_call(
        kern,
        grid=(n_lat // tm,),
        in_specs=[pl.BlockSpec((1, 2, tm), lambda i: (i, 0, 0), memory_space=pltpu.SMEM),
                  pl.BlockSpec((tm, LANE), lambda i: (i, 0)), pl.BlockSpec((tm, d), lambda i: (i, 0)),
                  _const_spec(mod.shape), _const_spec(pg.shape), pl.BlockSpec(memory_space=pl.ANY)],
        out_specs=pl.BlockSpec((tm, d), lambda i: (i, 0)),
        out_shape=jax.ShapeDtypeStruct((n_lat, d), F32),
        scratch_shapes=[pltpu.VMEM((2, tm, d), F32), pltpu.SemaphoreType.DMA(())],
        compiler_params=_params(("arbitrary",)),
        name="moe_combine",
    )(dest, route, xs, mod, pg, yg)


def _rope_tables(n_lat, n_ctx):
    rows = n_lat // GRID_W

    def tabs(rot_dim):
        axis_dim = rot_dim // 2
        quarter = rot_dim // 4
        inv_freq = ROPE_BASE ** (-jnp.arange(0, axis_dim, 2, dtype=F32) / axis_dim)
        ang_r = jnp.arange(rows, dtype=F32)[:, None] * inv_freq[None, :]
        ang_c = jnp.arange(GRID_W, dtype=F32)[:, None] * inv_freq[None, :]

        def over_grid(fn):
            r = jnp.broadcast_to(fn(ang_r)[:, None, :], (rows, GRID_W, quarter))
            c = jnp.broadcast_to(fn(ang_c)[None, :, :], (rows, GRID_W, quarter))
            return jnp.concatenate([r, r, c, c], axis=-1).reshape(n_lat, rot_dim)

        cos, sin = over_grid(jnp.cos), over_grid(jnp.sin)
        lower = (jnp.arange(rot_dim) % (2 * quarter)) < quarter
        sin_m = jnp.where(lower[None, :], -sin, 0.0)
        sin_p = jnp.where(lower[None, :], 0.0, sin)
        pad = LANE - rot_dim
        cos = jnp.pad(cos, ((0, n_ctx), (0, pad)), constant_values=1.0)
        sin_m = jnp.pad(sin_m, ((0, n_ctx), (0, pad)))
        sin_p = jnp.pad(sin_p, ((0, n_ctx), (0, pad)))
        return [cos, sin_m, sin_p]

    return jnp.stack(tabs(A_HEAD_DIM) + tabs(B_ROPE_DIM))


def _layer_weights(i, w_in, w_uq, w_ukv, w_o):
    d = w_in.shape[1]
    q_w = A_WIDTH + B_Q_RANK
    wi = w_in[i]
    qa, cq = wi[:, :A_WIDTH], wi[:, A_WIDTH:q_w]
    o1, o2, o3 = q_w + A_KV_WIDTH, q_w + 2 * A_KV_WIDTH, q_w + 2 * A_KV_WIDTH + B_KV_RANK
    ka, va, ckv, kr = wi[:, q_w:o1], wi[:, o1:o2], wi[:, o2:o3], wi[:, o3:]
    win = jnp.concatenate([qa, ka, va, cq, ckv, kr, jnp.zeros((d, LANE - B_ROPE_DIM), F32)], axis=1).astype(BF16)
    uq = w_uq[i].reshape(B_Q_RANK, B_HEADS, B_NOPE_DIM + B_ROPE_DIM)
    uq = jnp.pad(uq, ((0, 0), (0, 0), (0, B_QK_PAD - B_NOPE_DIM - B_ROPE_DIM)))
    wuq = uq.reshape(B_Q_RANK, B_HEADS * B_QK_PAD).astype(BF16)
    ukv = w_ukv[i].reshape(B_KV_RANK, B_HEADS, B_NOPE_DIM + B_V_DIM)
    wukn = ukv[:, :, :B_NOPE_DIM].reshape(B_KV_RANK, B_HEADS * B_NOPE_DIM).astype(BF16)
    wuvt = ukv[:, :, B_NOPE_DIM:].reshape(B_KV_RANK, B_HEADS * B_V_DIM).T.astype(BF16)
    return win, wuq, wukn, wuvt, w_o[i].astype(BF16)


def _routing_slots(route, n_lat, tm_e, n_tiles):
    e12 = route[:, 0:2].astype(jnp.int32).T.reshape(-1)
    onehot = (e12[:, None] == jnp.arange(N_EXPERTS)[None, :]).astype(jnp.int32)
    csum = jnp.cumsum(onehot, axis=0)
    rank = jnp.sum(csum * onehot, axis=1) - 1
    counts = csum[-1]
    tiles_per = (counts + tm_e - 1) // tm_e
    tile_end = jnp.cumsum(tiles_per)
    start = (tile_end - tiles_per) * tm_e
    dest = (jnp.sum(start[None, :] * onehot, axis=1) + rank).astype(jnp.int32)
    n_used = tile_end[-1].astype(jnp.int32).reshape(1)
    t_idx = jnp.arange(n_tiles, dtype=jnp.int32)
    tile_expert = jnp.minimum(jnp.sum((t_idx[:, None] >= tile_end[None, :]).astype(jnp.int32), axis=1),
                              N_EXPERTS - 1).astype(jnp.int32)
    token = jnp.tile(jnp.arange(n_lat, dtype=jnp.int32), 2)
    slot_src = jnp.zeros((n_tiles * tm_e,), jnp.int32).at[dest].set(token, unique_indices=True)
    return dest.reshape(2, n_lat), slot_src.reshape(n_tiles, 1, tm_e), tile_expert, n_used


def _tile(n, candidates):
    for c in candidates:
        if n % c == 0:
            return c
    raise ValueError(f"no supported tile size for {n} rows")


def kernel(x, c, ctx, c_ctx, w_ada, b_ada, pre_attn_g, post_attn_g, pre_ffn_g, post_ffn_g, w_in, attn_sink, q_norm_g,
           kv_norm_g, w_uq, w_ukv, grp_a_g, grp_b_g, w_o, ffn_w1, ffn_w3, ffn_w2, router_w, moe_w1, moe_w3, moe_w2):
    b, n_lat, d = x.shape
    n_ctx = ctx.shape[1]
    depth = w_ada.shape[0]
    assert b == 1 and c.shape[0] == 1, "single-sequence kernel"
    t_all = n_lat + n_ctx
    tm = _tile(t_all, (640, 256, 128))
    tq_a = 256
    tq_b = _tile(n_lat, (1024, 512, 256))
    tk_b = _tile(t_all, (1280, 256))
    tf = 512
    tm_r = _tile(n_lat, (512, 256))
    assert n_ctx == tq_a and n_lat % tq_a == 0 and n_lat >= tq_a + 2 * WINDOW

    xs = jnp.concatenate([x[0], ctx[0]], axis=0)
    cc = jnp.zeros((8, d), F32).at[0].set(c[0]).at[1].set(c_ctx)
    mods = _ada(cc, w_ada, b_ada)
    rope_tab = _rope_tables(n_lat, n_ctx)
    row2 = lambda v: v.reshape(1, -1)

    for i in range(depth):
        last = i == depth - 1
        mod = mods[i]
        win, wuq, wukn, wuvt, wo = _layer_weights(i, w_in, w_uq, w_ukv, w_o)
        qa, ka, va, qb, kb, vbt = _proj(xs, mod, row2(pre_attn_g[i]), win, row2(q_norm_g[i]), row2(kv_norm_g[i]),
                                        wuq, wukn, wuvt, rope_tab, n_lat=n_lat, tm=tm)
        sink_tab = jnp.repeat((attn_sink[i] * LOG2E).reshape(A_KV_HEADS, A_HEADS // A_KV_HEADS), tq_a, axis=1)
        oa = _attn_a(qa, ka, va, sink_tab, n_lat=n_lat, n_ctx=n_ctx, tq=tq_a)
        ob = _mla(qb, kb, vbt, jnp.zeros((t_all, B_WIDTH), BF16), q_row0=0, n_q=n_lat, k_row0=0, n_k=t_all,
                  tq=tq_b, tk=tk_b)
        ob = _mla(qb, kb, vbt, ob, q_row0=n_lat, n_q=n_ctx, k_row0=n_lat, n_k=n_ctx, tq=n_ctx, tk=n_ctx)
        xs = _attn_out(oa, ob, xs, mod, row2(grp_a_g[i]), row2(grp_b_g[i]), wo, row2(post_attn_g[i]),
                       n_lat=n_lat, tm=tm)
        jj = i // 2
        if i % 2 == 0:
            xs = _ffn_dense(xs, mod, row2(pre_ffn_g[i]), row2(post_ffn_g[i]), ffn_w1[jj].astype(BF16),
                            ffn_w3[jj].astype(BF16), ffn_w2[jj].astype(BF16), n_lat=n_lat, tm=tm, tf=tf)
        else:
            n_rows = n_lat
            assert last, "expert layers other than the last would also need the context rows routed"
            nj = moe_w1.shape[-1] // tf
            tm_e = -(-tm // (nj * ROW_CHUNK)) * ROW_CHUNK * nj
            n_tiles = (2 * n_rows + N_EXPERTS * (tm_e - 1)) // tm_e + 1
            rw_pad = jnp.pad(router_w[jj], ((0, 0), (0, LANE - N_EXPERTS)))
            route = _route(xs, mod, row2(pre_ffn_g[i]), rw_pad, n_lat=n_rows, tm=tm_r)
            dest, slot_src, tile_expert, n_used = _routing_slots(route, n_rows, tm_e, n_tiles)
            dest_t = dest.reshape(2, n_rows // tm_r, tm_r).transpose(1, 0, 2)
            yg = _ffn_moe(xs, slot_src, tile_expert, n_used, mod, row2(pre_ffn_g[i]), moe_w1[jj].astype(BF16),
                          moe_w3[jj].astype(BF16), moe_w2[jj].astype(BF16), tm=tm_e, tf=tf)
            xs = _combine(yg, dest_t, route, xs, mod, row2(post_ffn_g[i]), n_lat=n_rows, tm=tm_r)
    return xs[:n_lat].reshape(b, n_lat, d)
```

```python
import functools
import math

import jax
import jax.numpy as jnp
from jax import lax
from jax.experimental import pallas as pl
from jax.experimental.pallas import tpu as pltpu

GRID_W = 64
EPS = 1e-6
NEG_INF = -1e30
ROPE_BASE = 10000.0
A_HEADS = 8
A_KV_HEADS = 2
A_HEAD_DIM = 128
WINDOW = 128
B_HEADS = 8
B_Q_RANK = 384
B_KV_RANK = 256
B_NOPE_DIM = 128
B_ROPE_DIM = 64
B_V_DIM = 128
N_EXPERTS = 8
A_WIDTH = A_HEADS * A_HEAD_DIM
A_KV_WIDTH = A_KV_HEADS * A_HEAD_DIM
B_WIDTH = B_HEADS * B_V_DIM
B_QK_PAD = 256
B_VT_ROWS = B_V_DIM + 16

LOG2E = math.log2(math.e)
LANE = 128
VMEM_LIMIT_BYTES = 60 * 1024 * 1024

F32 = jnp.float32
BF16 = jnp.bfloat16
NT_DIMS = (((1,), (1,)), ((), ()))


def _params(sem, vmem=VMEM_LIMIT_BYTES):
    return pltpu.CompilerParams(dimension_semantics=sem, vmem_limit_bytes=vmem)


def _const_spec(shape):
    nd = len(shape)
    return pl.BlockSpec(shape, lambda *_: (0,) * nd, pipeline_mode=pl.Buffered(1))


def _rms(x):
    return x * lax.rsqrt(jnp.mean(x * x, axis=-1, keepdims=True) + EPS)


DMA_UNROLL = 8
ROW_CHUNK = 16
ROW_UNROLL = 8


def _row_chunks(tm, body):
    def step(r, carry):
        body(pl.ds(pl.multiple_of(r * ROW_CHUNK, ROW_CHUNK), ROW_CHUNK), r * ROW_CHUNK)
        return carry

    lax.fori_loop(0, tm // ROW_CHUNK, step, 0, unroll=ROW_UNROLL)


def _mod_row(tile_row0, r0, n_lat):
    return jnp.where(tile_row0 + r0 >= n_lat, 1, 0)


def _norm_mod_rows(x_ref, g_ref, mod_ref, k_shift, tile_row0, n_lat, h_ref, gs_ref):
    tm, d = x_ref.shape
    gs_ref[...] = g_ref[...] * (1.0 + mod_ref[:, (k_shift + 1) * d:(k_shift + 2) * d])

    def body(rows, r0):
        mrow = _mod_row(tile_row0, r0, n_lat)
        h = _rms(x_ref[rows, :]) * gs_ref[pl.ds(mrow, 1), :]
        h_ref[rows, :] = (h + mod_ref[pl.ds(mrow, 1), k_shift * d:(k_shift + 1) * d]).astype(BF16)

    _row_chunks(tm, body)


def _resid_rows(o_ref, y_ref, x_ref, pg_ref, mod_ref, k_gate, tile_row0, n_lat, gs_ref):
    tm, d = x_ref.shape
    gs_ref[...] = pg_ref[...] * mod_ref[:, k_gate * d:(k_gate + 1) * d]

    def body(rows, r0):
        mrow = _mod_row(tile_row0, r0, n_lat)
        o_ref[rows, :] = x_ref[rows, :] + gs_ref[pl.ds(mrow, 1), :] * _rms(y_ref[rows, :])

    _row_chunks(tm, body)


def _is_ctx_rows(tile_index, tm, n_lat):
    row = tile_index * tm + lax.broadcasted_iota(jnp.int32, (tm, 1), 0)
    return row >= n_lat


def _mod_select(mod_ref, k, d, is_ctx):
    return jnp.where(is_ctx, mod_ref[1:2, k * d:(k + 1) * d], mod_ref[0:1, k * d:(k + 1) * d])


def _ada_kernel(c_ref, w_ref, b_ref, o_ref):
    c = c_ref[...]
    a = c * (1.0 / (1.0 + jnp.exp(-c)))
    o_ref[...] = jnp.dot(a, w_ref[...], preferred_element_type=F32, precision=lax.Precision.HIGHEST) + b_ref[...]


def _ada(cc, w_ada, b_ada):
    depth, d, n = w_ada.shape
    tn = n // 8
    return pl.pallas_call(
        _ada_kernel,
        grid=(depth, n // tn),
        in_specs=[pl.BlockSpec((8, d), lambda l, j: (0, 0)),
                  pl.BlockSpec((None, d, tn), lambda l, j: (l, 0, j)),
                  pl.BlockSpec((None, 1, tn), lambda l, j: (l, 0, j))],
        out_specs=pl.BlockSpec((None, 8, tn), lambda l, j: (l, 0, j)),
        out_shape=jax.ShapeDtypeStruct((depth, 8, n), F32),
        compiler_params=_params(("arbitrary", "arbitrary")),
        name="ada_mod",
    )(cc, w_ada, b_ada.reshape(depth, 1, n))


def _rope(t, cos, sin_m, sin_p, quarter):
    n = t.shape[-1]
    return t * cos + pltpu.roll(t, n - quarter, 1) * sin_m + pltpu.roll(t, quarter, 1) * sin_p


def _proj_kernel(x_ref, mod_ref, g_ref, win_ref, qg_ref, kvg_ref, wuq_ref, wukn_ref, wuvt_ref, rope_ref,
                 qa_ref, ka_ref, va_ref, qb_ref, kb_ref, vbt_ref, *, tm, n_lat, scale_a, scale_b):
    d = x_ref.shape[-1]
    is_ctx = _is_ctx_rows(pl.program_id(0), tm, n_lat)
    h = _rms(x_ref[...]) * g_ref[...] * (1.0 + _mod_select(mod_ref, 1, d, is_ctx)) + _mod_select(mod_ref, 0, d, is_ctx)
    proj = jnp.dot(h.astype(BF16), win_ref[...], preferred_element_type=F32)
    cos_a, sinm_a, sinp_a = rope_ref[0], rope_ref[1], rope_ref[2]
    cos_b, sinm_b, sinp_b = rope_ref[3], rope_ref[4], rope_ref[5]
    qa_q = A_HEAD_DIM // 4
    b_q = B_ROPE_DIM // 4
    cos_as, sinm_as, sinp_as = cos_a * scale_a, sinm_a * scale_a, sinp_a * scale_a
    for hh in range(A_HEADS):
        t = proj[:, hh * LANE:(hh + 1) * LANE]
        qa_ref[:, hh * LANE:(hh + 1) * LANE] = _rope(t, cos_as, sinm_as, sinp_as, qa_q).astype(BF16)
    o = A_WIDTH
    for hh in range(A_KV_HEADS):
        t = proj[:, o + hh * LANE:o + (hh + 1) * LANE]
        ka_ref[:, hh * LANE:(hh + 1) * LANE] = _rope(t, cos_a, sinm_a, sinp_a, qa_q).astype(BF16)
    o += A_KV_WIDTH
    va_ref[...] = proj[:, o:o + A_KV_WIDTH].astype(BF16)
    o += A_KV_WIDTH
    cq = (_rms(proj[:, o:o + B_Q_RANK]) * qg_ref[...]).astype(BF16)
    o += B_Q_RANK
    qm = jnp.dot(cq, wuq_ref[...], preferred_element_type=F32)
    cos_bs, sinm_bs, sinp_bs = cos_b * scale_b, sinm_b * scale_b, sinp_b * scale_b
    for hh in range(B_HEADS):
        b0 = hh * B_QK_PAD
        qb_ref[hh, :, 0:LANE] = (qm[:, b0:b0 + LANE] * scale_b).astype(BF16)
        qb_ref[hh, :, LANE:2 * LANE] = _rope(qm[:, b0 + LANE:b0 + 2 * LANE], cos_bs, sinm_bs, sinp_bs, b_q).astype(BF16)
    ckv = (_rms(proj[:, o:o + B_KV_RANK]) * kvg_ref[...]).astype(BF16)
    o += B_KV_RANK
    kr = _rope(proj[:, o:o + LANE], cos_b, sinm_b, sinp_b, b_q).astype(BF16)
    kn = jnp.dot(ckv, wukn_ref[...], preferred_element_type=F32)
    vt = lax.dot_general(wuvt_ref[...], ckv, NT_DIMS, preferred_element_type=F32)
    for hh in range(B_HEADS):
        kb_ref[hh, :, 0:LANE] = kn[:, hh * LANE:(hh + 1) * LANE].astype(BF16)
        kb_ref[hh, :, LANE:2 * LANE] = kr
        vbt_ref[hh, 0:B_V_DIM, :] = vt[hh * B_V_DIM:(hh + 1) * B_V_DIM, :].astype(BF16)
        vbt_ref[hh, B_V_DIM:B_VT_ROWS, :] = jnp.ones((B_VT_ROWS - B_V_DIM, tm), BF16)


def _proj(xs, mod, g, win, qg, kvg, wuq, wukn, wuvt, rope_tab, *, n_lat, tm):
    t_all, d = xs.shape
    row = lambda i: (i, 0)
    kern = functools.partial(_proj_kernel, tm=tm, n_lat=n_lat, scale_a=LOG2E / math.sqrt(A_HEAD_DIM),
                             scale_b=LOG2E / math.sqrt(B_NOPE_DIM + B_ROPE_DIM))
    return pl.pallas_call(
        kern,
        grid=(t_all // tm,),
        in_specs=[pl.BlockSpec((tm, d), row), _const_spec(mod.shape), _const_spec(g.shape), _const_spec(win.shape),
                  _const_spec(qg.shape), _const_spec(kvg.shape), _const_spec(wuq.shape), _const_spec(wukn.shape),
                  _const_spec(wuvt.shape), pl.BlockSpec((6, tm, LANE), lambda i: (0, i, 0))],
        out_specs=[pl.BlockSpec((tm, A_WIDTH), row), pl.BlockSpec((tm, A_KV_WIDTH), row),
                   pl.BlockSpec((tm, A_KV_WIDTH), row),
                   pl.BlockSpec((B_HEADS, tm, B_QK_PAD), lambda i: (0, i, 0)),
                   pl.BlockSpec((B_HEADS, tm, B_QK_PAD), lambda i: (0, i, 0)),
                   pl.BlockSpec((B_HEADS, B_VT_ROWS, tm), lambda i: (0, 0, i))],
        out_shape=[jax.ShapeDtypeStruct((t_all, A_WIDTH), BF16), jax.ShapeDtypeStruct((t_all, A_KV_WIDTH), BF16),
                   jax.ShapeDtypeStruct((t_all, A_KV_WIDTH), BF16),
                   jax.ShapeDtypeStruct((B_HEADS, t_all, B_QK_PAD), BF16),
                   jax.ShapeDtypeStruct((B_HEADS, t_all, B_QK_PAD), BF16),
                   jax.ShapeDtypeStruct((B_HEADS, B_VT_ROWS, t_all), BF16)],
        compiler_params=_params(("arbitrary",)),
        name="pre_attn_proj",
    )(xs, mod, g, win, qg, kvg, wuq, wukn, wuvt, rope_tab)


def _attn_a_kernel(q_ref, k_ref, v_ref, sink_ref, o_ref, *, tq, n_lat, n_ctx):
    i = pl.program_id(0)
    n_lat_tiles = n_lat // tq
    win = tq + 2 * WINDOW
    grp = A_HEADS // A_KV_HEADS
    kc = k_ref[n_lat:n_lat + n_ctx, :]
    vc = v_ref[n_lat:n_lat + n_ctx, :]
    tn_dims = (((0,), (0,)), ((), ()))

    def group_out(g, kw, vw, mask):
        gs = slice(g * A_HEAD_DIM, (g + 1) * A_HEAD_DIM)
        heads = range(g * grp, (g + 1) * grp)
        q = jnp.concatenate([q_ref[:, hh * A_HEAD_DIM:(hh + 1) * A_HEAD_DIM] for hh in heads], axis=0)
        sink = sink_ref[g:g + 1, :]
        s_ctx = lax.dot_general(kc[:, gs], q, NT_DIMS, preferred_element_type=F32)
        m = jnp.maximum(_col_max(s_ctx), sink)
        if kw is not None:
            s_loc = lax.dot_general(kw[:, gs], q, NT_DIMS, preferred_element_type=F32)
            s_loc = jnp.concatenate([jnp.where(mask, s_loc[:, a * tq:(a + 1) * tq], NEG_INF) for a in range(grp)],
                                    axis=1)
            m = jnp.maximum(m, _col_max(s_loc))
        p_ctx = jnp.exp2(s_ctx - m)
        den = _col_sum(p_ctx) + jnp.exp2(sink - m)
        acc = lax.dot_general(vc[:, gs], p_ctx.astype(BF16), tn_dims, preferred_element_type=F32)
        if kw is not None:
            p_loc = jnp.exp2(s_loc - m)
            den = den + _col_sum(p_loc)
            acc = acc + lax.dot_general(vw[:, gs], p_loc.astype(BF16), tn_dims, preferred_element_type=F32)
        o = acc * (1.0 / den)
        for a, hh in enumerate(heads):
            o_ref[:, hh * A_HEAD_DIM:(hh + 1) * A_HEAD_DIM] = o[:, a * tq:(a + 1) * tq].T.astype(o_ref.dtype)

    @pl.when(i < n_lat_tiles)
    def _latent():
        q0 = i * tq
        ws = pl.multiple_of(jnp.clip(q0 - WINDOW, 0, n_lat - win), WINDOW)
        kw = k_ref[pl.ds(ws, win), :]
        vw = v_ref[pl.ds(ws, win), :]
        kpos = ws + lax.broadcasted_iota(jnp.int32, (win, tq), 0)
        qpos = q0 + lax.broadcasted_iota(jnp.int32, (win, tq), 1)
        mask = jnp.abs(qpos - kpos) <= WINDOW
        for g in range(A_KV_HEADS):
            group_out(g, kw, vw, mask)

    @pl.when(i >= n_lat_tiles)
    def _context():
        for g in range(A_KV_HEADS):
            group_out(g, None, None, None)


def _attn_a(qa, ka, va, sink_tab, *, n_lat, n_ctx, tq):
    t_all = qa.shape[0]
    kern = functools.partial(_attn_a_kernel, tq=tq, n_lat=n_lat, n_ctx=n_ctx)
    return pl.pallas_call(
        kern,
        grid=(t_all // tq,),
        in_specs=[pl.BlockSpec((tq, A_WIDTH), lambda i: (i, 0)), _const_spec(ka.shape), _const_spec(va.shape),
                  _const_spec(sink_tab.shape)],
        out_specs=pl.BlockSpec((tq, A_WIDTH), lambda i: (i, 0)),
        out_shape=jax.ShapeDtypeStruct((t_all, A_WIDTH), BF16),
        compiler_params=_params(("arbitrary",)),
        name="attn_a",
    )(qa, ka, va, sink_tab)


MLA_STRIP = 256
MLA_HEAD_UNROLL = 4


def _col_fold(st, pair_op):
    rows = st.shape[0]
    groups = 4 if rows % 32 == 0 else 1
    part = st[0:rows // groups]
    for g in range(1, groups):
        part = pair_op(part, st[g * (rows // groups):(g + 1) * (rows // groups)])
    return part


def _col_max(st):
    return jnp.max(_col_fold(st, jnp.maximum), axis=0, keepdims=True)


def _col_sum(st):
    return jnp.sum(_col_fold(st, jnp.add), axis=0, keepdims=True)


def _mla_kernel(q_ref, k_ref, vt_ref, prev_hbm, o_ref, acc_ref, m_ref, s_ref, *, nk, tq):
    del prev_hbm
    j = pl.program_id(1)
    ns = tq // MLA_STRIP

    @pl.when(j == 0)
    def _init():
        m_ref[...] = jnp.full(m_ref.shape, NEG_INF, F32)
        acc_ref[...] = jnp.zeros(acc_ref.shape, F32)

    tk = k_ref.shape[1]
    k_half = tk // 2
    v_half = (tk // MLA_STRIP + 1) // 2 * MLA_STRIP if tk > MLA_STRIP else tk

    def scores(hh, c, slot):
        q = q_ref[hh, c * MLA_STRIP:(c + 1) * MLA_STRIP, :]
        for r0, r1 in ((0, k_half), (k_half, tk)):
            s_ref[slot, r0:r1, :] = lax.dot_general(k_ref[hh, r0:r1, :], q, NT_DIMS, preferred_element_type=F32)

    def softmax_pv(hh, c, slot):
        cs = slice(c * MLA_STRIP, (c + 1) * MLA_STRIP)
        st = s_ref[slot]
        m_old = m_ref[hh, :, cs]
        m_new = jnp.maximum(m_old, _col_max(st))
        alpha = jnp.exp2(m_old - m_new)
        p = jnp.exp2(st - m_new).astype(BF16)
        pv = jnp.dot(vt_ref[hh, :, 0:v_half], p[0:v_half], preferred_element_type=F32)
        if v_half < tk:
            pv = pv + jnp.dot(vt_ref[hh, :, v_half:tk], p[v_half:tk], preferred_element_type=F32)
        acc_ref[hh, :, cs] = alpha * acc_ref[hh, :, cs] + pv
        m_ref[hh, :, cs] = m_new

    pipelined = ns % 2 == 0

    def head(hh, carry):
        for c in range(ns):
            if not pipelined:
                scores(hh, c, 0)
            elif c + 1 < ns:
                scores(hh, c + 1, (c + 1) % 2)
            else:
                scores(jnp.minimum(hh + 1, B_HEADS - 1), 0, 0)
            softmax_pv(hh, c, c % 2 if pipelined else 0)
        return carry

    if pipelined:
        scores(0, 0, 0)
    lax.fori_loop(0, B_HEADS, head, 0, unroll=MLA_HEAD_UNROLL)

    @pl.when(j == nk - 1)
    def _finish():
        for hh in range(B_HEADS):
            o = acc_ref[hh, 0:B_V_DIM, :] * (1.0 / acc_ref[hh, B_V_DIM:B_V_DIM + 1, :])
            o_ref[:, hh * B_V_DIM:(hh + 1) * B_V_DIM] = o.T.astype(o_ref.dtype)


def _mla(qb, kb, vbt, prev, *, q_row0, n_q, k_row0, n_k, tq, tk):
    t_all = qb.shape[1]
    nq, nk = n_q // tq, n_k // tk
    qo, ko = q_row0 // tq, k_row0 // tk
    kern = functools.partial(_mla_kernel, nk=nk, tq=tq)
    return pl.pallas_call(
        kern,
        grid=(nq, nk),
        in_specs=[pl.BlockSpec((B_HEADS, tq, B_QK_PAD), lambda i, j: (0, qo + i, 0)),
                  pl.BlockSpec((B_HEADS, tk, B_QK_PAD), lambda i, j: (0, ko + j, 0)),
                  pl.BlockSpec((B_HEADS, B_VT_ROWS, tk), lambda i, j: (0, 0, ko + j)),
                  pl.BlockSpec(memory_space=pl.ANY)],
        out_specs=pl.BlockSpec((tq, B_WIDTH), lambda i, j: (qo + i, 0)),
        out_shape=jax.ShapeDtypeStruct((t_all, B_WIDTH), BF16),
        scratch_shapes=[pltpu.VMEM((B_HEADS, B_VT_ROWS, tq), F32), pltpu.VMEM((B_HEADS, 1, tq), F32),
                        pltpu.VMEM((2, tk, MLA_STRIP), F32)],
        input_output_aliases={3: 0},
        compiler_params=_params(("arbitrary", "arbitrary")),
        name="mla_attn",
    )(qb, kb, vbt, prev)


def _out_kernel(oa_ref, ob_ref, x_ref, mod_ref, ga_ref, gb_ref, wo_ref, pg_ref, o_ref, *, tm, n_lat):
    d = x_ref.shape[-1]
    is_ctx = _is_ctx_rows(pl.program_id(0), tm, n_lat)
    a = (_rms(oa_ref[...].astype(F32)) * ga_ref[...]).astype(BF16)
    b = (_rms(ob_ref[...].astype(F32)) * gb_ref[...]).astype(BF16)
    y = (jnp.dot(a, wo_ref[0:A_WIDTH, :], preferred_element_type=F32)
         + jnp.dot(b, wo_ref[A_WIDTH:A_WIDTH + B_WIDTH, :], preferred_element_type=F32))
    o_ref[...] = x_ref[...] + _mod_select(mod_ref, 2, d, is_ctx) * (_rms(y) * pg_ref[...])


def _attn_out(oa, ob, xs, mod, ga, gb, wo, pg, *, n_lat, tm):
    t_all, d = xs.shape
    row = lambda i: (i, 0)
    kern = functools.partial(_out_kernel, tm=tm, n_lat=n_lat)
    return pl.pallas_call(
        kern,
        grid=(t_all // tm,),
        in_specs=[pl.BlockSpec((tm, A_WIDTH), row), pl.BlockSpec((tm, B_WIDTH), row), pl.BlockSpec((tm, d), row),
                  _const_spec(mod.shape), _const_spec(ga.shape), _const_spec(gb.shape), _const_spec(wo.shape),
                  _const_spec(pg.shape)],
        out_specs=pl.BlockSpec((tm, d), row),
        out_shape=jax.ShapeDtypeStruct((t_all, d), F32),
        compiler_params=_params(("arbitrary",)),
        name="attn_out_mix",
    )(oa, ob, xs, mod, ga, gb, wo, pg)


def _swiglu_step(x_ref, mod_ref, g_ref, w1_ref, w3_ref, w2_ref, o_ref, h_ref, gs_ref, tile_row0, n_lat,
                 alongside=None):
    j = pl.program_id(1)

    @pl.when(j == 0)
    def _prologue():
        _norm_mod_rows(x_ref, g_ref, mod_ref, 3, tile_row0, n_lat, h_ref, gs_ref)
        o_ref[...] = jnp.zeros(o_ref.shape, o_ref.dtype)

    if alongside is not None:
        alongside()
    h = h_ref[...]
    a = jnp.dot(h, w1_ref[...].astype(BF16), preferred_element_type=F32)
    b = jnp.dot(h, w3_ref[...].astype(BF16), preferred_element_type=F32)
    u = (a * (1.0 / (1.0 + jnp.exp(-a))) * b).astype(BF16)
    o_ref[...] += jnp.dot(u, w2_ref[...], preferred_element_type=F32)


def _ffn_dense_kernel(x_ref, mod_ref, g_ref, pg_ref, w1_ref, w3_ref, w2_ref, o_ref, h_ref, gs_ref, acc_ref, *, tm,
                      n_lat, nj):
    tile_row0 = pl.program_id(0) * tm
    _swiglu_step(x_ref, mod_ref, g_ref, w1_ref, w3_ref, w2_ref, acc_ref, h_ref, gs_ref, tile_row0, n_lat)

    @pl.when(pl.program_id(1) == nj - 1)
    def _epilogue():
        _resid_rows(o_ref, acc_ref, x_ref, pg_ref, mod_ref, 5, tile_row0, n_lat, gs_ref)


def _ffn_dense(xs, mod, g, pg, w1, w3, w2, *, n_lat, tm, tf):
    t_all, d = xs.shape
    dff = w1.shape[1]
    nj = dff // tf
    kern = functools.partial(_ffn_dense_kernel, tm=tm, n_lat=n_lat, nj=nj)
    return pl.pallas_call(
        kern,
        grid=(t_all // tm, nj),
        in_specs=[pl.BlockSpec((tm, d), lambda i, j: (i, 0)), _const_spec(mod.shape), _const_spec(g.shape),
                  _const_spec(pg.shape),
                  pl.BlockSpec((d, tf), lambda i, j: (0, j)), pl.BlockSpec((d, tf), lambda i, j: (0, j)),
                  pl.BlockSpec((tf, d), lambda i, j: (j, 0))],
        out_specs=pl.BlockSpec((tm, d), lambda i, j: (i, 0)),
        out_shape=jax.ShapeDtypeStruct((t_all, d), F32),
        scratch_shapes=[pltpu.VMEM((tm, d), BF16), pltpu.VMEM((8, d), F32), pltpu.VMEM((tm, d), F32)],
        compiler_params=_params(("arbitrary", "arbitrary")),
        name="ffn_dense",
    )(xs, mod, g, pg, w1, w3, w2)


def _ffn_moe_kernel(te_ref, nu_ref, cur_ref, nxt_ref, x_hbm, mod_ref, g_ref, w1_ref, w3_ref, w2_ref, o_ref, h_ref,
                    gs_ref, xbuf, sem, *, tm, nj):
    t, j = pl.program_id(0), pl.program_id(1)
    n_used = nu_ref[0]
    used = t < n_used
    slot = t % 2
    rows_per_step = tm // nj

    def row_copy(src_ref, r, s):
        return pltpu.make_async_copy(x_hbm.at[pl.ds(src_ref[0, 0, r], 1)], xbuf.at[s, pl.ds(r, 1)], sem.at[s])

    @pl.when((t == 0) & (j == 0))
    def _first_tile():
        lax.fori_loop(0, tm, lambda r, c: (row_copy(cur_ref, r, 0).start(), c)[1], 0, unroll=DMA_UNROLL)

    @pl.when((t <= n_used) & (j == 0))
    def _rows_arrived():
        lax.fori_loop(0, tm, lambda r, c: (row_copy(cur_ref, r, slot).wait(), c)[1], 0, unroll=DMA_UNROLL)

    @pl.when(used)
    def _used():
        def request_next_tile():
            for k in range(rows_per_step):
                row_copy(nxt_ref, j * rows_per_step + k, 1 - slot).start()

        _swiglu_step(xbuf.at[slot], mod_ref, g_ref, w1_ref, w3_ref, w2_ref, o_ref, h_ref, gs_ref, 0, tm,
                     alongside=request_next_tile)

    @pl.when(jnp.logical_not(used) & (j == 0))
    def _idle():
        o_ref[...] = jnp.zeros(o_ref.shape, o_ref.dtype)


def _ffn_moe(xs, slot_src, tile_expert, n_used, mod, g, w1, w3, w2, *, tm, tf):
    d = xs.shape[1]
    n_tiles = slot_src.shape[0]
    dff = w1.shape[2]
    nj = dff // tf
    assert tm % nj == 0, "the next tile's row requests are spread evenly over the d_ff chunks"

    def t_eff(t, nu):
        return jnp.minimum(t, nu[0] - 1)

    def j_eff(t, j, nu):
        return jnp.where(t < nu[0], j, nj - 1)

    smem_rows = lambda fn: pl.BlockSpec((1, 1, tm), fn, memory_space=pltpu.SMEM)
    grid_spec = pltpu.PrefetchScalarGridSpec(
        num_scalar_prefetch=2,
        grid=(n_tiles, nj),
        in_specs=[smem_rows(lambda t, j, te, nu: (t, 0, 0)),
                  smem_rows(lambda t, j, te, nu: (jnp.minimum(t + 1, n_tiles - 1), 0, 0)),
                  pl.BlockSpec(memory_space=pl.ANY),
                  pl.BlockSpec(mod.shape, lambda t, j, te, nu: (0, 0), pipeline_mode=pl.Buffered(1)),
                  pl.BlockSpec(g.shape, lambda t, j, te, nu: (0, 0), pipeline_mode=pl.Buffered(1)),
                  pl.BlockSpec((None, d, tf), lambda t, j, te, nu: (te[t_eff(t, nu)], 0, j_eff(t, j, nu))),
                  pl.BlockSpec((None, d, tf), lambda t, j, te, nu: (te[t_eff(t, nu)], 0, j_eff(t, j, nu))),
                  pl.BlockSpec((None, tf, d), lambda t, j, te, nu: (te[t_eff(t, nu)], j_eff(t, j, nu), 0))],
        out_specs=pl.BlockSpec((tm, d), lambda t, j, te, nu: (t, 0)),
        scratch_shapes=[pltpu.VMEM((tm, d), BF16), pltpu.VMEM((8, d), F32), pltpu.VMEM((2, tm, d), F32),
                        pltpu.SemaphoreType.DMA((2,))],
    )
    return pl.pallas_call(
        functools.partial(_ffn_moe_kernel, tm=tm, nj=nj),
        grid_spec=grid_spec,
        out_shape=jax.ShapeDtypeStruct((n_tiles * tm, d), F32),
        compiler_params=_params(("arbitrary", "arbitrary")),
        name="ffn_moe",
    )(tile_expert, n_used, slot_src, slot_src, xs, mod, g, w1, w3, w2)


def _route_kernel(x_ref, mod_ref, g_ref, rw_ref, o_ref):
    d = x_ref.shape[-1]
    h = _rms(x_ref[...]) * g_ref[...] * (1.0 + mod_ref[0:1, 4 * d:5 * d]) + mod_ref[0:1, 3 * d:4 * d]
    h_hi = h.astype(BF16)
    h_lo = (h - h_hi.astype(F32)).astype(BF16)
    hi = jnp.dot(h_hi, rw_ref[0], preferred_element_type=F32)
    logits = hi + pltpu.roll(hi, LANE - N_EXPERTS, 1) + jnp.dot(h_lo, rw_ref[1], preferred_element_type=F32)
    lane = lax.broadcasted_iota(jnp.int32, logits.shape, 1)
    lg = jnp.where(lane < N_EXPERTS, logits, -jnp.inf)
    m1 = jnp.max(lg, axis=-1, keepdims=True)
    i1 = jnp.min(jnp.where(lg == m1, lane, LANE), axis=-1, keepdims=True)
    lg2 = jnp.where(lane == i1, -jnp.inf, lg)
    m2 = jnp.max(lg2, axis=-1, keepdims=True)
    i2 = jnp.min(jnp.where(lg2 == m2, lane, LANE), axis=-1, keepdims=True)
    e = jnp.exp(m2 - m1)
    w1 = 1.0 / (1.0 + e)
    w2 = e * w1
    o_ref[...] = jnp.where(lane == 0, i1.astype(F32),
                           jnp.where(lane == 1, i2.astype(F32), jnp.where(lane == 2, w1, jnp.where(lane == 3, w2, 0.0))))


def _route(xs, mod, g, rw_pad, *, n_lat, tm):
    d = xs.shape[1]
    return pl.pallas_call(
        _route_kernel,
        grid=(n_lat // tm,),
        in_specs=[pl.BlockSpec((tm, d), lambda i: (i, 0)), _const_spec(mod.shape), _const_spec(g.shape),
                  _const_spec(rw_pad.shape)],
        out_specs=pl.BlockSpec((tm, LANE), lambda i: (i, 0)),
        out_shape=jax.ShapeDtypeStruct((n_lat, LANE), F32),
        compiler_params=_params(("arbitrary",)),
        name="moe_route",
    )(xs, mod, g, rw_pad)


def _combine_kernel(dest_ref, rt_ref, x_ref, mod_ref, pg_ref, y_hbm, o_ref, buf, sem, *, tm):
    def row_copy(r, k):
        return pltpu.make_async_copy(y_hbm.at[pl.ds(dest_ref[0, k, r], 1)], buf.at[k, pl.ds(r, 1)], sem)

    def start(r, c):
        row_copy(r, 0).start()
        row_copy(r, 1).start()
        return c

    def wait(r, c):
        row_copy(r, 0).wait()
        row_copy(r, 1).wait()
        return c

    lax.fori_loop(0, tm, start, 0, unroll=DMA_UNROLL)
    lax.fori_loop(0, tm, wait, 0, unroll=DMA_UNROLL)
    d = x_ref.shape[-1]
    rt = rt_ref[...]
    y = rt[:, 2:3] * buf[0] + rt[:, 3:4] * buf[1]
    o_ref[...] = x_ref[...] + mod_ref[0:1, 5 * d:6 * d] * (_rms(y) * pg_ref[...])


def _combine(yg, dest, route, xs, mod, pg, *, n_lat, tm):
    d = xs.shape[1]
    kern = functools.partial(_combine_kernel, tm=tm)
    return pl.pallas_call(
        kern,
        grid=(n_lat // tm,),
        in_specs=[pl.BlockSpec((1, 2, tm), lambda i: (i, 0, 0), memory_space=pltpu.SMEM),
                  pl.BlockSpec((tm, LANE), lambda i: (i, 0)), pl.BlockSpec((tm, d), lambda i: (i, 0)),
                  _const_spec(mod.shape), _const_spec(pg.shape), pl.BlockSpec(memory_space=pl.ANY)],
        out_specs=pl.BlockSpec((tm, d), lambda i: (i, 0)),
        out_shape=jax.ShapeDtypeStruct((n_lat, d), F32),
        scratch_shapes=[pltpu.VMEM((2, tm, d), F32), pltpu.SemaphoreType.DMA(())],
        compiler_params=_params(("arbitrary",)),
        name="moe_combine",
    )(dest, route, xs, mod, pg, yg)


def _rope_tables(n_lat, n_ctx):
    rows = n_lat // GRID_W

    def tabs(rot_dim):
        axis_dim = rot_dim // 2
        quarter = rot_dim // 4
        inv_freq = ROPE_BASE ** (-jnp.arange(0, axis_dim, 2, dtype=F32) / axis_dim)
        ang_r = jnp.arange(rows, dtype=F32)[:, None] * inv_freq[None, :]
        ang_c = jnp.arange(GRID_W, dtype=F32)[:, None] * inv_freq[None, :]

        def over_grid(fn):
            r = jnp.broadcast_to(fn(ang_r)[:, None, :], (rows, GRID_W, quarter))
            c = jnp.broadcast_to(fn(ang_c)[None, :, :], (rows, GRID_W, quarter))
            return jnp.concatenate([r, r, c, c], axis=-1).reshape(n_lat, rot_dim)

        cos, sin = over_grid(jnp.cos), over_grid(jnp.sin)
        lower = (jnp.arange(rot_dim) % (2 * quarter)) < quarter
        sin_m = jnp.where(lower[None, :], -sin, 0.0)
        sin_p = jnp.where(lower[None, :], 0.0, sin)
        pad = LANE - rot_dim
        cos = jnp.pad(cos, ((0, n_ctx), (0, pad)), constant_values=1.0)
        sin_m = jnp.pad(sin_m, ((0, n_ctx), (0, pad)))
        sin_p = jnp.pad(sin_p, ((0, n_ctx), (0, pad)))
        return [cos, sin_m, sin_p]

    return jnp.stack(tabs(A_HEAD_DIM) + tabs(B_ROPE_DIM))


def _layer_weights(i, w_in, w_uq, w_ukv, w_o):
    d = w_in.shape[1]
    q_w = A_WIDTH + B_Q_RANK
    wi = w_in[i]
    qa, cq = wi[:, :A_WIDTH], wi[:, A_WIDTH:q_w]
    o1, o2, o3 = q_w + A_KV_WIDTH, q_w + 2 * A_KV_WIDTH, q_w + 2 * A_KV_WIDTH + B_KV_RANK
    ka, va, ckv, kr = wi[:, q_w:o1], wi[:, o1:o2], wi[:, o2:o3], wi[:, o3:]
    win = jnp.concatenate([qa, ka, va, cq, ckv, kr, jnp.zeros((d, LANE - B_ROPE_DIM), F32)], axis=1).astype(BF16)
    uq = w_uq[i].reshape(B_Q_RANK, B_HEADS, B_NOPE_DIM + B_ROPE_DIM)
    uq = jnp.pad(uq, ((0, 0), (0, 0), (0, B_QK_PAD - B_NOPE_DIM - B_ROPE_DIM)))
    wuq = uq.reshape(B_Q_RANK, B_HEADS * B_QK_PAD).astype(BF16)
    ukv = w_ukv[i].reshape(B_KV_RANK, B_HEADS, B_NOPE_DIM + B_V_DIM)
    wukn = ukv[:, :, :B_NOPE_DIM].reshape(B_KV_RANK, B_HEADS * B_NOPE_DIM).astype(BF16)
    wuvt = ukv[:, :, B_NOPE_DIM:].reshape(B_KV_RANK, B_HEADS * B_V_DIM).T.astype(BF16)
    return win, wuq, wukn, wuvt, w_o[i].astype(BF16)


def _routing_slots(route, n_lat, tm_e, n_tiles):
    e12 = route[:, 0:2].astype(jnp.int32).T.reshape(-1)
    onehot = (e12[:, None] == jnp.arange(N_EXPERTS)[None, :]).astype(jnp.int32)
    csum = jnp.cumsum(onehot, axis=0)
    rank = jnp.sum(csum * onehot, axis=1) - 1
    counts = csum[-1]
    tiles_per = (counts + tm_e - 1) // tm_e
    tile_end = jnp.cumsum(tiles_per)
    start = (tile_end - tiles_per) * tm_e
    dest = (jnp.sum(start[None, :] * onehot, axis=1) + rank).astype(jnp.int32)
    n_used = tile_end[-1].astype(jnp.int32).reshape(1)
    t_idx = jnp.arange(n_tiles, dtype=jnp.int32)
    tile_expert = jnp.minimum(jnp.sum((t_idx[:, None] >= tile_end[None, :]).astype(jnp.int32), axis=1),
                              N_EXPERTS - 1).astype(jnp.int32)
    token = jnp.tile(jnp.arange(n_lat, dtype=jnp.int32), 2)
    slot_src = jnp.zeros((n_tiles * tm_e,), jnp.int32).at[dest].set(token, unique_indices=True)
    return dest.reshape(2, n_lat), slot_src.reshape(n_tiles, 1, tm_e), tile_expert, n_used


def _tile(n, candidates):
    for c in candidates:
        if n % c == 0:
            return c
    raise ValueError(f"no supported tile size for {n} rows")


def kernel(x, c, ctx, c_ctx, w_ada, b_ada, pre_attn_g, post_attn_g, pre_ffn_g, post_ffn_g, w_in, attn_sink, q_norm_g,
           kv_norm_g, w_uq, w_ukv, grp_a_g, grp_b_g, w_o, ffn_w1, ffn_w3, ffn_w2, router_w, moe_w1, moe_w3, moe_w2):
    b, n_lat, d = x.shape
    n_ctx = ctx.shape[1]
    depth = w_ada.shape[0]
    assert b == 1 and c.shape[0] == 1, "single-sequence kernel"
    t_all = n_lat + n_ctx
    tm = _tile(t_all, (640, 256, 128))
    tq_a = 256
    tq_b = _tile(n_lat, (1024, 512, 256))
    tk_b = _tile(t_all, (1280, 256))
    tf = 512
    tm_r = _tile(n_lat, (512, 256))
    assert n_ctx == tq_a and n_lat % tq_a == 0 and n_lat >= tq_a + 2 * WINDOW

    xs = jnp.concatenate([x[0], ctx[0]], axis=0)
    cc = jnp.zeros((8, d), F32).at[0].set(c[0]).at[1].set(c_ctx)
    mods = _ada(cc, w_ada, b_ada)
    rope_tab = _rope_tables(n_lat, n_ctx)
    row2 = lambda v: v.reshape(1, -1)

    for i in range(depth):
        last = i == depth - 1
        mod = mods[i]
        win, wuq, wukn, wuvt, wo = _layer_weights(i, w_in, w_uq, w_ukv, w_o)
        qa, ka, va, qb, kb, vbt = _proj(xs, mod, row2(pre_attn_g[i]), win, row2(q_norm_g[i]), row2(kv_norm_g[i]),
                                        wuq, wukn, wuvt, rope_tab, n_lat=n_lat, tm=tm)
        sink_tab = jnp.repeat((attn_sink[i] * LOG2E).reshape(A_KV_HEADS, A_HEADS // A_KV_HEADS), tq_a, axis=1)
        oa = _attn_a(qa, ka, va, sink_tab, n_lat=n_lat, n_ctx=n_ctx, tq=tq_a)
        ob = _mla(qb, kb, vbt, jnp.zeros((t_all, B_WIDTH), BF16), q_row0=0, n_q=n_lat, k_row0=0, n_k=t_all,
                  tq=tq_b, tk=tk_b)
        ob = _mla(qb, kb, vbt, ob, q_row0=n_lat, n_q=n_ctx, k_row0=n_lat, n_k=n_ctx, tq=n_ctx, tk=n_ctx)
        xs = _attn_out(oa, ob, xs, mod, row2(grp_a_g[i]), row2(grp_b_g[i]), wo, row2(post_attn_g[i]),
                       n_lat=n_lat, tm=tm)
        jj = i // 2
        if i % 2 == 0:
            xs = _ffn_dense(xs, mod, row2(pre_ffn_g[i]), row2(post_ffn_g[i]), ffn_w1[jj].astype(BF16),
                            ffn_w3[jj].astype(BF16), ffn_w2[jj].astype(BF16), n_lat=n_lat, tm=tm, tf=tf)
        else:
            n_rows = n_lat
            assert last, "expert layers other than the last would also need the context rows routed"
            nj = moe_w1.shape[-1] // tf
            tm_e = -(-tm // (nj * ROW_CHUNK)) * ROW_CHUNK * nj
            n_tiles = (2 * n_rows + N_EXPERTS * (tm_e - 1)) // tm_e + 1
            rw_hi = router_w[jj].astype(BF16)
            rw_lo = (router_w[jj] - rw_hi.astype(F32)).astype(BF16)
            zeros = jnp.zeros((d, LANE - 2 * N_EXPERTS), BF16)
            rw_pad = jnp.stack([jnp.concatenate([rw_hi, rw_lo, zeros], axis=1),
                                jnp.concatenate([rw_hi, jnp.zeros_like(rw_lo), zeros], axis=1)])
            route = _route(xs, mod, row2(pre_ffn_g[i]), rw_pad, n_lat=n_rows, tm=tm_r)
            dest, slot_src, tile_expert, n_used = _routing_slots(route, n_rows, tm_e, n_tiles)
            dest_t = dest.reshape(2, n_rows // tm_r, tm_r).transpose(1, 0, 2)
            yg = _ffn_moe(xs, slot_src, tile_expert, n_used, mod, row2(pre_ffn_g[i]), moe_w1[jj], moe_w3[jj],
                          moe_w2[jj].astype(BF16), tm=tm_e, tf=tf)
            xs = _combine(yg, dest_t, route, xs, mod, row2(post_ffn_g[i]), n_lat=n_rows, tm=tm_r)
    return xs[:n_lat].reshape(b, n_lat, d)
```

```python
import functools
import math

import jax
import jax.numpy as jnp
from jax import lax
from jax.experimental import pallas as pl
from jax.experimental.pallas import tpu as pltpu

GRID_W = 64
EPS = 1e-6
NEG_INF = -1e30
ROPE_BASE = 10000.0
A_HEADS = 8
A_KV_HEADS = 2
A_HEAD_DIM = 128
WINDOW = 128
B_HEADS = 8
B_Q_RANK = 384
B_KV_RANK = 256
B_NOPE_DIM = 128
B_ROPE_DIM = 64
B_V_DIM = 128
N_EXPERTS = 8
A_WIDTH = A_HEADS * A_HEAD_DIM
A_KV_WIDTH = A_KV_HEADS * A_HEAD_DIM
B_WIDTH = B_HEADS * B_V_DIM
B_QK_PAD = 256
B_VT_ROWS = B_V_DIM + 16

LOG2E = math.log2(math.e)
LANE = 128
VMEM_LIMIT_BYTES = 60 * 1024 * 1024

F32 = jnp.float32
BF16 = jnp.bfloat16
NT_DIMS = (((1,), (1,)), ((), ()))


def _params(sem, vmem=VMEM_LIMIT_BYTES):
    return pltpu.CompilerParams(dimension_semantics=sem, vmem_limit_bytes=vmem)


def _const_spec(shape):
    nd = len(shape)
    return pl.BlockSpec(shape, lambda *_: (0,) * nd, pipeline_mode=pl.Buffered(1))


def _rms(x):
    return x * lax.rsqrt(jnp.mean(x * x, axis=-1, keepdims=True) + EPS)


DMA_UNROLL = 8
ROW_CHUNK = 16
ROW_UNROLL = 8


def _row_chunks(tm, body):
    def step(r, carry):
        body(pl.ds(pl.multiple_of(r * ROW_CHUNK, ROW_CHUNK), ROW_CHUNK), r * ROW_CHUNK)
        return carry

    lax.fori_loop(0, tm // ROW_CHUNK, step, 0, unroll=ROW_UNROLL)


def _mod_row(tile_row0, r0, n_lat):
    return jnp.where(tile_row0 + r0 >= n_lat, 1, 0)


def _norm_mod_rows(x_ref, g_ref, mod_ref, k_shift, tile_row0, n_lat, h_ref, gs_ref):
    tm, d = x_ref.shape
    gs_ref[...] = g_ref[...] * (1.0 + mod_ref[:, (k_shift + 1) * d:(k_shift + 2) * d])

    def body(rows, r0):
        mrow = _mod_row(tile_row0, r0, n_lat)
        h = _rms(x_ref[rows, :]) * gs_ref[pl.ds(mrow, 1), :]
        h_ref[rows, :] = (h + mod_ref[pl.ds(mrow, 1), k_shift * d:(k_shift + 1) * d]).astype(BF16)

    _row_chunks(tm, body)


def _resid_rows(o_ref, y_ref, x_ref, pg_ref, mod_ref, k_gate, tile_row0, n_lat, gs_ref):
    tm, d = x_ref.shape
    gs_ref[...] = pg_ref[...] * mod_ref[:, k_gate * d:(k_gate + 1) * d]

    def body(rows, r0):
        mrow = _mod_row(tile_row0, r0, n_lat)
        o_ref[rows, :] = x_ref[rows, :] + gs_ref[pl.ds(mrow, 1), :] * _rms(y_ref[rows, :])

    _row_chunks(tm, body)


def _is_ctx_rows(tile_index, tm, n_lat):
    row = tile_index * tm + lax.broadcasted_iota(jnp.int32, (tm, 1), 0)
    return row >= n_lat


def _mod_select(mod_ref, k, d, is_ctx):
    return jnp.where(is_ctx, mod_ref[1:2, k * d:(k + 1) * d], mod_ref[0:1, k * d:(k + 1) * d])


def _ada_kernel(c_ref, w_ref, b_ref, o_ref):
    c = c_ref[...]
    a = c * (1.0 / (1.0 + jnp.exp(-c)))
    o_ref[...] = jnp.dot(a, w_ref[...], preferred_element_type=F32, precision=lax.Precision.HIGHEST) + b_ref[...]


def _ada(cc, w_ada, b_ada):
    depth, d, n = w_ada.shape
    tn = n // 8
    return pl.pallas_call(
        _ada_kernel,
        grid=(depth, n // tn),
        in_specs=[pl.BlockSpec((8, d), lambda l, j: (0, 0)),
                  pl.BlockSpec((None, d, tn), lambda l, j: (l, 0, j)),
                  pl.BlockSpec((None, 1, tn), lambda l, j: (l, 0, j))],
        out_specs=pl.BlockSpec((None, 8, tn), lambda l, j: (l, 0, j)),
        out_shape=jax.ShapeDtypeStruct((depth, 8, n), F32),
        compiler_params=_params(("arbitrary", "arbitrary")),
        name="ada_mod",
    )(cc, w_ada, b_ada.reshape(depth, 1, n))


def _rope(t, cos, sin_m, sin_p, quarter):
    n = t.shape[-1]
    return t * cos + pltpu.roll(t, n - quarter, 1) * sin_m + pltpu.roll(t, quarter, 1) * sin_p


def _proj_kernel(x_ref, mod_ref, g_ref, win_ref, qg_ref, kvg_ref, wuq_ref, wukn_ref, wuvt_ref, rope_ref,
                 qa_ref, ka_ref, va_ref, qb_ref, kb_ref, vbt_ref, *, tm, n_lat, scale_a, scale_b):
    d = x_ref.shape[-1]
    is_ctx = _is_ctx_rows(pl.program_id(0), tm, n_lat)
    h = _rms(x_ref[...]) * g_ref[...] * (1.0 + _mod_select(mod_ref, 1, d, is_ctx)) + _mod_select(mod_ref, 0, d, is_ctx)
    proj = jnp.dot(h.astype(BF16), win_ref[...], preferred_element_type=F32)
    cos_a, sinm_a, sinp_a = rope_ref[0], rope_ref[1], rope_ref[2]
    cos_b, sinm_b, sinp_b = rope_ref[3], rope_ref[4], rope_ref[5]
    qa_q = A_HEAD_DIM // 4
    b_q = B_ROPE_DIM // 4
    cos_as, sinm_as, sinp_as = cos_a * scale_a, sinm_a * scale_a, sinp_a * scale_a
    for hh in range(A_HEADS):
        t = proj[:, hh * LANE:(hh + 1) * LANE]
        qa_ref[:, hh * LANE:(hh + 1) * LANE] = _rope(t, cos_as, sinm_as, sinp_as, qa_q).astype(BF16)
    o = A_WIDTH
    for hh in range(A_KV_HEADS):
        t = proj[:, o + hh * LANE:o + (hh + 1) * LANE]
        ka_ref[:, hh * LANE:(hh + 1) * LANE] = _rope(t, cos_a, sinm_a, sinp_a, qa_q).astype(BF16)
    o += A_KV_WIDTH
    va_ref[...] = proj[:, o:o + A_KV_WIDTH].astype(BF16)
    o += A_KV_WIDTH
    cq = (_rms(proj[:, o:o + B_Q_RANK]) * qg_ref[...]).astype(BF16)
    o += B_Q_RANK
    qm = jnp.dot(cq, wuq_ref[...], preferred_element_type=F32)
    cos_bs, sinm_bs, sinp_bs = cos_b * scale_b, sinm_b * scale_b, sinp_b * scale_b
    for hh in range(B_HEADS):
        b0 = hh * B_QK_PAD
        qb_ref[hh, :, 0:LANE] = (qm[:, b0:b0 + LANE] * scale_b).astype(BF16)
        qb_ref[hh, :, LANE:2 * LANE] = _rope(qm[:, b0 + LANE:b0 + 2 * LANE], cos_bs, sinm_bs, sinp_bs, b_q).astype(BF16)
    ckv = (_rms(proj[:, o:o + B_KV_RANK]) * kvg_ref[...]).astype(BF16)
    o += B_KV_RANK
    kr = _rope(proj[:, o:o + LANE], cos_b, sinm_b, sinp_b, b_q).astype(BF16)
    kn = jnp.dot(ckv, wukn_ref[...], preferred_element_type=F32)
    vt = lax.dot_general(wuvt_ref[...], ckv, NT_DIMS, preferred_element_type=F32)
    for hh in range(B_HEADS):
        kb_ref[hh, :, 0:LANE] = kn[:, hh * LANE:(hh + 1) * LANE].astype(BF16)
        kb_ref[hh, :, LANE:2 * LANE] = kr
        vbt_ref[hh, 0:B_V_DIM, :] = vt[hh * B_V_DIM:(hh + 1) * B_V_DIM, :].astype(BF16)
        vbt_ref[hh, B_V_DIM:B_VT_ROWS, :] = jnp.ones((B_VT_ROWS - B_V_DIM, tm), BF16)


def _proj(xs, mod, g, win, qg, kvg, wuq, wukn, wuvt, rope_tab, *, n_lat, tm):
    t_all, d = xs.shape
    row = lambda i: (i, 0)
    kern = functools.partial(_proj_kernel, tm=tm, n_lat=n_lat, scale_a=LOG2E / math.sqrt(A_HEAD_DIM),
                             scale_b=LOG2E / math.sqrt(B_NOPE_DIM + B_ROPE_DIM))
    return pl.pallas_call(
        kern,
        grid=(t_all // tm,),
        in_specs=[pl.BlockSpec((tm, d), row), _const_spec(mod.shape), _const_spec(g.shape), _const_spec(win.shape),
                  _const_spec(qg.shape), _const_spec(kvg.shape), _const_spec(wuq.shape), _const_spec(wukn.shape),
                  _const_spec(wuvt.shape), pl.BlockSpec((6, tm, LANE), lambda i: (0, i, 0))],
        out_specs=[pl.BlockSpec((tm, A_WIDTH), row), pl.BlockSpec((tm, A_KV_WIDTH), row),
                   pl.BlockSpec((tm, A_KV_WIDTH), row),
                   pl.BlockSpec((B_HEADS, tm, B_QK_PAD), lambda i: (0, i, 0)),
                   pl.BlockSpec((B_HEADS, tm, B_QK_PAD), lambda i: (0, i, 0)),
                   pl.BlockSpec((B_HEADS, B_VT_ROWS, tm), lambda i: (0, 0, i))],
        out_shape=[jax.ShapeDtypeStruct((t_all, A_WIDTH), BF16), jax.ShapeDtypeStruct((t_all, A_KV_WIDTH), BF16),
                   jax.ShapeDtypeStruct((t_all, A_KV_WIDTH), BF16),
                   jax.ShapeDtypeStruct((B_HEADS, t_all, B_QK_PAD), BF16),
                   jax.ShapeDtypeStruct((B_HEADS, t_all, B_QK_PAD), BF16),
                   jax.ShapeDtypeStruct((B_HEADS, B_VT_ROWS, t_all), BF16)],
        compiler_params=_params(("arbitrary",)),
        name="pre_attn_proj",
    )(xs, mod, g, win, qg, kvg, wuq, wukn, wuvt, rope_tab)


def _attn_a_kernel(q_ref, k_ref, v_ref, sink_ref, o_ref, *, tq, n_lat, n_ctx):
    i = pl.program_id(0)
    n_lat_tiles = n_lat // tq
    win = tq + 2 * WINDOW
    grp = A_HEADS // A_KV_HEADS
    kc = k_ref[n_lat:n_lat + n_ctx, :]
    vc = v_ref[n_lat:n_lat + n_ctx, :]
    tn_dims = (((0,), (0,)), ((), ()))

    def group_out(g, kw, vw, mask):
        gs = slice(g * A_HEAD_DIM, (g + 1) * A_HEAD_DIM)
        heads = range(g * grp, (g + 1) * grp)
        q = jnp.concatenate([q_ref[:, hh * A_HEAD_DIM:(hh + 1) * A_HEAD_DIM] for hh in heads], axis=0)
        sink = sink_ref[g:g + 1, :]
        s_ctx = lax.dot_general(kc[:, gs], q, NT_DIMS, preferred_element_type=F32)
        m = jnp.maximum(_col_max(s_ctx), sink)
        if kw is not None:
            s_loc = lax.dot_general(kw[:, gs], q, NT_DIMS, preferred_element_type=F32)
            s_loc = jnp.concatenate([jnp.where(mask, s_loc[:, a * tq:(a + 1) * tq], NEG_INF) for a in range(grp)],
                                    axis=1)
            m = jnp.maximum(m, _col_max(s_loc))
        p_ctx = jnp.exp2(s_ctx - m)
        den = _col_sum(p_ctx) + jnp.exp2(sink - m)
        acc = lax.dot_general(vc[:, gs], p_ctx.astype(BF16), tn_dims, preferred_element_type=F32)
        if kw is not None:
            p_loc = jnp.exp2(s_loc - m)
            den = den + _col_sum(p_loc)
            acc = acc + lax.dot_general(vw[:, gs], p_loc.astype(BF16), tn_dims, preferred_element_type=F32)
        o = acc * (1.0 / den)
        for a, hh in enumerate(heads):
            o_ref[:, hh * A_HEAD_DIM:(hh + 1) * A_HEAD_DIM] = o[:, a * tq:(a + 1) * tq].T.astype(o_ref.dtype)

    @pl.when(i < n_lat_tiles)
    def _latent():
        q0 = i * tq
        ws = pl.multiple_of(jnp.clip(q0 - WINDOW, 0, n_lat - win), WINDOW)
        kw = k_ref[pl.ds(ws, win), :]
        vw = v_ref[pl.ds(ws, win), :]
        kpos = ws + lax.broadcasted_iota(jnp.int32, (win, tq), 0)
        qpos = q0 + lax.broadcasted_iota(jnp.int32, (win, tq), 1)
        mask = jnp.abs(qpos - kpos) <= WINDOW
        for g in range(A_KV_HEADS):
            group_out(g, kw, vw, mask)

    @pl.when(i >= n_lat_tiles)
    def _context():
        for g in range(A_KV_HEADS):
            group_out(g, None, None, None)


def _attn_a(qa, ka, va, sink_tab, *, n_lat, n_ctx, tq):
    t_all = qa.shape[0]
    kern = functools.partial(_attn_a_kernel, tq=tq, n_lat=n_lat, n_ctx=n_ctx)
    return pl.pallas_call(
        kern,
        grid=(t_all // tq,),
        in_specs=[pl.BlockSpec((tq, A_WIDTH), lambda i: (i, 0)), _const_spec(ka.shape), _const_spec(va.shape),
                  _const_spec(sink_tab.shape)],
        out_specs=pl.BlockSpec((tq, A_WIDTH), lambda i: (i, 0)),
        out_shape=jax.ShapeDtypeStruct((t_all, A_WIDTH), BF16),
        compiler_params=_params(("arbitrary",)),
        name="attn_a",
    )(qa, ka, va, sink_tab)


MLA_STRIP = 256
MLA_HEAD_UNROLL = 4


def _col_fold(st, pair_op):
    rows = st.shape[0]
    groups = 4 if rows % 32 == 0 else 1
    part = st[0:rows // groups]
    for g in range(1, groups):
        part = pair_op(part, st[g * (rows // groups):(g + 1) * (rows // groups)])
    return part


def _col_max(st):
    return jnp.max(_col_fold(st, jnp.maximum), axis=0, keepdims=True)


def _col_sum(st):
    return jnp.sum(_col_fold(st, jnp.add), axis=0, keepdims=True)


def _mla_kernel(q_ref, k_ref, vt_ref, prev_hbm, o_ref, acc_ref, m_ref, s_ref, *, nk, tq):
    del prev_hbm
    j = pl.program_id(1)
    ns = tq // MLA_STRIP

    @pl.when(j == 0)
    def _init():
        m_ref[...] = jnp.full(m_ref.shape, NEG_INF, F32)
        acc_ref[...] = jnp.zeros(acc_ref.shape, F32)

    tk = k_ref.shape[1]
    k_half = tk // 2
    v_half = (tk // MLA_STRIP + 1) // 2 * MLA_STRIP if tk > MLA_STRIP else tk

    def scores(hh, c, slot):
        q = q_ref[hh, c * MLA_STRIP:(c + 1) * MLA_STRIP, :]
        for r0, r1 in ((0, k_half), (k_half, tk)):
            s_ref[slot, r0:r1, :] = lax.dot_general(k_ref[hh, r0:r1, :], q, NT_DIMS, preferred_element_type=F32)

    def softmax_pv(hh, c, slot):
        cs = slice(c * MLA_STRIP, (c + 1) * MLA_STRIP)
        st = s_ref[slot]
        m_old = m_ref[hh, :, cs]
        m_new = jnp.maximum(m_old, _col_max(st))
        alpha = jnp.exp2(m_old - m_new)
        p = jnp.exp2(st - m_new).astype(BF16)
        pv = jnp.dot(vt_ref[hh, :, 0:v_half], p[0:v_half], preferred_element_type=F32)
        if v_half < tk:
            pv = pv + jnp.dot(vt_ref[hh, :, v_half:tk], p[v_half:tk], preferred_element_type=F32)
        acc_ref[hh, :, cs] = alpha * acc_ref[hh, :, cs] + pv
        m_ref[hh, :, cs] = m_new

    pipelined = ns % 2 == 0

    def head(hh, carry):
        for c in range(ns):
            if not pipelined:
                scores(hh, c, 0)
            elif c + 1 < ns:
                scores(hh, c + 1, (c + 1) % 2)
            else:
                scores(jnp.minimum(hh + 1, B_HEADS - 1), 0, 0)
            softmax_pv(hh, c, c % 2 if pipelined else 0)
        return carry

    if pipelined:
        scores(0, 0, 0)
    lax.fori_loop(0, B_HEADS, head, 0, unroll=MLA_HEAD_UNROLL)

    @pl.when(j == nk - 1)
    def _finish():
        for hh in range(B_HEADS):
            o = acc_ref[hh, 0:B_V_DIM, :] * (1.0 / acc_ref[hh, B_V_DIM:B_V_DIM + 1, :])
            o_ref[:, hh * B_V_DIM:(hh + 1) * B_V_DIM] = o.T.astype(o_ref.dtype)


def _mla(qb, kb, vbt, prev, *, q_row0, n_q, k_row0, n_k, tq, tk):
    t_all = qb.shape[1]
    nq, nk = n_q // tq, n_k // tk
    qo, ko = q_row0 // tq, k_row0 // tk
    kern = functools.partial(_mla_kernel, nk=nk, tq=tq)
    return pl.pallas_call(
        kern,
        grid=(nq, nk),
        in_specs=[pl.BlockSpec((B_HEADS, tq, B_QK_PAD), lambda i, j: (0, qo + i, 0)),
                  pl.BlockSpec((B_HEADS, tk, B_QK_PAD), lambda i, j: (0, ko + j, 0)),
                  pl.BlockSpec((B_HEADS, B_VT_ROWS, tk), lambda i, j: (0, 0, ko + j)),
                  pl.BlockSpec(memory_space=pl.ANY)],
        out_specs=pl.BlockSpec((tq, B_WIDTH), lambda i, j: (qo + i, 0)),
        out_shape=jax.ShapeDtypeStruct((t_all, B_WIDTH), BF16),
        scratch_shapes=[pltpu.VMEM((B_HEADS, B_VT_ROWS, tq), F32), pltpu.VMEM((B_HEADS, 1, tq), F32),
                        pltpu.VMEM((2, tk, MLA_STRIP), F32)],
        input_output_aliases={3: 0},
        compiler_params=_params(("arbitrary", "arbitrary")),
        name="mla_attn",
    )(qb, kb, vbt, prev)


def _out_kernel(oa_ref, ob_ref, x_ref, mod_ref, ga_ref, gb_ref, wo_ref, pg_ref, o_ref, *, tm, n_lat):
    d = x_ref.shape[-1]
    is_ctx = _is_ctx_rows(pl.program_id(0), tm, n_lat)
    a = (_rms(oa_ref[...].astype(F32)) * ga_ref[...]).astype(BF16)
    b = (_rms(ob_ref[...].astype(F32)) * gb_ref[...]).astype(BF16)
    y = (jnp.dot(a, wo_ref[0:A_WIDTH, :], preferred_element_type=F32)
         + jnp.dot(b, wo_ref[A_WIDTH:A_WIDTH + B_WIDTH, :], preferred_element_type=F32))
    o_ref[...] = x_ref[...] + _mod_select(mod_ref, 2, d, is_ctx) * (_rms(y) * pg_ref[...])


def _attn_out(oa, ob, xs, mod, ga, gb, wo, pg, *, n_lat, tm):
    t_all, d = xs.shape
    row = lambda i: (i, 0)
    kern = functools.partial(_out_kernel, tm=tm, n_lat=n_lat)
    return pl.pallas_call(
        kern,
        grid=(t_all // tm,),
        in_specs=[pl.BlockSpec((tm, A_WIDTH), row), pl.BlockSpec((tm, B_WIDTH), row), pl.BlockSpec((tm, d), row),
                  _const_spec(mod.shape), _const_spec(ga.shape), _const_spec(gb.shape), _const_spec(wo.shape),
                  _const_spec(pg.shape)],
        out_specs=pl.BlockSpec((tm, d), row),
        out_shape=jax.ShapeDtypeStruct((t_all, d), F32),
        compiler_params=_params(("arbitrary",)),
        name="attn_out_mix",
    )(oa, ob, xs, mod, ga, gb, wo, pg)


def _swiglu_step(x_ref, mod_ref, g_ref, w1_ref, w3_ref, w2_ref, o_ref, h_ref, gs_ref, tile_row0, n_lat,
                 alongside=None):
    j = pl.program_id(1)

    @pl.when(j == 0)
    def _prologue():
        _norm_mod_rows(x_ref, g_ref, mod_ref, 3, tile_row0, n_lat, h_ref, gs_ref)
        o_ref[...] = jnp.zeros(o_ref.shape, o_ref.dtype)

    if alongside is not None:
        alongside()
    h = h_ref[...]
    a = jnp.dot(h, w1_ref[...].astype(BF16), preferred_element_type=F32)
    b = jnp.dot(h, w3_ref[...].astype(BF16), preferred_element_type=F32)
    u = (a * (1.0 / (1.0 + jnp.exp(-a))) * b).astype(BF16)
    o_ref[...] += jnp.dot(u, w2_ref[...].astype(BF16), preferred_element_type=F32)


def _ffn_dense_kernel(x_ref, mod_ref, g_ref, pg_ref, w1_ref, w3_ref, w2_ref, o_ref, h_ref, gs_ref, acc_ref, *, tm,
                      n_lat, nj):
    tile_row0 = pl.program_id(0) * tm
    _swiglu_step(x_ref, mod_ref, g_ref, w1_ref, w3_ref, w2_ref, acc_ref, h_ref, gs_ref, tile_row0, n_lat)

    @pl.when(pl.program_id(1) == nj - 1)
    def _epilogue():
        _resid_rows(o_ref, acc_ref, x_ref, pg_ref, mod_ref, 5, tile_row0, n_lat, gs_ref)


def _ffn_dense(xs, mod, g, pg, w1, w3, w2, *, n_lat, tm, tf):
    t_all, d = xs.shape
    dff = w1.shape[1]
    nj = dff // tf
    kern = functools.partial(_ffn_dense_kernel, tm=tm, n_lat=n_lat, nj=nj)
    return pl.pallas_call(
        kern,
        grid=(t_all // tm, nj),
        in_specs=[pl.BlockSpec((tm, d), lambda i, j: (i, 0)), _const_spec(mod.shape), _const_spec(g.shape),
                  _const_spec(pg.shape),
                  pl.BlockSpec((d, tf), lambda i, j: (0, j)), pl.BlockSpec((d, tf), lambda i, j: (0, j)),
                  pl.BlockSpec((tf, d), lambda i, j: (j, 0))],
        out_specs=pl.BlockSpec((tm, d), lambda i, j: (i, 0)),
        out_shape=jax.ShapeDtypeStruct((t_all, d), F32),
        scratch_shapes=[pltpu.VMEM((tm, d), BF16), pltpu.VMEM((8, d), F32), pltpu.VMEM((tm, d), F32)],
        compiler_params=_params(("arbitrary", "arbitrary")),
        name="ffn_dense",
    )(xs, mod, g, pg, w1, w3, w2)


def _ffn_moe_kernel(te_ref, nu_ref, cur_ref, nxt_ref, x_hbm, mod_ref, g_ref, w1_ref, w3_ref, w2_ref, o_ref, h_ref,
                    gs_ref, xbuf, sem, *, tm, nj):
    t, j = pl.program_id(0), pl.program_id(1)
    n_used = nu_ref[0]
    used = t < n_used
    slot = t % 2
    rows_per_step = tm // nj

    def row_copy(src_ref, r, s):
        return pltpu.make_async_copy(x_hbm.at[pl.ds(src_ref[0, 0, r], 1)], xbuf.at[s, pl.ds(r, 1)], sem.at[s])

    @pl.when((t == 0) & (j == 0))
    def _first_tile():
        lax.fori_loop(0, tm, lambda r, c: (row_copy(cur_ref, r, 0).start(), c)[1], 0, unroll=DMA_UNROLL)

    @pl.when((t <= n_used) & (j == 0))
    def _rows_arrived():
        lax.fori_loop(0, tm, lambda r, c: (row_copy(cur_ref, r, slot).wait(), c)[1], 0, unroll=DMA_UNROLL)

    @pl.when(used)
    def _used():
        def request_next_tile():
            for k in range(rows_per_step):
                row_copy(nxt_ref, j * rows_per_step + k, 1 - slot).start()

        _swiglu_step(xbuf.at[slot], mod_ref, g_ref, w1_ref, w3_ref, w2_ref, o_ref, h_ref, gs_ref, 0, tm,
                     alongside=request_next_tile)

    @pl.when(jnp.logical_not(used) & (j == 0))
    def _idle():
        o_ref[...] = jnp.zeros(o_ref.shape, o_ref.dtype)


def _ffn_moe(xs, slot_src, tile_expert, n_used, mod, g, w1, w3, w2, *, tm, tf):
    d = xs.shape[1]
    n_tiles = slot_src.shape[0]
    dff = w1.shape[2]
    nj = dff // tf
    assert tm % nj == 0, "the next tile's row requests are spread evenly over the d_ff chunks"

    def t_eff(t, nu):
        return jnp.minimum(t, nu[0] - 1)

    def j_eff(t, j, nu):
        return jnp.where(t < nu[0], j, nj - 1)

    smem_rows = lambda fn: pl.BlockSpec((1, 1, tm), fn, memory_space=pltpu.SMEM)
    grid_spec = pltpu.PrefetchScalarGridSpec(
        num_scalar_prefetch=2,
        grid=(n_tiles, nj),
        in_specs=[smem_rows(lambda t, j, te, nu: (t, 0, 0)),
                  smem_rows(lambda t, j, te, nu: (jnp.minimum(t + 1, n_tiles - 1), 0, 0)),
                  pl.BlockSpec(memory_space=pl.ANY),
                  pl.BlockSpec(mod.shape, lambda t, j, te, nu: (0, 0), pipeline_mode=pl.Buffered(1)),
                  pl.BlockSpec(g.shape, lambda t, j, te, nu: (0, 0), pipeline_mode=pl.Buffered(1)),
                  pl.BlockSpec((None, d, tf), lambda t, j, te, nu: (te[t_eff(t, nu)], 0, j_eff(t, j, nu))),
                  pl.BlockSpec((None, d, tf), lambda t, j, te, nu: (te[t_eff(t, nu)], 0, j_eff(t, j, nu))),
                  pl.BlockSpec((None, tf, d), lambda t, j, te, nu: (te[t_eff(t, nu)], j_eff(t, j, nu), 0))],
        out_specs=pl.BlockSpec((tm, d), lambda t, j, te, nu: (t, 0)),
        scratch_shapes=[pltpu.VMEM((tm, d), BF16), pltpu.VMEM((8, d), F32), pltpu.VMEM((2, tm, d), F32),
                        pltpu.SemaphoreType.DMA((2,))],
    )
    return pl.pallas_call(
        functools.partial(_ffn_moe_kernel, tm=tm, nj=nj),
        grid_spec=grid_spec,
        out_shape=jax.ShapeDtypeStruct((n_tiles * tm, d), F32),
        compiler_params=_params(("arbitrary", "arbitrary")),
        name="ffn_moe",
    )(tile_expert, n_used, slot_src, slot_src, xs, mod, g, w1, w3, w2)


def _bf16_part(x):
    return lax.bitcast_convert_type(lax.bitcast_convert_type(x, jnp.uint32) & jnp.uint32(0xFFFF0000), F32)


def _route_kernel(x_ref, mod_ref, g_ref, rw_ref, o_ref):
    d = x_ref.shape[-1]
    h = _rms(x_ref[...]) * g_ref[...] * (1.0 + mod_ref[0:1, 4 * d:5 * d]) + mod_ref[0:1, 3 * d:4 * d]
    h_top = _bf16_part(h)
    h_hi, h_lo = h_top.astype(BF16), (h - h_top).astype(BF16)
    hi = jnp.dot(h_hi, rw_ref[0], preferred_element_type=F32)
    logits = hi + pltpu.roll(hi, LANE - N_EXPERTS, 1) + jnp.dot(h_lo, rw_ref[1], preferred_element_type=F32)
    lane = lax.broadcasted_iota(jnp.int32, logits.shape, 1)
    lg = jnp.where(lane < N_EXPERTS, logits, -jnp.inf)
    m1 = jnp.max(lg, axis=-1, keepdims=True)
    i1 = jnp.min(jnp.where(lg == m1, lane, LANE), axis=-1, keepdims=True)
    lg2 = jnp.where(lane == i1, -jnp.inf, lg)
    m2 = jnp.max(lg2, axis=-1, keepdims=True)
    i2 = jnp.min(jnp.where(lg2 == m2, lane, LANE), axis=-1, keepdims=True)
    e = jnp.exp(m2 - m1)
    w1 = 1.0 / (1.0 + e)
    w2 = e * w1
    o_ref[...] = jnp.where(lane == 0, i1.astype(F32),
                           jnp.where(lane == 1, i2.astype(F32), jnp.where(lane == 2, w1, jnp.where(lane == 3, w2, 0.0))))


def _route(xs, mod, g, rw_pad, *, n_lat, tm):
    d = xs.shape[1]
    return pl.pallas_call(
        _route_kernel,
        grid=(n_lat // tm,),
        in_specs=[pl.BlockSpec((tm, d), lambda i: (i, 0)), _const_spec(mod.shape), _const_spec(g.shape),
                  _const_spec(rw_pad.shape)],
        out_specs=pl.BlockSpec((tm, LANE), lambda i: (i, 0)),
        out_shape=jax.ShapeDtypeStruct((n_lat, LANE), F32),
        compiler_params=_params(("arbitrary",)),
        name="moe_route",
    )(xs, mod, g, rw_pad)


def _combine_kernel(cur_ref, nxt_ref, rt_ref, x_ref, mod_ref, pg_ref, y_hbm, o_ref, buf, sem, *, tm, n_steps):
    i = pl.program_id(0)
    slot = i % 2

    def row_copy(dest_ref, r, k, s):
        return pltpu.make_async_copy(y_hbm.at[pl.ds(dest_ref[0, k, r], 1)], buf.at[s, k, pl.ds(r, 1)], sem.at[s])

    def start_tile0(r, c):
        row_copy(cur_ref, r, 0, 0).start()
        row_copy(cur_ref, r, 1, 0).start()
        return c

    def wait(r, c):
        row_copy(cur_ref, r, 0, slot).wait()
        row_copy(cur_ref, r, 1, slot).wait()
        return c

    @pl.when(i == 0)
    def _first_tile():
        lax.fori_loop(0, tm, start_tile0, 0, unroll=DMA_UNROLL)

    lax.fori_loop(0, tm, wait, 0, unroll=DMA_UNROLL)

    @pl.when(i < n_steps)
    def _tile():
        for r in range(tm):
            row_copy(nxt_ref, r, 0, 1 - slot).start()
            row_copy(nxt_ref, r, 1, 1 - slot).start()
        d = x_ref.shape[-1]
        rt = rt_ref[...]
        y = rt[:, 2:3] * buf[slot, 0] + rt[:, 3:4] * buf[slot, 1]
        o_ref[...] = x_ref[...] + mod_ref[0:1, 5 * d:6 * d] * (_rms(y) * pg_ref[...])


def _combine(yg, dest, route, xs, mod, pg, *, n_lat, tm):
    d = xs.shape[1]
    n_steps = n_lat // tm
    kern = functools.partial(_combine_kernel, tm=tm, n_steps=n_steps)
    tile = lambda i: jnp.minimum(i, n_steps - 1)
    nxt_tile = lambda i: jnp.minimum(i + 1, n_steps - 1)
    return pl.pallas_call(
        kern,
        grid=(n_steps + 1,),
        in_specs=[pl.BlockSpec((1, 2, tm), lambda i: (tile(i), 0, 0), memory_space=pltpu.SMEM),
                  pl.BlockSpec((1, 2, tm), lambda i: (nxt_tile(i), 0, 0), memory_space=pltpu.SMEM),
                  pl.BlockSpec((tm, LANE), lambda i: (tile(i), 0)), pl.BlockSpec((tm, d), lambda i: (tile(i), 0)),
                  _const_spec(mod.shape), _const_spec(pg.shape), pl.BlockSpec(memory_space=pl.ANY)],
        out_specs=pl.BlockSpec((tm, d), lambda i: (tile(i), 0)),
        out_shape=jax.ShapeDtypeStruct((n_lat, d), F32),
        scratch_shapes=[pltpu.VMEM((2, 2, tm, d), F32), pltpu.SemaphoreType.DMA((2,))],
        compiler_params=_params(("arbitrary",)),
        name="moe_combine",
    )(dest, dest, route, xs, mod, pg, yg)


def _rope_tables(n_lat, n_ctx):
    rows = n_lat // GRID_W

    def tabs(rot_dim):
        axis_dim = rot_dim // 2
        quarter = rot_dim // 4
        inv_freq = ROPE_BASE ** (-jnp.arange(0, axis_dim, 2, dtype=F32) / axis_dim)
        ang_r = jnp.arange(rows, dtype=F32)[:, None] * inv_freq[None, :]
        ang_c = jnp.arange(GRID_W, dtype=F32)[:, None] * inv_freq[None, :]

        def over_grid(fn):
            r = jnp.broadcast_to(fn(ang_r)[:, None, :], (rows, GRID_W, quarter))
            c = jnp.broadcast_to(fn(ang_c)[None, :, :], (rows, GRID_W, quarter))
            return jnp.concatenate([r, r, c, c], axis=-1).reshape(n_lat, rot_dim)

        cos, sin = over_grid(jnp.cos), over_grid(jnp.sin)
        lower = (jnp.arange(rot_dim) % (2 * quarter)) < quarter
        sin_m = jnp.where(lower[None, :], -sin, 0.0)
        sin_p = jnp.where(lower[None, :], 0.0, sin)
        pad = LANE - rot_dim
        cos = jnp.pad(cos, ((0, n_ctx), (0, pad)), constant_values=1.0)
        sin_m = jnp.pad(sin_m, ((0, n_ctx), (0, pad)))
        sin_p = jnp.pad(sin_p, ((0, n_ctx), (0, pad)))
        return [cos, sin_m, sin_p]

    return jnp.stack(tabs(A_HEAD_DIM) + tabs(B_ROPE_DIM))


def _layer_weights(i, w_in, w_uq, w_ukv, w_o):
    d = w_in.shape[1]
    q_w = A_WIDTH + B_Q_RANK
    wi = w_in[i]
    qa, cq = wi[:, :A_WIDTH], wi[:, A_WIDTH:q_w]
    o1, o2, o3 = q_w + A_KV_WIDTH, q_w + 2 * A_KV_WIDTH, q_w + 2 * A_KV_WIDTH + B_KV_RANK
    ka, va, ckv, kr = wi[:, q_w:o1], wi[:, o1:o2], wi[:, o2:o3], wi[:, o3:]
    win = jnp.concatenate([qa, ka, va, cq, ckv, kr, jnp.zeros((d, LANE - B_ROPE_DIM), F32)], axis=1).astype(BF16)
    uq = w_uq[i].reshape(B_Q_RANK, B_HEADS, B_NOPE_DIM + B_ROPE_DIM)
    uq = jnp.pad(uq, ((0, 0), (0, 0), (0, B_QK_PAD - B_NOPE_DIM - B_ROPE_DIM)))
    wuq = uq.reshape(B_Q_RANK, B_HEADS * B_QK_PAD).astype(BF16)
    ukv = w_ukv[i].reshape(B_KV_RANK, B_HEADS, B_NOPE_DIM + B_V_DIM)
    wukn = ukv[:, :, :B_NOPE_DIM].reshape(B_KV_RANK, B_HEADS * B_NOPE_DIM).astype(BF16)
    wuvt = ukv[:, :, B_NOPE_DIM:].reshape(B_KV_RANK, B_HEADS * B_V_DIM).T.astype(BF16)
    return win, wuq, wukn, wuvt, w_o[i].astype(BF16)


def _routing_slots(route, n_lat, tm_e, n_tiles):
    e12 = route[:, 0:2].astype(jnp.int32).T.reshape(-1)
    onehot = (e12[:, None] == jnp.arange(N_EXPERTS)[None, :]).astype(jnp.int32)
    csum = jnp.cumsum(onehot, axis=0)
    rank = jnp.sum(csum * onehot, axis=1) - 1
    counts = csum[-1]
    tiles_per = (counts + tm_e - 1) // tm_e
    tile_end = jnp.cumsum(tiles_per)
    start = (tile_end - tiles_per) * tm_e
    dest = (jnp.sum(start[None, :] * onehot, axis=1) + rank).astype(jnp.int32)
    n_used = tile_end[-1].astype(jnp.int32).reshape(1)
    t_idx = jnp.arange(n_tiles, dtype=jnp.int32)
    tile_expert = jnp.minimum(jnp.sum((t_idx[:, None] >= tile_end[None, :]).astype(jnp.int32), axis=1),
                              N_EXPERTS - 1).astype(jnp.int32)
    token = jnp.tile(jnp.arange(n_lat, dtype=jnp.int32), 2)
    slot_src = jnp.zeros((n_tiles * tm_e,), jnp.int32).at[dest].set(token, unique_indices=True)
    return dest.reshape(2, n_lat), slot_src.reshape(n_tiles, 1, tm_e), tile_expert, n_used


def _tile(n, candidates):
    for c in candidates:
        if n % c == 0:
            return c
    raise ValueError(f"no supported tile size for {n} rows")


def kernel(x, c, ctx, c_ctx, w_ada, b_ada, pre_attn_g, post_attn_g, pre_ffn_g, post_ffn_g, w_in, attn_sink, q_norm_g,
           kv_norm_g, w_uq, w_ukv, grp_a_g, grp_b_g, w_o, ffn_w1, ffn_w3, ffn_w2, router_w, moe_w1, moe_w3, moe_w2):
    b, n_lat, d = x.shape
    n_ctx = ctx.shape[1]
    depth = w_ada.shape[0]
    assert b == 1 and c.shape[0] == 1, "single-sequence kernel"
    t_all = n_lat + n_ctx
    tm = _tile(t_all, (640, 256, 128))
    tq_a = 256
    tq_b = _tile(n_lat, (1024, 512, 256))
    tk_b = _tile(t_all, (1280, 256))
    tf = 512
    tm_r = _tile(n_lat, (512, 256))
    tm_c = 256
    assert n_ctx == tq_a and n_lat % tq_a == 0 and n_lat >= tq_a + 2 * WINDOW

    xs = jnp.concatenate([x[0], ctx[0]], axis=0)
    cc = jnp.zeros((8, d), F32).at[0].set(c[0]).at[1].set(c_ctx)
    mods = _ada(cc, w_ada, b_ada)
    rope_tab = _rope_tables(n_lat, n_ctx)
    row2 = lambda v: v.reshape(1, -1)

    for i in range(depth):
        last = i == depth - 1
        mod = mods[i]
        win, wuq, wukn, wuvt, wo = _layer_weights(i, w_in, w_uq, w_ukv, w_o)
        qa, ka, va, qb, kb, vbt = _proj(xs, mod, row2(pre_attn_g[i]), win, row2(q_norm_g[i]), row2(kv_norm_g[i]),
                                        wuq, wukn, wuvt, rope_tab, n_lat=n_lat, tm=tm)
        sink_tab = jnp.repeat((attn_sink[i] * LOG2E).reshape(A_KV_HEADS, A_HEADS // A_KV_HEADS), tq_a, axis=1)
        oa = _attn_a(qa, ka, va, sink_tab, n_lat=n_lat, n_ctx=n_ctx, tq=tq_a)
        ob = _mla(qb, kb, vbt, jnp.zeros((t_all, B_WIDTH), BF16), q_row0=0, n_q=n_lat, k_row0=0, n_k=t_all,
                  tq=tq_b, tk=tk_b)
        ob = _mla(qb, kb, vbt, ob, q_row0=n_lat, n_q=n_ctx, k_row0=n_lat, n_k=n_ctx, tq=n_ctx, tk=n_ctx)
        xs = _attn_out(oa, ob, xs, mod, row2(grp_a_g[i]), row2(grp_b_g[i]), wo, row2(post_attn_g[i]),
                       n_lat=n_lat, tm=tm)
        jj = i // 2
        if i % 2 == 0:
            xs = _ffn_dense(xs, mod, row2(pre_ffn_g[i]), row2(post_ffn_g[i]), ffn_w1[jj].astype(BF16),
                            ffn_w3[jj].astype(BF16), ffn_w2[jj].astype(BF16), n_lat=n_lat, tm=tm, tf=tf)
        else:
            n_rows = n_lat
            assert last, "expert layers other than the last would also need the context rows routed"
            nj = moe_w1.shape[-1] // tf
            tm_e = -(-tm // (nj * ROW_CHUNK)) * ROW_CHUNK * nj
            n_tiles = (2 * n_rows + N_EXPERTS * (tm_e - 1)) // tm_e + 1
            rw_top = _bf16_part(router_w[jj])
            rw_hi, rw_lo = rw_top.astype(BF16), (router_w[jj] - rw_top).astype(BF16)
            zeros = jnp.zeros((d, LANE - 2 * N_EXPERTS), BF16)
            rw_pad = jnp.stack([jnp.concatenate([rw_hi, rw_lo, zeros], axis=1),
                                jnp.concatenate([rw_hi, jnp.zeros_like(rw_lo), zeros], axis=1)])
            route = _route(xs, mod, row2(pre_ffn_g[i]), rw_pad, n_lat=n_rows, tm=tm_r)
            dest, slot_src, tile_expert, n_used = _routing_slots(route, n_rows, tm_e, n_tiles)
            dest_t = dest.reshape(2, n_rows // tm_c, tm_c).transpose(1, 0, 2)
            yg = _ffn_moe(xs, slot_src, tile_expert, n_used, mod, row2(pre_ffn_g[i]), moe_w1[jj], moe_w3[jj],
                          moe_w2[jj], tm=tm_e, tf=tf)
            xs = _combine(yg, dest_t, route, xs, mod, row2(post_ffn_g[i]), n_lat=n_rows, tm=tm_c)
    return xs[:n_lat].reshape(b, n_lat, d)
```

```python
import functools
import math

import jax
import jax.numpy as jnp
from jax import lax
from jax.experimental import pallas as pl
from jax.experimental.pallas import tpu as pltpu

GRID_W = 64
EPS = 1e-6
NEG_INF = -1e30
ROPE_BASE = 10000.0
A_HEADS = 8
A_KV_HEADS = 2
A_HEAD_DIM = 128
WINDOW = 128
B_HEADS = 8
B_Q_RANK = 384
B_KV_RANK = 256
B_NOPE_DIM = 128
B_ROPE_DIM = 64
B_V_DIM = 128
N_EXPERTS = 8
A_WIDTH = A_HEADS * A_HEAD_DIM
A_KV_WIDTH = A_KV_HEADS * A_HEAD_DIM
B_WIDTH = B_HEADS * B_V_DIM
B_QK_PAD = 256
B_VT_ROWS = B_V_DIM + 16

LOG2E = math.log2(math.e)
LANE = 128
VMEM_LIMIT_BYTES = 60 * 1024 * 1024

F32 = jnp.float32
BF16 = jnp.bfloat16
NT_DIMS = (((1,), (1,)), ((), ()))


def _params(sem, vmem=VMEM_LIMIT_BYTES):
    return pltpu.CompilerParams(dimension_semantics=sem, vmem_limit_bytes=vmem)


def _const_spec(shape):
    nd = len(shape)
    return pl.BlockSpec(shape, lambda *_: (0,) * nd, pipeline_mode=pl.Buffered(1))


def _rms(x):
    return x * lax.rsqrt(jnp.mean(x * x, axis=-1, keepdims=True) + EPS)


DMA_UNROLL = 8
ROW_CHUNK = 16
ROW_UNROLL = 8


def _row_chunks(tm, body):
    def step(r, carry):
        body(pl.ds(pl.multiple_of(r * ROW_CHUNK, ROW_CHUNK), ROW_CHUNK), r * ROW_CHUNK)
        return carry

    lax.fori_loop(0, tm // ROW_CHUNK, step, 0, unroll=ROW_UNROLL)


def _mod_row(tile_row0, r0, n_lat):
    return jnp.where(tile_row0 + r0 >= n_lat, 1, 0)


def _norm_mod_rows(x_ref, g_ref, mod_ref, k_shift, tile_row0, n_lat, h_ref, gs_ref):
    tm, d = x_ref.shape
    gs_ref[...] = g_ref[...] * (1.0 + mod_ref[:, (k_shift + 1) * d:(k_shift + 2) * d])

    def body(rows, r0):
        mrow = _mod_row(tile_row0, r0, n_lat)
        h = _rms(x_ref[rows, :]) * gs_ref[pl.ds(mrow, 1), :]
        h_ref[rows, :] = (h + mod_ref[pl.ds(mrow, 1), k_shift * d:(k_shift + 1) * d]).astype(BF16)

    _row_chunks(tm, body)


def _resid_rows(o_ref, y_ref, x_ref, pg_ref, mod_ref, k_gate, tile_row0, n_lat, gs_ref):
    tm, d = x_ref.shape
    gs_ref[...] = pg_ref[...] * mod_ref[:, k_gate * d:(k_gate + 1) * d]

    def body(rows, r0):
        mrow = _mod_row(tile_row0, r0, n_lat)
        o_ref[rows, :] = x_ref[rows, :] + gs_ref[pl.ds(mrow, 1), :] * _rms(y_ref[rows, :])

    _row_chunks(tm, body)


def _is_ctx_rows(tile_index, tm, n_lat):
    row = tile_index * tm + lax.broadcasted_iota(jnp.int32, (tm, 1), 0)
    return row >= n_lat


def _mod_select(mod_ref, k, d, is_ctx):
    return jnp.where(is_ctx, mod_ref[1:2, k * d:(k + 1) * d], mod_ref[0:1, k * d:(k + 1) * d])


ADA_ROWS = 2


def _ada_kernel(c_ref, w_ref, b_ref, o_ref):
    tn = w_ref.shape[-1]
    o_ref[...] = jnp.zeros(o_ref.shape, o_ref.dtype)
    for r in range(ADA_ROWS):
        c = c_ref[r]
        a = c * (1.0 / (1.0 + jnp.exp(-c)))
        for l in range(tn // LANE):
            cols = slice(l * LANE, (l + 1) * LANE)
            o_ref[r:r + 1, cols] = _col_sum(a * w_ref[:, cols]) + b_ref[:, cols]


def _ada(cc, w_ada, b_ada):
    depth, d, n = w_ada.shape
    tn = n // 8
    c_lanes = jnp.broadcast_to(cc[:, :, None], (ADA_ROWS, d, LANE))
    return pl.pallas_call(
        _ada_kernel,
        grid=(depth, n // tn),
        in_specs=[pl.BlockSpec((ADA_ROWS, d, LANE), lambda l, j: (0, 0, 0)),
                  pl.BlockSpec((None, d, tn), lambda l, j: (l, 0, j)),
                  pl.BlockSpec((None, 1, tn), lambda l, j: (l, 0, j))],
        out_specs=pl.BlockSpec((None, 8, tn), lambda l, j: (l, 0, j)),
        out_shape=jax.ShapeDtypeStruct((depth, 8, n), F32),
        compiler_params=_params(("arbitrary", "arbitrary")),
        name="ada_mod",
    )(c_lanes, w_ada, b_ada.reshape(depth, 1, n))


def _rope(t, cos, sin_m, sin_p, quarter):
    n = t.shape[-1]
    return t * cos + pltpu.roll(t, n - quarter, 1) * sin_m + pltpu.roll(t, quarter, 1) * sin_p


def _proj_kernel(x_ref, mod_ref, g_ref, win_ref, qg_ref, kvg_ref, wuq_ref, wukn_ref, wuvt_ref, rope_ref,
                 qa_ref, ka_ref, va_ref, qb_ref, kb_ref, vbt_ref, *, tm, n_lat, scale_a, scale_b):
    d = x_ref.shape[-1]
    is_ctx = _is_ctx_rows(pl.program_id(0), tm, n_lat)
    h = _rms(x_ref[...]) * g_ref[...] * (1.0 + _mod_select(mod_ref, 1, d, is_ctx)) + _mod_select(mod_ref, 0, d, is_ctx)
    proj = jnp.dot(h.astype(BF16), win_ref[...], preferred_element_type=F32)
    cos_a, sinm_a, sinp_a = rope_ref[0], rope_ref[1], rope_ref[2]
    cos_b, sinm_b, sinp_b = rope_ref[3], rope_ref[4], rope_ref[5]
    qa_q = A_HEAD_DIM // 4
    b_q = B_ROPE_DIM // 4
    cos_as, sinm_as, sinp_as = cos_a * scale_a, sinm_a * scale_a, sinp_a * scale_a
    for hh in range(A_HEADS):
        t = proj[:, hh * LANE:(hh + 1) * LANE]
        qa_ref[:, hh * LANE:(hh + 1) * LANE] = _rope(t, cos_as, sinm_as, sinp_as, qa_q).astype(BF16)
    o = A_WIDTH
    for hh in range(A_KV_HEADS):
        t = proj[:, o + hh * LANE:o + (hh + 1) * LANE]
        ka_ref[:, hh * LANE:(hh + 1) * LANE] = _rope(t, cos_a, sinm_a, sinp_a, qa_q).astype(BF16)
    o += A_KV_WIDTH
    va_ref[...] = proj[:, o:o + A_KV_WIDTH].astype(BF16)
    o += A_KV_WIDTH
    cq = (_rms(proj[:, o:o + B_Q_RANK]) * qg_ref[...]).astype(BF16)
    o += B_Q_RANK
    qm = jnp.dot(cq, wuq_ref[...], preferred_element_type=F32)
    cos_bs, sinm_bs, sinp_bs = cos_b * scale_b, sinm_b * scale_b, sinp_b * scale_b
    for hh in range(B_HEADS):
        b0 = hh * B_QK_PAD
        qb_ref[hh, :, 0:LANE] = (qm[:, b0:b0 + LANE] * scale_b).astype(BF16)
        qb_ref[hh, :, LANE:2 * LANE] = _rope(qm[:, b0 + LANE:b0 + 2 * LANE], cos_bs, sinm_bs, sinp_bs, b_q).astype(BF16)
    ckv = (_rms(proj[:, o:o + B_KV_RANK]) * kvg_ref[...]).astype(BF16)
    o += B_KV_RANK
    kr = _rope(proj[:, o:o + LANE], cos_b, sinm_b, sinp_b, b_q).astype(BF16)
    kn = jnp.dot(ckv, wukn_ref[...], preferred_element_type=F32)
    vt = lax.dot_general(wuvt_ref[...], ckv, NT_DIMS, preferred_element_type=F32)
    for hh in range(B_HEADS):
        kb_ref[hh, :, 0:LANE] = kn[:, hh * LANE:(hh + 1) * LANE].astype(BF16)
        kb_ref[hh, :, LANE:2 * LANE] = kr
        vbt_ref[hh, 0:B_V_DIM, :] = vt[hh * B_V_DIM:(hh + 1) * B_V_DIM, :].astype(BF16)
        vbt_ref[hh, B_V_DIM:B_VT_ROWS, :] = jnp.ones((B_VT_ROWS - B_V_DIM, tm), BF16)


def _proj(xs, mod, g, win, qg, kvg, wuq, wukn, wuvt, rope_tab, *, n_lat, tm):
    t_all, d = xs.shape
    row = lambda i: (i, 0)
    kern = functools.partial(_proj_kernel, tm=tm, n_lat=n_lat, scale_a=LOG2E / math.sqrt(A_HEAD_DIM),
                             scale_b=LOG2E / math.sqrt(B_NOPE_DIM + B_ROPE_DIM))
    return pl.pallas_call(
        kern,
        grid=(t_all // tm,),
        in_specs=[pl.BlockSpec((tm, d), row), _const_spec(mod.shape), _const_spec(g.shape), _const_spec(win.shape),
                  _const_spec(qg.shape), _const_spec(kvg.shape), _const_spec(wuq.shape), _const_spec(wukn.shape),
                  _const_spec(wuvt.shape), pl.BlockSpec((6, tm, LANE), lambda i: (0, i, 0))],
        out_specs=[pl.BlockSpec((tm, A_WIDTH), row), pl.BlockSpec((tm, A_KV_WIDTH), row),
                   pl.BlockSpec((tm, A_KV_WIDTH), row),
                   pl.BlockSpec((B_HEADS, tm, B_QK_PAD), lambda i: (0, i, 0)),
                   pl.BlockSpec((B_HEADS, tm, B_QK_PAD), lambda i: (0, i, 0)),
                   pl.BlockSpec((B_HEADS, B_VT_ROWS, tm), lambda i: (0, 0, i))],
        out_shape=[jax.ShapeDtypeStruct((t_all, A_WIDTH), BF16), jax.ShapeDtypeStruct((t_all, A_KV_WIDTH), BF16),
                   jax.ShapeDtypeStruct((t_all, A_KV_WIDTH), BF16),
                   jax.ShapeDtypeStruct((B_HEADS, t_all, B_QK_PAD), BF16),
                   jax.ShapeDtypeStruct((B_HEADS, t_all, B_QK_PAD), BF16),
                   jax.ShapeDtypeStruct((B_HEADS, B_VT_ROWS, t_all), BF16)],
        compiler_params=_params(("arbitrary",)),
        name="pre_attn_proj",
    )(xs, mod, g, win, qg, kvg, wuq, wukn, wuvt, rope_tab)


def _attn_a_kernel(q_ref, k_ref, v_ref, sink_ref, o_ref, s_ref, *, tq, n_lat, n_ctx):
    i = pl.program_id(0)
    n_lat_tiles = n_lat // tq
    win = tq + 2 * WINDOW
    grp = A_HEADS // A_KV_HEADS
    tn_dims = (((0,), (0,)), ((), ()))

    def run_tile(keys, vals, mask):
        n_keys = keys[0].shape[0]

        def scores(hh, slot):
            q = q_ref[:, hh * A_HEAD_DIM:(hh + 1) * A_HEAD_DIM]
            s_ref[slot, 0:n_keys, :] = lax.dot_general(keys[hh // grp], q, NT_DIMS, preferred_element_type=F32)

        def head_out(hh, slot):
            s = s_ref[slot, 0:n_keys, :]
            if mask is not None:
                s = jnp.concatenate([jnp.where(mask, s[0:win], NEG_INF), s[win:n_keys]], axis=0)
            sink = sink_ref[hh:hh + 1, :]
            m = jnp.maximum(_col_max(s), sink)
            p = jnp.exp2(s - m)
            den = _col_sum(p) + jnp.exp2(sink - m)
            o = lax.dot_general(vals[hh // grp], p.astype(BF16), tn_dims, preferred_element_type=F32)
            o_ref[:, hh * A_HEAD_DIM:(hh + 1) * A_HEAD_DIM] = (o * (1.0 / den)).T.astype(o_ref.dtype)

        ahead = s_ref.shape[0] - 1
        for hh in range(min(ahead, A_HEADS)):
            scores(hh, hh)
        for hh in range(A_HEADS):
            if hh + ahead < A_HEADS:
                scores(hh + ahead, (hh + ahead) % (ahead + 1))
            head_out(hh, hh % (ahead + 1))

    def kv_head(x, g):
        return x[:, g * A_HEAD_DIM:(g + 1) * A_HEAD_DIM]

    kc = k_ref[n_lat:n_lat + n_ctx, :]
    vc = v_ref[n_lat:n_lat + n_ctx, :]

    @pl.when(i < n_lat_tiles)
    def _latent():
        q0 = i * tq
        ws = pl.multiple_of(jnp.clip(q0 - WINDOW, 0, n_lat - win), WINDOW)
        kw = k_ref[pl.ds(ws, win), :]
        vw = v_ref[pl.ds(ws, win), :]
        kpos = ws + lax.broadcasted_iota(jnp.int32, (win, tq), 0)
        qpos = q0 + lax.broadcasted_iota(jnp.int32, (win, tq), 1)
        mask = jnp.abs(qpos - kpos) <= WINDOW
        keys = [jnp.concatenate([kv_head(kw, g), kv_head(kc, g)], axis=0) for g in range(A_KV_HEADS)]
        vals = [jnp.concatenate([kv_head(vw, g), kv_head(vc, g)], axis=0) for g in range(A_KV_HEADS)]
        run_tile(keys, vals, mask)

    @pl.when(i >= n_lat_tiles)
    def _context():
        run_tile([kv_head(kc, g) for g in range(A_KV_HEADS)], [kv_head(vc, g) for g in range(A_KV_HEADS)], None)


def _attn_a(qa, ka, va, sink_tab, *, n_lat, n_ctx, tq):
    t_all = qa.shape[0]
    kern = functools.partial(_attn_a_kernel, tq=tq, n_lat=n_lat, n_ctx=n_ctx)
    return pl.pallas_call(
        kern,
        grid=(t_all // tq,),
        in_specs=[pl.BlockSpec((tq, A_WIDTH), lambda i: (i, 0)), _const_spec(ka.shape), _const_spec(va.shape),
                  _const_spec(sink_tab.shape)],
        out_specs=pl.BlockSpec((tq, A_WIDTH), lambda i: (i, 0)),
        out_shape=jax.ShapeDtypeStruct((t_all, A_WIDTH), BF16),
        scratch_shapes=[pltpu.VMEM((4, tq + 2 * WINDOW + n_ctx, tq), F32)],
        compiler_params=_params(("arbitrary",)),
        name="attn_a",
    )(qa, ka, va, sink_tab)


MLA_STRIP = 256
MLA_UNITS_PER_TRIP = 16


def _col_fold(st, pair_op):
    rows = st.shape[0]
    groups = 4 if rows % 32 == 0 else 1
    part = st[0:rows // groups]
    for g in range(1, groups):
        part = pair_op(part, st[g * (rows // groups):(g + 1) * (rows // groups)])
    return part


def _col_max(st):
    return jnp.max(_col_fold(st, jnp.maximum), axis=0, keepdims=True)


def _col_sum(st):
    return jnp.sum(_col_fold(st, jnp.add), axis=0, keepdims=True)


def _mla_kernel(q_ref, k_ref, vt_ref, prev_hbm, o_ref, acc_ref, m_ref, s_ref, *, nk, tq):
    del prev_hbm
    j = pl.program_id(1)
    ns = tq // MLA_STRIP

    @pl.when(j == 0)
    def _init():
        m_ref[...] = jnp.full(m_ref.shape, NEG_INF, F32)
        acc_ref[...] = jnp.zeros(acc_ref.shape, F32)

    tk = k_ref.shape[1]
    k_half = tk // 2
    v_half = (tk // MLA_STRIP + 1) // 2 * MLA_STRIP if tk > MLA_STRIP else tk

    def scores(hh, c, slot):
        q = q_ref[hh, c * MLA_STRIP:(c + 1) * MLA_STRIP, :]
        for r0, r1 in ((0, k_half), (k_half, tk)):
            s_ref[slot, r0:r1, :] = lax.dot_general(k_ref[hh, r0:r1, :], q, NT_DIMS, preferred_element_type=F32)

    def softmax_pv(hh, c, slot):
        cs = slice(c * MLA_STRIP, (c + 1) * MLA_STRIP)
        st = s_ref[slot]
        m_old = m_ref[hh, :, cs]
        m_new = jnp.maximum(m_old, _col_max(st))
        alpha = jnp.exp2(m_old - m_new)
        p = jnp.exp2(st - m_new).astype(BF16)
        pv = jnp.dot(vt_ref[hh, :, 0:v_half], p[0:v_half], preferred_element_type=F32)
        if v_half < tk:
            pv = pv + jnp.dot(vt_ref[hh, :, v_half:tk], p[v_half:tk], preferred_element_type=F32)
        acc_ref[hh, :, cs] = alpha * acc_ref[hh, :, cs] + pv
        m_ref[hh, :, cs] = m_new

    pipelined = ns % 2 == 0

    def head(hh, carry):
        for c in range(ns):
            if not pipelined:
                scores(hh, c, 0)
            elif c + 1 < ns:
                scores(hh, c + 1, (c + 1) % 2)
            else:
                scores(jnp.minimum(hh + 1, B_HEADS - 1), 0, 0)
            softmax_pv(hh, c, c % 2 if pipelined else 0)
        return carry

    if pipelined:
        scores(0, 0, 0)
    lax.fori_loop(0, B_HEADS, head, 0, unroll=max(1, min(B_HEADS, MLA_UNITS_PER_TRIP // ns)))

    @pl.when(j == nk - 1)
    def _finish():
        for hh in range(B_HEADS):
            o = acc_ref[hh, 0:B_V_DIM, :] * (1.0 / acc_ref[hh, B_V_DIM:B_V_DIM + 1, :])
            o_ref[:, hh * B_V_DIM:(hh + 1) * B_V_DIM] = o.T.astype(o_ref.dtype)


def _mla(qb, kb, vbt, prev, *, q_row0, n_q, k_row0, n_k, tq, tk):
    t_all = qb.shape[1]
    nq, nk = n_q // tq, n_k // tk
    qo, ko = q_row0 // tq, k_row0 // tk
    kern = functools.partial(_mla_kernel, nk=nk, tq=tq)
    return pl.pallas_call(
        kern,
        grid=(nq, nk),
        in_specs=[pl.BlockSpec((B_HEADS, tq, B_QK_PAD), lambda i, j: (0, qo + i, 0)),
                  pl.BlockSpec((B_HEADS, tk, B_QK_PAD), lambda i, j: (0, ko + j, 0)),
                  pl.BlockSpec((B_HEADS, B_VT_ROWS, tk), lambda i, j: (0, 0, ko + j)),
                  pl.BlockSpec(memory_space=pl.ANY)],
        out_specs=pl.BlockSpec((tq, B_WIDTH), lambda i, j: (qo + i, 0)),
        out_shape=jax.ShapeDtypeStruct((t_all, B_WIDTH), BF16),
        scratch_shapes=[pltpu.VMEM((B_HEADS, B_VT_ROWS, tq), F32), pltpu.VMEM((B_HEADS, 1, tq), F32),
                        pltpu.VMEM((2, tk, MLA_STRIP), F32)],
        input_output_aliases={3: 0},
        compiler_params=_params(("arbitrary", "arbitrary")),
        name="mla_attn",
    )(qb, kb, vbt, prev)


def _out_kernel(oa_ref, ob_ref, x_ref, mod_ref, ga_ref, gb_ref, wo_ref, pg_ref, o_ref, *, tm, n_lat):
    d = x_ref.shape[-1]
    is_ctx = _is_ctx_rows(pl.program_id(0), tm, n_lat)
    a = (_rms(oa_ref[...].astype(F32)) * ga_ref[...]).astype(BF16)
    b = (_rms(ob_ref[...].astype(F32)) * gb_ref[...]).astype(BF16)
    y = (jnp.dot(a, wo_ref[0:A_WIDTH, :], preferred_element_type=F32)
         + jnp.dot(b, wo_ref[A_WIDTH:A_WIDTH + B_WIDTH, :], preferred_element_type=F32))
    o_ref[...] = x_ref[...] + _mod_select(mod_ref, 2, d, is_ctx) * (_rms(y) * pg_ref[...])


def _attn_out(oa, ob, xs, mod, ga, gb, wo, pg, *, n_lat, tm):
    t_all, d = xs.shape
    row = lambda i: (i, 0)
    kern = functools.partial(_out_kernel, tm=tm, n_lat=n_lat)
    return pl.pallas_call(
        kern,
        grid=(t_all // tm,),
        in_specs=[pl.BlockSpec((tm, A_WIDTH), row), pl.BlockSpec((tm, B_WIDTH), row), pl.BlockSpec((tm, d), row),
                  _const_spec(mod.shape), _const_spec(ga.shape), _const_spec(gb.shape), _const_spec(wo.shape),
                  _const_spec(pg.shape)],
        out_specs=pl.BlockSpec((tm, d), row),
        out_shape=jax.ShapeDtypeStruct((t_all, d), F32),
        compiler_params=_params(("arbitrary",)),
        name="attn_out_mix",
    )(oa, ob, xs, mod, ga, gb, wo, pg)


def _swiglu_step(x_ref, mod_ref, g_ref, w1_ref, w3_ref, w2_ref, o_ref, h_ref, gs_ref, tile_row0, n_lat,
                 alongside=None):
    j = pl.program_id(1)

    @pl.when(j == 0)
    def _prologue():
        _norm_mod_rows(x_ref, g_ref, mod_ref, 3, tile_row0, n_lat, h_ref, gs_ref)
        o_ref[...] = jnp.zeros(o_ref.shape, o_ref.dtype)

    if alongside is not None:
        alongside()
    h = h_ref[...]
    a = jnp.dot(h, w1_ref[...].astype(BF16), preferred_element_type=F32)
    b = jnp.dot(h, w3_ref[...].astype(BF16), preferred_element_type=F32)
    u = (a * (1.0 / (1.0 + jnp.exp(-a))) * b).astype(BF16)
    o_ref[...] += jnp.dot(u, w2_ref[...].astype(BF16), preferred_element_type=F32)


def _ffn_dense_kernel(x_ref, mod_ref, g_ref, pg_ref, w1_ref, w3_ref, w2_ref, o_ref, h_ref, gs_ref, acc_ref, *, tm,
                      n_lat, nj):
    tile_row0 = pl.program_id(0) * tm
    _swiglu_step(x_ref, mod_ref, g_ref, w1_ref, w3_ref, w2_ref, acc_ref, h_ref, gs_ref, tile_row0, n_lat)

    @pl.when(pl.program_id(1) == nj - 1)
    def _epilogue():
        _resid_rows(o_ref, acc_ref, x_ref, pg_ref, mod_ref, 5, tile_row0, n_lat, gs_ref)


def _ffn_dense(xs, mod, g, pg, w1, w3, w2, *, n_lat, tm, tf):
    t_all, d = xs.shape
    dff = w1.shape[1]
    nj = dff // tf
    kern = functools.partial(_ffn_dense_kernel, tm=tm, n_lat=n_lat, nj=nj)
    return pl.pallas_call(
        kern,
        grid=(t_all // tm, nj),
        in_specs=[pl.BlockSpec((tm, d), lambda i, j: (i, 0)), _const_spec(mod.shape), _const_spec(g.shape),
                  _const_spec(pg.shape),
                  pl.BlockSpec((d, tf), lambda i, j: (0, j)), pl.BlockSpec((d, tf), lambda i, j: (0, j)),
                  pl.BlockSpec((tf, d), lambda i, j: (j, 0))],
        out_specs=pl.BlockSpec((tm, d), lambda i, j: (i, 0)),
        out_shape=jax.ShapeDtypeStruct((t_all, d), F32),
        scratch_shapes=[pltpu.VMEM((tm, d), BF16), pltpu.VMEM((8, d), F32), pltpu.VMEM((tm, d), F32)],
        compiler_params=_params(("arbitrary", "arbitrary")),
        name="ffn_dense",
    )(xs, mod, g, pg, w1, w3, w2)


def _ffn_moe_kernel(te_ref, nu_ref, cur_ref, nxt_ref, x_hbm, mod_ref, g_ref, w1_ref, w3_ref, w2_ref, o_ref, h_ref,
                    gs_ref, xbuf, sem, *, tm, nj):
    t, j = pl.program_id(0), pl.program_id(1)
    n_used = nu_ref[0]
    used = t < n_used
    slot = t % 2
    rows_per_step = tm // nj

    def row_copy(src_ref, r, s):
        return pltpu.make_async_copy(x_hbm.at[pl.ds(src_ref[0, 0, r], 1)], xbuf.at[s, pl.ds(r, 1)], sem.at[s])

    @pl.when((t == 0) & (j == 0))
    def _first_tile():
        lax.fori_loop(0, tm, lambda r, c: (row_copy(cur_ref, r, 0).start(), c)[1], 0, unroll=DMA_UNROLL)

    @pl.when((t <= n_used) & (j == 0))
    def _rows_arrived():
        lax.fori_loop(0, tm, lambda r, c: (row_copy(cur_ref, r, slot).wait(), c)[1], 0, unroll=DMA_UNROLL)

    @pl.when(used)
    def _used():
        def request_next_tile():
            for k in range(rows_per_step):
                row_copy(nxt_ref, j * rows_per_step + k, 1 - slot).start()

        _swiglu_step(xbuf.at[slot], mod_ref, g_ref, w1_ref, w3_ref, w2_ref, o_ref, h_ref, gs_ref, 0, tm,
                     alongside=request_next_tile)

    @pl.when(jnp.logical_not(used) & (j == 0))
    def _idle():
        o_ref[...] = jnp.zeros(o_ref.shape, o_ref.dtype)


def _ffn_moe(xs, slot_src, tile_expert, n_used, mod, g, w1, w3, w2, *, tm, tf):
    d = xs.shape[1]
    n_tiles = slot_src.shape[0]
    dff = w1.shape[2]
    nj = dff // tf
    assert tm % nj == 0, "the next tile's row requests are spread evenly over the d_ff chunks"

    def t_eff(t, nu):
        return jnp.minimum(t, nu[0] - 1)

    def j_eff(t, j, nu):
        return jnp.where(t < nu[0], j, nj - 1)

    smem_rows = lambda fn: pl.BlockSpec((1, 1, tm), fn, memory_space=pltpu.SMEM)
    grid_spec = pltpu.PrefetchScalarGridSpec(
        num_scalar_prefetch=2,
        grid=(n_tiles, nj),
        in_specs=[smem_rows(lambda t, j, te, nu: (t, 0, 0)),
                  smem_rows(lambda t, j, te, nu: (jnp.minimum(t + 1, n_tiles - 1), 0, 0)),
                  pl.BlockSpec(memory_space=pl.ANY),
                  pl.BlockSpec(mod.shape, lambda t, j, te, nu: (0, 0), pipeline_mode=pl.Buffered(1)),
                  pl.BlockSpec(g.shape, lambda t, j, te, nu: (0, 0), pipeline_mode=pl.Buffered(1)),
                  pl.BlockSpec((None, d, tf), lambda t, j, te, nu: (te[t_eff(t, nu)], 0, j_eff(t, j, nu))),
                  pl.BlockSpec((None, d, tf), lambda t, j, te, nu: (te[t_eff(t, nu)], 0, j_eff(t, j, nu))),
                  pl.BlockSpec((None, tf, d), lambda t, j, te, nu: (te[t_eff(t, nu)], j_eff(t, j, nu), 0))],
        out_specs=pl.BlockSpec((tm, d), lambda t, j, te, nu: (t, 0)),
        scratch_shapes=[pltpu.VMEM((tm, d), BF16), pltpu.VMEM((8, d), F32), pltpu.VMEM((2, tm, d), F32),
                        pltpu.SemaphoreType.DMA((2,))],
    )
    return pl.pallas_call(
        functools.partial(_ffn_moe_kernel, tm=tm, nj=nj),
        grid_spec=grid_spec,
        out_shape=jax.ShapeDtypeStruct((n_tiles * tm, d), F32),
        compiler_params=_params(("arbitrary", "arbitrary")),
        name="ffn_moe",
    )(tile_expert, n_used, slot_src, slot_src, xs, mod, g, w1, w3, w2)


def _bf16_part(x):
    return lax.bitcast_convert_type(lax.bitcast_convert_type(x, jnp.uint32) & jnp.uint32(0xFFFF0000), F32)


def _route_kernel(x_ref, mod_ref, g_ref, rw_ref, o_ref):
    d = x_ref.shape[-1]
    h = _rms(x_ref[...]) * g_ref[...] * (1.0 + mod_ref[0:1, 4 * d:5 * d]) + mod_ref[0:1, 3 * d:4 * d]
    h_top = _bf16_part(h)
    h_hi, h_lo = h_top.astype(BF16), (h - h_top).astype(BF16)
    hi = jnp.dot(h_hi, rw_ref[0], preferred_element_type=F32)
    logits = hi + pltpu.roll(hi, LANE - N_EXPERTS, 1) + jnp.dot(h_lo, rw_ref[1], preferred_element_type=F32)
    lane = lax.broadcasted_iota(jnp.int32, logits.shape, 1)
    lg = jnp.where(lane < N_EXPERTS, logits, -jnp.inf)
    m1 = jnp.max(lg, axis=-1, keepdims=True)
    i1 = jnp.min(jnp.where(lg == m1, lane, LANE), axis=-1, keepdims=True)
    lg2 = jnp.where(lane == i1, -jnp.inf, lg)
    m2 = jnp.max(lg2, axis=-1, keepdims=True)
    i2 = jnp.min(jnp.where(lg2 == m2, lane, LANE), axis=-1, keepdims=True)
    e = jnp.exp(m2 - m1)
    w1 = 1.0 / (1.0 + e)
    w2 = e * w1
    o_ref[...] = jnp.where(lane == 0, i1.astype(F32),
                           jnp.where(lane == 1, i2.astype(F32), jnp.where(lane == 2, w1, jnp.where(lane == 3, w2, 0.0))))


def _route(xs, mod, g, rw_pad, *, n_lat, tm):
    d = xs.shape[1]
    return pl.pallas_call(
        _route_kernel,
        grid=(n_lat // tm,),
        in_specs=[pl.BlockSpec((tm, d), lambda i: (i, 0)), _const_spec(mod.shape), _const_spec(g.shape),
                  _const_spec(rw_pad.shape)],
        out_specs=pl.BlockSpec((tm, LANE), lambda i: (i, 0)),
        out_shape=jax.ShapeDtypeStruct((n_lat, LANE), F32),
        compiler_params=_params(("arbitrary",)),
        name="moe_route",
    )(xs, mod, g, rw_pad)


def _combine_kernel(cur_ref, nxt_ref, rt_ref, x_ref, mod_ref, pg_ref, y_hbm, o_ref, buf, sem, *, tm, n_steps):
    i = pl.program_id(0)
    slot = i % 2

    def row_copy(dest_ref, r, k, s):
        return pltpu.make_async_copy(y_hbm.at[pl.ds(dest_ref[0, k, r], 1)], buf.at[s, k, pl.ds(r, 1)], sem.at[s])

    def start_tile0(r, c):
        row_copy(cur_ref, r, 0, 0).start()
        row_copy(cur_ref, r, 1, 0).start()
        return c

    def wait(r, c):
        row_copy(cur_ref, r, 0, slot).wait()
        row_copy(cur_ref, r, 1, slot).wait()
        return c

    @pl.when(i == 0)
    def _first_tile():
        lax.fori_loop(0, tm, start_tile0, 0, unroll=DMA_UNROLL)

    lax.fori_loop(0, tm, wait, 0, unroll=DMA_UNROLL)

    @pl.when(i < n_steps)
    def _tile():
        for r in range(tm):
            row_copy(nxt_ref, r, 0, 1 - slot).start()
            row_copy(nxt_ref, r, 1, 1 - slot).start()
        d = x_ref.shape[-1]
        rt = rt_ref[...]
        y = rt[:, 2:3] * buf[slot, 0] + rt[:, 3:4] * buf[slot, 1]
        o_ref[...] = x_ref[...] + mod_ref[0:1, 5 * d:6 * d] * (_rms(y) * pg_ref[...])


def _combine(yg, dest, route, xs, mod, pg, *, n_lat, tm):
    d = xs.shape[1]
    n_steps = n_lat // tm
    kern = functools.partial(_combine_kernel, tm=tm, n_steps=n_steps)
    tile = lambda i: jnp.minimum(i, n_steps - 1)
    nxt_tile = lambda i: jnp.minimum(i + 1, n_steps - 1)
    return pl.pallas_call(
        kern,
        grid=(n_steps + 1,),
        in_specs=[pl.BlockSpec((1, 2, tm), lambda i: (tile(i), 0, 0), memory_space=pltpu.SMEM),
                  pl.BlockSpec((1, 2, tm), lambda i: (nxt_tile(i), 0, 0), memory_space=pltpu.SMEM),
                  pl.BlockSpec((tm, LANE), lambda i: (tile(i), 0)), pl.BlockSpec((tm, d), lambda i: (tile(i), 0)),
                  _const_spec(mod.shape), _const_spec(pg.shape), pl.BlockSpec(memory_space=pl.ANY)],
        out_specs=pl.BlockSpec((tm, d), lambda i: (tile(i), 0)),
        out_shape=jax.ShapeDtypeStruct((n_lat, d), F32),
        scratch_shapes=[pltpu.VMEM((2, 2, tm, d), F32), pltpu.SemaphoreType.DMA((2,))],
        compiler_params=_params(("arbitrary",)),
        name="moe_combine",
    )(dest, dest, route, xs, mod, pg, yg)


def _rope_tables(n_lat, n_ctx):
    rows = n_lat // GRID_W

    def tabs(rot_dim):
        axis_dim = rot_dim // 2
        quarter = rot_dim // 4
        inv_freq = ROPE_BASE ** (-jnp.arange(0, axis_dim, 2, dtype=F32) / axis_dim)
        ang_r = jnp.arange(rows, dtype=F32)[:, None] * inv_freq[None, :]
        ang_c = jnp.arange(GRID_W, dtype=F32)[:, None] * inv_freq[None, :]

        def over_grid(fn):
            r = jnp.broadcast_to(fn(ang_r)[:, None, :], (rows, GRID_W, quarter))
            c = jnp.broadcast_to(fn(ang_c)[None, :, :], (rows, GRID_W, quarter))
            return jnp.concatenate([r, r, c, c], axis=-1).reshape(n_lat, rot_dim)

        cos, sin = over_grid(jnp.cos), over_grid(jnp.sin)
        lower = (jnp.arange(rot_dim) % (2 * quarter)) < quarter
        sin_m = jnp.where(lower[None, :], -sin, 0.0)
        sin_p = jnp.where(lower[None, :], 0.0, sin)
        pad = LANE - rot_dim
        cos = jnp.pad(cos, ((0, n_ctx), (0, pad)), constant_values=1.0)
        sin_m = jnp.pad(sin_m, ((0, n_ctx), (0, pad)))
        sin_p = jnp.pad(sin_p, ((0, n_ctx), (0, pad)))
        return [cos, sin_m, sin_p]

    return jnp.stack(tabs(A_HEAD_DIM) + tabs(B_ROPE_DIM))


def _layer_weights(i, w_in, w_uq, w_ukv, w_o):
    d = w_in.shape[1]
    q_w = A_WIDTH + B_Q_RANK
    wi = w_in[i]
    qa, cq = wi[:, :A_WIDTH], wi[:, A_WIDTH:q_w]
    o1, o2, o3 = q_w + A_KV_WIDTH, q_w + 2 * A_KV_WIDTH, q_w + 2 * A_KV_WIDTH + B_KV_RANK
    ka, va, ckv, kr = wi[:, q_w:o1], wi[:, o1:o2], wi[:, o2:o3], wi[:, o3:]
    win = jnp.concatenate([qa, ka, va, cq, ckv, kr, jnp.zeros((d, LANE - B_ROPE_DIM), F32)], axis=1).astype(BF16)
    uq = w_uq[i].reshape(B_Q_RANK, B_HEADS, B_NOPE_DIM + B_ROPE_DIM)
    uq = jnp.pad(uq, ((0, 0), (0, 0), (0, B_QK_PAD - B_NOPE_DIM - B_ROPE_DIM)))
    wuq = uq.reshape(B_Q_RANK, B_HEADS * B_QK_PAD).astype(BF16)
    ukv = w_ukv[i].reshape(B_KV_RANK, B_HEADS, B_NOPE_DIM + B_V_DIM)
    wukn = ukv[:, :, :B_NOPE_DIM].reshape(B_KV_RANK, B_HEADS * B_NOPE_DIM).astype(BF16)
    wuvt = ukv[:, :, B_NOPE_DIM:].reshape(B_KV_RANK, B_HEADS * B_V_DIM).T.astype(BF16)
    return win, wuq, wukn, wuvt, w_o[i].astype(BF16)


def _routing_slots(route, n_lat, tm_e, n_tiles):
    e12 = route[:, 0:2].astype(jnp.int32).T.reshape(-1)
    onehot = (e12[:, None] == jnp.arange(N_EXPERTS)[None, :]).astype(jnp.int32)
    csum = jnp.cumsum(onehot, axis=0)
    rank = jnp.sum(csum * onehot, axis=1) - 1
    counts = csum[-1]
    tiles_per = (counts + tm_e - 1) // tm_e
    tile_end = jnp.cumsum(tiles_per)
    start = (tile_end - tiles_per) * tm_e
    dest = (jnp.sum(start[None, :] * onehot, axis=1) + rank).astype(jnp.int32)
    n_used = tile_end[-1].astype(jnp.int32).reshape(1)
    t_idx = jnp.arange(n_tiles, dtype=jnp.int32)
    tile_expert = jnp.minimum(jnp.sum((t_idx[:, None] >= tile_end[None, :]).astype(jnp.int32), axis=1),
                              N_EXPERTS - 1).astype(jnp.int32)
    token = jnp.tile(jnp.arange(n_lat, dtype=jnp.int32), 2)
    slot_src = jnp.zeros((n_tiles * tm_e,), jnp.int32).at[dest].set(token, unique_indices=True)
    return dest.reshape(2, n_lat), slot_src.reshape(n_tiles, 1, tm_e), tile_expert, n_used


def _tile(n, candidates):
    for c in candidates:
        if n % c == 0:
            return c
    raise ValueError(f"no supported tile size for {n} rows")


def kernel(x, c, ctx, c_ctx, w_ada, b_ada, pre_attn_g, post_attn_g, pre_ffn_g, post_ffn_g, w_in, attn_sink, q_norm_g,
           kv_norm_g, w_uq, w_ukv, grp_a_g, grp_b_g, w_o, ffn_w1, ffn_w3, ffn_w2, router_w, moe_w1, moe_w3, moe_w2):
    b, n_lat, d = x.shape
    n_ctx = ctx.shape[1]
    depth = w_ada.shape[0]
    assert b == 1 and c.shape[0] == 1, "single-sequence kernel"
    t_all = n_lat + n_ctx
    tm = _tile(t_all, (640, 256, 128))
    tq_a = 256
    tq_b = _tile(n_lat, (1024, 512, 256))
    tk_b = _tile(t_all, (1280, 256))
    tf = 512
    tm_r = _tile(n_lat, (512, 256))
    tm_c = 256
    assert n_ctx == tq_a and n_lat % tq_a == 0 and n_lat >= tq_a + 2 * WINDOW

    xs = jnp.concatenate([x[0], ctx[0]], axis=0)
    cc = jnp.stack([c[0], c_ctx])
    mods = _ada(cc, w_ada, b_ada)
    rope_tab = _rope_tables(n_lat, n_ctx)
    row2 = lambda v: v.reshape(1, -1)

    for i in range(depth):
        last = i == depth - 1
        mod = mods[i]
        win, wuq, wukn, wuvt, wo = _layer_weights(i, w_in, w_uq, w_ukv, w_o)
        qa, ka, va, qb, kb, vbt = _proj(xs, mod, row2(pre_attn_g[i]), win, row2(q_norm_g[i]), row2(kv_norm_g[i]),
                                        wuq, wukn, wuvt, rope_tab, n_lat=n_lat, tm=tm)
        sink_tab = jnp.broadcast_to((attn_sink[i] * LOG2E)[:, None], (A_HEADS, tq_a))
        oa = _attn_a(qa, ka, va, sink_tab, n_lat=n_lat, n_ctx=n_ctx, tq=tq_a)
        ob = _mla(qb, kb, vbt, jnp.zeros((t_all, B_WIDTH), BF16), q_row0=0, n_q=n_lat, k_row0=0, n_k=t_all,
                  tq=tq_b, tk=tk_b)
        ob = _mla(qb, kb, vbt, ob, q_row0=n_lat, n_q=n_ctx, k_row0=n_lat, n_k=n_ctx, tq=n_ctx, tk=n_ctx)
        xs = _attn_out(oa, ob, xs, mod, row2(grp_a_g[i]), row2(grp_b_g[i]), wo, row2(post_attn_g[i]),
                       n_lat=n_lat, tm=tm)
        jj = i // 2
        if i % 2 == 0:
            xs = _ffn_dense(xs, mod, row2(pre_ffn_g[i]), row2(post_ffn_g[i]), ffn_w1[jj].astype(BF16),
                            ffn_w3[jj].astype(BF16), ffn_w2[jj].astype(BF16), n_lat=n_lat, tm=tm, tf=tf)
        else:
            n_rows = n_lat
            assert last, "expert layers other than the last would also need the context rows routed"
            nj = moe_w1.shape[-1] // tf
            tm_e = -(-tm // (nj * ROW_CHUNK)) * ROW_CHUNK * nj
            n_tiles = (2 * n_rows + N_EXPERTS * (tm_e - 1)) // tm_e + 1
            rw_top = _bf16_part(router_w[jj])
            rw_hi, rw_lo = rw_top.astype(BF16), (router_w[jj] - rw_top).astype(BF16)
            zeros = jnp.zeros((d, LANE - 2 * N_EXPERTS), BF16)
            rw_pad = jnp.stack([jnp.concatenate([rw_hi, rw_lo, zeros], axis=1),
                                jnp.concatenate([rw_hi, jnp.zeros_like(rw_lo), zeros], axis=1)])
            route = _route(xs, mod, row2(pre_ffn_g[i]), rw_pad, n_lat=n_rows, tm=tm_r)
            dest, slot_src, tile_expert, n_used = _routing_slots(route, n_rows, tm_e, n_tiles)
            dest_t = dest.reshape(2, n_rows // tm_c, tm_c).transpose(1, 0, 2)
            yg = _ffn_moe(xs, slot_src, tile_expert, n_used, mod, row2(pre_ffn_g[i]), moe_w1[jj], moe_w3[jj],
                          moe_w2[jj], tm=tm_e, tf=tf)
            xs = _combine(yg, dest_t, route, xs, mod, row2(post_ffn_g[i]), n_lat=n_rows, tm=tm_c)
    return xs[:n_lat].reshape(b, n_lat, d)
```

```python
import functools
import math

import jax
import jax.numpy as jnp
from jax import lax
from jax.experimental import pallas as pl
from jax.experimental.pallas import tpu as pltpu

GRID_W = 64
EPS = 1e-6
NEG_INF = -1e30
ROPE_BASE = 10000.0
A_HEADS = 8
A_KV_HEADS = 2
A_HEAD_DIM = 128
WINDOW = 128
B_HEADS = 8
B_Q_RANK = 384
B_KV_RANK = 256
B_NOPE_DIM = 128
B_ROPE_DIM = 64
B_V_DIM = 128
N_EXPERTS = 8
A_WIDTH = A_HEADS * A_HEAD_DIM
A_KV_WIDTH = A_KV_HEADS * A_HEAD_DIM
B_WIDTH = B_HEADS * B_V_DIM
B_QK_PAD = 256
B_VT_ROWS = B_V_DIM + 16

LOG2E = math.log2(math.e)
LANE = 128
VMEM_LIMIT_BYTES = 60 * 1024 * 1024

F32 = jnp.float32
BF16 = jnp.bfloat16
NT_DIMS = (((1,), (1,)), ((), ()))


def _params(sem, vmem=VMEM_LIMIT_BYTES):
    return pltpu.CompilerParams(dimension_semantics=sem, vmem_limit_bytes=vmem)


def _const_spec(shape):
    nd = len(shape)
    return pl.BlockSpec(shape, lambda *_: (0,) * nd, pipeline_mode=pl.Buffered(1))


def _rms(x):
    return x * lax.rsqrt(jnp.mean(x * x, axis=-1, keepdims=True) + EPS)


DMA_UNROLL = 8
ROW_CHUNK = 16
ROW_UNROLL = 8


def _row_chunks(tm, body):
    def step(r, carry):
        body(pl.ds(pl.multiple_of(r * ROW_CHUNK, ROW_CHUNK), ROW_CHUNK), r * ROW_CHUNK)
        return carry

    lax.fori_loop(0, tm // ROW_CHUNK, step, 0, unroll=ROW_UNROLL)


def _mod_row(tile_row0, r0, n_lat):
    return jnp.where(tile_row0 + r0 >= n_lat, 1, 0)


def _norm_mod_rows(x_ref, g_ref, mod_ref, k_shift, tile_row0, n_lat, h_ref, gs_ref):
    tm, d = x_ref.shape
    gs_ref[...] = g_ref[...] * (1.0 + mod_ref[:, (k_shift + 1) * d:(k_shift + 2) * d])

    def body(rows, r0):
        mrow = _mod_row(tile_row0, r0, n_lat)
        h = _rms(x_ref[rows, :]) * gs_ref[pl.ds(mrow, 1), :]
        h_ref[rows, :] = (h + mod_ref[pl.ds(mrow, 1), k_shift * d:(k_shift + 1) * d]).astype(BF16)

    _row_chunks(tm, body)


def _resid_rows(o_ref, y_ref, x_ref, pg_ref, mod_ref, k_gate, tile_row0, n_lat, gs_ref):
    tm, d = x_ref.shape
    gs_ref[...] = pg_ref[...] * mod_ref[:, k_gate * d:(k_gate + 1) * d]

    def body(rows, r0):
        mrow = _mod_row(tile_row0, r0, n_lat)
        o_ref[rows, :] = x_ref[rows, :] + gs_ref[pl.ds(mrow, 1), :] * _rms(y_ref[rows, :])

    _row_chunks(tm, body)


def _is_ctx_rows(tile_index, tm, n_lat):
    row = tile_index * tm + lax.broadcasted_iota(jnp.int32, (tm, 1), 0)
    return row >= n_lat


def _mod_select(mod_ref, k, d, is_ctx):
    return jnp.where(is_ctx, mod_ref[1:2, k * d:(k + 1) * d], mod_ref[0:1, k * d:(k + 1) * d])


ADA_ROWS = 2


def _ada_kernel(c_ref, w_ref, b_ref, o_ref):
    tn = w_ref.shape[-1]
    o_ref[...] = jnp.zeros(o_ref.shape, o_ref.dtype)
    for r in range(ADA_ROWS):
        c = c_ref[r]
        a = c * (1.0 / (1.0 + jnp.exp(-c)))
        for l in range(tn // LANE):
            cols = slice(l * LANE, (l + 1) * LANE)
            o_ref[r:r + 1, cols] = _col_sum(a * w_ref[:, cols]) + b_ref[:, cols]


def _ada(cc, w_ada, b_ada):
    depth, d, n = w_ada.shape
    tn = n // 8
    c_lanes = jnp.broadcast_to(cc[:, :, None], (ADA_ROWS, d, LANE))
    return pl.pallas_call(
        _ada_kernel,
        grid=(depth, n // tn),
        in_specs=[pl.BlockSpec((ADA_ROWS, d, LANE), lambda l, j: (0, 0, 0)),
                  pl.BlockSpec((None, d, tn), lambda l, j: (l, 0, j)),
                  pl.BlockSpec((None, 1, tn), lambda l, j: (l, 0, j))],
        out_specs=pl.BlockSpec((None, 8, tn), lambda l, j: (l, 0, j)),
        out_shape=jax.ShapeDtypeStruct((depth, 8, n), F32),
        compiler_params=_params(("arbitrary", "arbitrary")),
        name="ada_mod",
    )(c_lanes, w_ada, b_ada.reshape(depth, 1, n))


def _rope(t, cos, sin_m, sin_p, quarter):
    n = t.shape[-1]
    return t * cos + pltpu.roll(t, n - quarter, 1) * sin_m + pltpu.roll(t, quarter, 1) * sin_p


def _proj_kernel(x_ref, mod_ref, g_ref, win_ref, qg_ref, kvg_ref, wuq_ref, wukn_ref, wuvt_ref, rope_ref,
                 qa_ref, ka_ref, va_ref, qb_ref, kb_ref, vbt_ref, *, tm, n_lat, scale_a, scale_b):
    d = x_ref.shape[-1]
    is_ctx = _is_ctx_rows(pl.program_id(0), tm, n_lat)
    h = _rms(x_ref[...]) * g_ref[...] * (1.0 + _mod_select(mod_ref, 1, d, is_ctx)) + _mod_select(mod_ref, 0, d, is_ctx)
    proj = jnp.dot(h.astype(BF16), win_ref[...], preferred_element_type=F32)
    cos_a, sinm_a, sinp_a = rope_ref[0], rope_ref[1], rope_ref[2]
    cos_b, sinm_b, sinp_b = rope_ref[3], rope_ref[4], rope_ref[5]
    qa_q = A_HEAD_DIM // 4
    b_q = B_ROPE_DIM // 4
    cos_as, sinm_as, sinp_as = cos_a * scale_a, sinm_a * scale_a, sinp_a * scale_a
    for hh in range(A_HEADS):
        t = proj[:, hh * LANE:(hh + 1) * LANE]
        qa_ref[:, hh * LANE:(hh + 1) * LANE] = _rope(t, cos_as, sinm_as, sinp_as, qa_q).astype(BF16)
    o = A_WIDTH
    for hh in range(A_KV_HEADS):
        t = proj[:, o + hh * LANE:o + (hh + 1) * LANE]
        ka_ref[:, hh * LANE:(hh + 1) * LANE] = _rope(t, cos_a, sinm_a, sinp_a, qa_q).astype(BF16)
    o += A_KV_WIDTH
    va_ref[...] = proj[:, o:o + A_KV_WIDTH].astype(BF16)
    o += A_KV_WIDTH
    cq = (_rms(proj[:, o:o + B_Q_RANK]) * qg_ref[...]).astype(BF16)
    o += B_Q_RANK
    qm = jnp.dot(cq, wuq_ref[...], preferred_element_type=F32)
    cos_bs, sinm_bs, sinp_bs = cos_b * scale_b, sinm_b * scale_b, sinp_b * scale_b
    for hh in range(B_HEADS):
        b0 = hh * B_QK_PAD
        qb_ref[hh, :, 0:LANE] = (qm[:, b0:b0 + LANE] * scale_b).astype(BF16)
        qb_ref[hh, :, LANE:2 * LANE] = _rope(qm[:, b0 + LANE:b0 + 2 * LANE], cos_bs, sinm_bs, sinp_bs, b_q).astype(BF16)
    ckv = (_rms(proj[:, o:o + B_KV_RANK]) * kvg_ref[...]).astype(BF16)
    o += B_KV_RANK
    kr = _rope(proj[:, o:o + LANE], cos_b, sinm_b, sinp_b, b_q).astype(BF16)
    kn = jnp.dot(ckv, wukn_ref[...], preferred_element_type=F32)
    vt = lax.dot_general(wuvt_ref[...], ckv, NT_DIMS, preferred_element_type=F32)
    for hh in range(B_HEADS):
        kb_ref[hh, :, 0:LANE] = kn[:, hh * LANE:(hh + 1) * LANE].astype(BF16)
        kb_ref[hh, :, LANE:2 * LANE] = kr
        vbt_ref[hh, 0:B_V_DIM, :] = vt[hh * B_V_DIM:(hh + 1) * B_V_DIM, :].astype(BF16)
        vbt_ref[hh, B_V_DIM:B_VT_ROWS, :] = jnp.ones((B_VT_ROWS - B_V_DIM, tm), BF16)


def _proj(xs, mod, g, win, qg, kvg, wuq, wukn, wuvt, rope_tab, *, n_lat, tm):
    t_all, d = xs.shape
    row = lambda i: (i, 0)
    kern = functools.partial(_proj_kernel, tm=tm, n_lat=n_lat, scale_a=LOG2E / math.sqrt(A_HEAD_DIM),
                             scale_b=LOG2E / math.sqrt(B_NOPE_DIM + B_ROPE_DIM))
    return pl.pallas_call(
        kern,
        grid=(t_all // tm,),
        in_specs=[pl.BlockSpec((tm, d), row), _const_spec(mod.shape), _const_spec(g.shape), _const_spec(win.shape),
                  _const_spec(qg.shape), _const_spec(kvg.shape), _const_spec(wuq.shape), _const_spec(wukn.shape),
                  _const_spec(wuvt.shape), pl.BlockSpec((6, tm, LANE), lambda i: (0, i, 0))],
        out_specs=[pl.BlockSpec((tm, A_WIDTH), row), pl.BlockSpec((tm, A_KV_WIDTH), row),
                   pl.BlockSpec((tm, A_KV_WIDTH), row),
                   pl.BlockSpec((B_HEADS, tm, B_QK_PAD), lambda i: (0, i, 0)),
                   pl.BlockSpec((B_HEADS, tm, B_QK_PAD), lambda i: (0, i, 0)),
                   pl.BlockSpec((B_HEADS, B_VT_ROWS, tm), lambda i: (0, 0, i))],
        out_shape=[jax.ShapeDtypeStruct((t_all, A_WIDTH), BF16), jax.ShapeDtypeStruct((t_all, A_KV_WIDTH), BF16),
                   jax.ShapeDtypeStruct((t_all, A_KV_WIDTH), BF16),
                   jax.ShapeDtypeStruct((B_HEADS, t_all, B_QK_PAD), BF16),
                   jax.ShapeDtypeStruct((B_HEADS, t_all, B_QK_PAD), BF16),
                   jax.ShapeDtypeStruct((B_HEADS, B_VT_ROWS, t_all), BF16)],
        compiler_params=_params(("arbitrary",)),
        name="pre_attn_proj",
    )(xs, mod, g, win, qg, kvg, wuq, wukn, wuvt, rope_tab)


def _attn_a_kernel(q_ref, k_ref, v_ref, sink_ref, o_ref, s_ref, *, tq, n_lat, n_ctx):
    i = pl.program_id(0)
    n_lat_tiles = n_lat // tq
    win = tq + 2 * WINDOW
    grp = A_HEADS // A_KV_HEADS
    tn_dims = (((0,), (0,)), ((), ()))

    def run_tile(keys, vals, mask):
        n_keys = keys[0].shape[0]

        def scores(hh, slot):
            q = q_ref[:, hh * A_HEAD_DIM:(hh + 1) * A_HEAD_DIM]
            s_ref[slot, 0:n_keys, :] = lax.dot_general(keys[hh // grp], q, NT_DIMS, preferred_element_type=F32)

        def head_out(hh, slot):
            s = s_ref[slot, 0:n_keys, :]
            if mask is not None:
                s = jnp.concatenate([jnp.where(mask, s[0:win], NEG_INF), s[win:n_keys]], axis=0)
            sink = sink_ref[hh:hh + 1, :]
            m = jnp.maximum(_col_max(s), sink)
            p = jnp.exp2(s - m)
            den = _col_sum(p) + jnp.exp2(sink - m)
            o = lax.dot_general(vals[hh // grp], p.astype(BF16), tn_dims, preferred_element_type=F32)
            o_ref[:, hh * A_HEAD_DIM:(hh + 1) * A_HEAD_DIM] = (o * (1.0 / den)).T.astype(o_ref.dtype)

        ahead = s_ref.shape[0] - 1
        for hh in range(min(ahead, A_HEADS)):
            scores(hh, hh)
        for hh in range(A_HEADS):
            if hh + ahead < A_HEADS:
                scores(hh + ahead, (hh + ahead) % (ahead + 1))
            head_out(hh, hh % (ahead + 1))

    def kv_head(x, g):
        return x[:, g * A_HEAD_DIM:(g + 1) * A_HEAD_DIM]

    kc = k_ref[n_lat:n_lat + n_ctx, :]
    vc = v_ref[n_lat:n_lat + n_ctx, :]

    @pl.when(i < n_lat_tiles)
    def _latent():
        q0 = i * tq
        ws = pl.multiple_of(jnp.clip(q0 - WINDOW, 0, n_lat - win), WINDOW)
        kw = k_ref[pl.ds(ws, win), :]
        vw = v_ref[pl.ds(ws, win), :]
        kpos = ws + lax.broadcasted_iota(jnp.int32, (win, tq), 0)
        qpos = q0 + lax.broadcasted_iota(jnp.int32, (win, tq), 1)
        mask = jnp.abs(qpos - kpos) <= WINDOW
        keys = [jnp.concatenate([kv_head(kw, g), kv_head(kc, g)], axis=0) for g in range(A_KV_HEADS)]
        vals = [jnp.concatenate([kv_head(vw, g), kv_head(vc, g)], axis=0) for g in range(A_KV_HEADS)]
        run_tile(keys, vals, mask)

    @pl.when(i >= n_lat_tiles)
    def _context():
        run_tile([kv_head(kc, g) for g in range(A_KV_HEADS)], [kv_head(vc, g) for g in range(A_KV_HEADS)], None)


def _attn_a(qa, ka, va, sink_tab, *, n_lat, n_ctx, tq):
    t_all = qa.shape[0]
    kern = functools.partial(_attn_a_kernel, tq=tq, n_lat=n_lat, n_ctx=n_ctx)
    return pl.pallas_call(
        kern,
        grid=(t_all // tq,),
        in_specs=[pl.BlockSpec((tq, A_WIDTH), lambda i: (i, 0)), _const_spec(ka.shape), _const_spec(va.shape),
                  _const_spec(sink_tab.shape)],
        out_specs=pl.BlockSpec((tq, A_WIDTH), lambda i: (i, 0)),
        out_shape=jax.ShapeDtypeStruct((t_all, A_WIDTH), BF16),
        scratch_shapes=[pltpu.VMEM((4, tq + 2 * WINDOW + n_ctx, tq), F32)],
        compiler_params=_params(("arbitrary",)),
        name="attn_a",
    )(qa, ka, va, sink_tab)


MLA_STRIP = 256
MLA_UNITS_PER_TRIP = 16


def _col_fold(st, pair_op):
    rows = st.shape[0]
    groups = 4 if rows % 32 == 0 else 1
    part = st[0:rows // groups]
    for g in range(1, groups):
        part = pair_op(part, st[g * (rows // groups):(g + 1) * (rows // groups)])
    return part


def _col_max(st):
    return jnp.max(_col_fold(st, jnp.maximum), axis=0, keepdims=True)


def _col_sum(st):
    return jnp.sum(_col_fold(st, jnp.add), axis=0, keepdims=True)


def _mla_kernel(q_ref, k_ref, vt_ref, prev_hbm, o_ref, acc_ref, m_ref, s_ref, *, nk, tq):
    del prev_hbm
    j = pl.program_id(1)
    ns = tq // MLA_STRIP

    @pl.when(j == 0)
    def _init():
        m_ref[...] = jnp.full(m_ref.shape, NEG_INF, F32)
        acc_ref[...] = jnp.zeros(acc_ref.shape, F32)

    tk = k_ref.shape[1]
    k_half = tk // 2
    v_half = (tk // MLA_STRIP + 1) // 2 * MLA_STRIP if tk > MLA_STRIP else tk

    def scores(hh, c, slot):
        q = q_ref[hh, c * MLA_STRIP:(c + 1) * MLA_STRIP, :]
        for r0, r1 in ((0, k_half), (k_half, tk)):
            s_ref[slot, r0:r1, :] = lax.dot_general(k_ref[hh, r0:r1, :], q, NT_DIMS, preferred_element_type=F32)

    def softmax_pv(hh, c, slot):
        cs = slice(c * MLA_STRIP, (c + 1) * MLA_STRIP)
        st = s_ref[slot]
        m_old = m_ref[hh, :, cs]
        m_new = jnp.maximum(m_old, _col_max(st))
        alpha = jnp.exp2(m_old - m_new)
        p = jnp.exp2(st - m_new).astype(BF16)
        pv = jnp.dot(vt_ref[hh, :, 0:v_half], p[0:v_half], preferred_element_type=F32)
        if v_half < tk:
            pv = pv + jnp.dot(vt_ref[hh, :, v_half:tk], p[v_half:tk], preferred_element_type=F32)
        acc_ref[hh, :, cs] = alpha * acc_ref[hh, :, cs] + pv
        m_ref[hh, :, cs] = m_new

    pipelined = ns % 2 == 0

    def head(hh, carry):
        for c in range(ns):
            if not pipelined:
                scores(hh, c, 0)
            elif c + 1 < ns:
                scores(hh, c + 1, (c + 1) % 2)
            else:
                scores(jnp.minimum(hh + 1, B_HEADS - 1), 0, 0)
            softmax_pv(hh, c, c % 2 if pipelined else 0)
        return carry

    if pipelined:
        scores(0, 0, 0)
    lax.fori_loop(0, B_HEADS, head, 0, unroll=max(1, min(B_HEADS, MLA_UNITS_PER_TRIP // ns)))

    @pl.when(j == nk - 1)
    def _finish():
        for hh in range(B_HEADS):
            o = acc_ref[hh, 0:B_V_DIM, :] * (1.0 / acc_ref[hh, B_V_DIM:B_V_DIM + 1, :])
            o_ref[:, hh * B_V_DIM:(hh + 1) * B_V_DIM] = o.T.astype(o_ref.dtype)


def _mla(qb, kb, vbt, prev, *, q_row0, n_q, k_row0, n_k, tq, tk):
    t_all = qb.shape[1]
    nq, nk = n_q // tq, n_k // tk
    qo, ko = q_row0 // tq, k_row0 // tk
    kern = functools.partial(_mla_kernel, nk=nk, tq=tq)
    return pl.pallas_call(
        kern,
        grid=(nq, nk),
        in_specs=[pl.BlockSpec((B_HEADS, tq, B_QK_PAD), lambda i, j: (0, qo + i, 0)),
                  pl.BlockSpec((B_HEADS, tk, B_QK_PAD), lambda i, j: (0, ko + j, 0)),
                  pl.BlockSpec((B_HEADS, B_VT_ROWS, tk), lambda i, j: (0, 0, ko + j)),
                  pl.BlockSpec(memory_space=pl.ANY)],
        out_specs=pl.BlockSpec((tq, B_WIDTH), lambda i, j: (qo + i, 0)),
        out_shape=jax.ShapeDtypeStruct((t_all, B_WIDTH), BF16),
        scratch_shapes=[pltpu.VMEM((B_HEADS, B_VT_ROWS, tq), F32), pltpu.VMEM((B_HEADS, 1, tq), F32),
                        pltpu.VMEM((2, tk, MLA_STRIP), F32)],
        input_output_aliases={3: 0},
        compiler_params=_params(("arbitrary", "arbitrary")),
        name="mla_attn",
    )(qb, kb, vbt, prev)


def _out_kernel(oa_ref, ob_ref, x_ref, mod_ref, ga_ref, gb_ref, wo_ref, pg_ref, o_ref, *, tm, n_lat):
    d = x_ref.shape[-1]
    is_ctx = _is_ctx_rows(pl.program_id(0), tm, n_lat)
    a = (_rms(oa_ref[...].astype(F32)) * ga_ref[...]).astype(BF16)
    b = (_rms(ob_ref[...].astype(F32)) * gb_ref[...]).astype(BF16)
    y = (jnp.dot(a, wo_ref[0:A_WIDTH, :], preferred_element_type=F32)
         + jnp.dot(b, wo_ref[A_WIDTH:A_WIDTH + B_WIDTH, :], preferred_element_type=F32))
    o_ref[...] = x_ref[...] + _mod_select(mod_ref, 2, d, is_ctx) * (_rms(y) * pg_ref[...])


def _attn_out(oa, ob, xs, mod, ga, gb, wo, pg, *, n_lat, tm):
    t_all, d = xs.shape
    row = lambda i: (i, 0)
    kern = functools.partial(_out_kernel, tm=tm, n_lat=n_lat)
    return pl.pallas_call(
        kern,
        grid=(t_all // tm,),
        in_specs=[pl.BlockSpec((tm, A_WIDTH), row), pl.BlockSpec((tm, B_WIDTH), row), pl.BlockSpec((tm, d), row),
                  _const_spec(mod.shape), _const_spec(ga.shape), _const_spec(gb.shape), _const_spec(wo.shape),
                  _const_spec(pg.shape)],
        out_specs=pl.BlockSpec((tm, d), row),
        out_shape=jax.ShapeDtypeStruct((t_all, d), F32),
        compiler_params=_params(("arbitrary",)),
        name="attn_out_mix",
    )(oa, ob, xs, mod, ga, gb, wo, pg)


def _swiglu_step(build_h, w1_ref, w3_ref, w2_ref, o_ref, h_ref, alongside=None):
    j = pl.program_id(1)

    @pl.when(j == 0)
    def _prologue():
        build_h()
        o_ref[...] = jnp.zeros(o_ref.shape, o_ref.dtype)

    if alongside is not None:
        alongside()
    h = h_ref[...]
    a = jnp.dot(h, w1_ref[...].astype(BF16), preferred_element_type=F32)
    b = jnp.dot(h, w3_ref[...].astype(BF16), preferred_element_type=F32)
    u = (a * (1.0 / (1.0 + jnp.exp(-a))) * b).astype(BF16)
    o_ref[...] += jnp.dot(u, w2_ref[...].astype(BF16), preferred_element_type=F32)


def _ffn_dense_kernel(x_ref, mod_ref, g_ref, pg_ref, w1_ref, w3_ref, w2_ref, o_ref, h_ref, gs_ref, acc_ref, *, tm,
                      n_lat, nj):
    tile_row0 = pl.program_id(0) * tm
    build_h = functools.partial(_norm_mod_rows, x_ref, g_ref, mod_ref, 3, tile_row0, n_lat, h_ref, gs_ref)
    _swiglu_step(build_h, w1_ref, w3_ref, w2_ref, acc_ref, h_ref)

    @pl.when(pl.program_id(1) == nj - 1)
    def _epilogue():
        _resid_rows(o_ref, acc_ref, x_ref, pg_ref, mod_ref, 5, tile_row0, n_lat, gs_ref)


def _ffn_dense(xs, mod, g, pg, w1, w3, w2, *, n_lat, tm, tf):
    t_all, d = xs.shape
    dff = w1.shape[1]
    nj = dff // tf
    kern = functools.partial(_ffn_dense_kernel, tm=tm, n_lat=n_lat, nj=nj)
    return pl.pallas_call(
        kern,
        grid=(t_all // tm, nj),
        in_specs=[pl.BlockSpec((tm, d), lambda i, j: (i, 0)), _const_spec(mod.shape), _const_spec(g.shape),
                  _const_spec(pg.shape),
                  pl.BlockSpec((d, tf), lambda i, j: (0, j)), pl.BlockSpec((d, tf), lambda i, j: (0, j)),
                  pl.BlockSpec((tf, d), lambda i, j: (j, 0))],
        out_specs=pl.BlockSpec((tm, d), lambda i, j: (i, 0)),
        out_shape=jax.ShapeDtypeStruct((t_all, d), F32),
        scratch_shapes=[pltpu.VMEM((tm, d), BF16), pltpu.VMEM((8, d), F32), pltpu.VMEM((tm, d), F32)],
        compiler_params=_params(("arbitrary", "arbitrary")),
        name="ffn_dense",
    )(xs, mod, g, pg, w1, w3, w2)


def _ffn_moe_kernel(te_ref, nu_ref, cur_ref, nxt_ref, x_hbm, w1_ref, w3_ref, w2_ref, o_ref, h_ref, xbuf, sem, *, tm,
                    nj):
    t, j = pl.program_id(0), pl.program_id(1)
    n_used = nu_ref[0]
    used = t < n_used
    slot = t % 2
    rows_per_step = tm // nj

    def row_copy(src_ref, r, s):
        return pltpu.make_async_copy(x_hbm.at[pl.ds(src_ref[0, 0, r], 1)], xbuf.at[s, pl.ds(r, 1)], sem.at[s])

    @pl.when((t == 0) & (j == 0))
    def _first_tile():
        lax.fori_loop(0, tm, lambda r, c: (row_copy(cur_ref, r, 0).start(), c)[1], 0, unroll=DMA_UNROLL)

    @pl.when((t <= n_used) & (j == 0))
    def _rows_arrived():
        lax.fori_loop(0, tm, lambda r, c: (row_copy(cur_ref, r, slot).wait(), c)[1], 0, unroll=DMA_UNROLL)

    @pl.when(used)
    def _used():
        def request_next_tile():
            for k in range(rows_per_step):
                row_copy(nxt_ref, j * rows_per_step + k, 1 - slot).start()

        def build_h():
            half = h_ref.shape[1] // 2
            lo, hi = _unpack_bf16_halves(xbuf[slot])
            h_ref[:, 0:half] = lo
            h_ref[:, half:2 * half] = hi

        _swiglu_step(build_h, w1_ref, w3_ref, w2_ref, o_ref, h_ref, alongside=request_next_tile)

    @pl.when(jnp.logical_not(used) & (j == 0))
    def _idle():
        o_ref[...] = jnp.zeros(o_ref.shape, o_ref.dtype)


def _ffn_moe(hp, slot_src, tile_expert, n_used, w1, w3, w2, *, tm, tf):
    d = 2 * hp.shape[1]
    n_tiles = slot_src.shape[0]
    dff = w1.shape[2]
    nj = dff // tf
    assert tm % nj == 0, "the next tile's row requests are spread evenly over the d_ff chunks"

    def t_eff(t, nu):
        return jnp.minimum(t, nu[0] - 1)

    def j_eff(t, j, nu):
        return jnp.where(t < nu[0], j, nj - 1)

    smem_rows = lambda fn: pl.BlockSpec((1, 1, tm), fn, memory_space=pltpu.SMEM)
    grid_spec = pltpu.PrefetchScalarGridSpec(
        num_scalar_prefetch=2,
        grid=(n_tiles, nj),
        in_specs=[smem_rows(lambda t, j, te, nu: (t, 0, 0)),
                  smem_rows(lambda t, j, te, nu: (jnp.minimum(t + 1, n_tiles - 1), 0, 0)),
                  pl.BlockSpec(memory_space=pl.ANY),
                  pl.BlockSpec((None, d, tf), lambda t, j, te, nu: (te[t_eff(t, nu)], 0, j_eff(t, j, nu))),
                  pl.BlockSpec((None, d, tf), lambda t, j, te, nu: (te[t_eff(t, nu)], 0, j_eff(t, j, nu))),
                  pl.BlockSpec((None, tf, d), lambda t, j, te, nu: (te[t_eff(t, nu)], j_eff(t, j, nu), 0))],
        out_specs=pl.BlockSpec((tm, d), lambda t, j, te, nu: (t, 0)),
        scratch_shapes=[pltpu.VMEM((tm, d), BF16), pltpu.VMEM((2, tm, d // 2), jnp.uint32),
                        pltpu.SemaphoreType.DMA((2,))],
    )
    return pl.pallas_call(
        functools.partial(_ffn_moe_kernel, tm=tm, nj=nj),
        grid_spec=grid_spec,
        out_shape=jax.ShapeDtypeStruct((n_tiles * tm, d), F32),
        compiler_params=_params(("arbitrary", "arbitrary")),
        name="ffn_moe",
    )(tile_expert, n_used, slot_src, slot_src, hp, w1, w3, w2)


def _pack_bf16_halves(h):
    half = h.shape[1] // 2
    bits = lax.bitcast_convert_type(h.astype(BF16).astype(F32), jnp.uint32)
    return (bits[:, 0:half] >> 16) | bits[:, half:2 * half]


def _unpack_bf16_halves(u):
    lo = lax.bitcast_convert_type(u << 16, F32).astype(BF16)
    hi = lax.bitcast_convert_type(u & jnp.uint32(0xFFFF0000), F32).astype(BF16)
    return lo, hi


def _bf16_part(x):
    return lax.bitcast_convert_type(lax.bitcast_convert_type(x, jnp.uint32) & jnp.uint32(0xFFFF0000), F32)


def _route_kernel(x_ref, mod_ref, g_ref, rw_ref, o_ref, hp_ref):
    d = x_ref.shape[-1]
    h = _rms(x_ref[...]) * g_ref[...] * (1.0 + mod_ref[0:1, 4 * d:5 * d]) + mod_ref[0:1, 3 * d:4 * d]
    hp_ref[...] = _pack_bf16_halves(h)
    h_top = _bf16_part(h)
    h_hi, h_lo = h_top.astype(BF16), (h - h_top).astype(BF16)
    hi = jnp.dot(h_hi, rw_ref[0], preferred_element_type=F32)
    logits = hi + pltpu.roll(hi, LANE - N_EXPERTS, 1) + jnp.dot(h_lo, rw_ref[1], preferred_element_type=F32)
    lane = lax.broadcasted_iota(jnp.int32, logits.shape, 1)
    lg = jnp.where(lane < N_EXPERTS, logits, -jnp.inf)
    m1 = jnp.max(lg, axis=-1, keepdims=True)
    i1 = jnp.min(jnp.where(lg == m1, lane, LANE), axis=-1, keepdims=True)
    lg2 = jnp.where(lane == i1, -jnp.inf, lg)
    m2 = jnp.max(lg2, axis=-1, keepdims=True)
    i2 = jnp.min(jnp.where(lg2 == m2, lane, LANE), axis=-1, keepdims=True)
    e = jnp.exp(m2 - m1)
    w1 = 1.0 / (1.0 + e)
    w2 = e * w1
    o_ref[...] = jnp.where(lane == 0, i1.astype(F32),
                           jnp.where(lane == 1, i2.astype(F32), jnp.where(lane == 2, w1, jnp.where(lane == 3, w2, 0.0))))


def _route(xs, mod, g, rw_pad, *, n_lat, tm):
    d = xs.shape[1]
    return pl.pallas_call(
        _route_kernel,
        grid=(n_lat // tm,),
        in_specs=[pl.BlockSpec((tm, d), lambda i: (i, 0)), _const_spec(mod.shape), _const_spec(g.shape),
                  _const_spec(rw_pad.shape)],
        out_specs=[pl.BlockSpec((tm, LANE), lambda i: (i, 0)), pl.BlockSpec((tm, d // 2), lambda i: (i, 0))],
        out_shape=[jax.ShapeDtypeStruct((n_lat, LANE), F32), jax.ShapeDtypeStruct((n_lat, d // 2), jnp.uint32)],
        compiler_params=_params(("arbitrary",)),
        name="moe_route",
    )(xs, mod, g, rw_pad)


def _combine_kernel(cur_ref, nxt_ref, rt_ref, x_ref, mod_ref, pg_ref, y_hbm, o_ref, buf, sem, *, tm, n_steps):
    i = pl.program_id(0)
    slot = i % 2

    def row_copy(dest_ref, r, k, s):
        return pltpu.make_async_copy(y_hbm.at[pl.ds(dest_ref[0, k, r], 1)], buf.at[s, k, pl.ds(r, 1)], sem.at[s])

    def start_tile0(r, c):
        row_copy(cur_ref, r, 0, 0).start()
        row_copy(cur_ref, r, 1, 0).start()
        return c

    def wait(r, c):
        row_copy(cur_ref, r, 0, slot).wait()
        row_copy(cur_ref, r, 1, slot).wait()
        return c

    @pl.when(i == 0)
    def _first_tile():
        lax.fori_loop(0, tm, start_tile0, 0, unroll=DMA_UNROLL)

    lax.fori_loop(0, tm, wait, 0, unroll=DMA_UNROLL)

    @pl.when(i < n_steps)
    def _tile():
        for r in range(tm):
            row_copy(nxt_ref, r, 0, 1 - slot).start()
            row_copy(nxt_ref, r, 1, 1 - slot).start()
        d = x_ref.shape[-1]
        rt = rt_ref[...]
        y = rt[:, 2:3] * buf[slot, 0] + rt[:, 3:4] * buf[slot, 1]
        o_ref[...] = x_ref[...] + mod_ref[0:1, 5 * d:6 * d] * (_rms(y) * pg_ref[...])


def _combine(yg, dest, route, xs, mod, pg, *, n_lat, tm):
    d = xs.shape[1]
    n_steps = n_lat // tm
    kern = functools.partial(_combine_kernel, tm=tm, n_steps=n_steps)
    tile = lambda i: jnp.minimum(i, n_steps - 1)
    nxt_tile = lambda i: jnp.minimum(i + 1, n_steps - 1)
    return pl.pallas_call(
        kern,
        grid=(n_steps + 1,),
        in_specs=[pl.BlockSpec((1, 2, tm), lambda i: (tile(i), 0, 0), memory_space=pltpu.SMEM),
                  pl.BlockSpec((1, 2, tm), lambda i: (nxt_tile(i), 0, 0), memory_space=pltpu.SMEM),
                  pl.BlockSpec((tm, LANE), lambda i: (tile(i), 0)), pl.BlockSpec((tm, d), lambda i: (tile(i), 0)),
                  _const_spec(mod.shape), _const_spec(pg.shape), pl.BlockSpec(memory_space=pl.ANY)],
        out_specs=pl.BlockSpec((tm, d), lambda i: (tile(i), 0)),
        out_shape=jax.ShapeDtypeStruct((n_lat, d), F32),
        scratch_shapes=[pltpu.VMEM((2, 2, tm, d), F32), pltpu.SemaphoreType.DMA((2,))],
        compiler_params=_params(("arbitrary",)),
        name="moe_combine",
    )(dest, dest, route, xs, mod, pg, yg)


def _rope_tables(n_lat, n_ctx):
    rows = n_lat // GRID_W

    def tabs(rot_dim):
        axis_dim = rot_dim // 2
        quarter = rot_dim // 4
        inv_freq = ROPE_BASE ** (-jnp.arange(0, axis_dim, 2, dtype=F32) / axis_dim)
        ang_r = jnp.arange(rows, dtype=F32)[:, None] * inv_freq[None, :]
        ang_c = jnp.arange(GRID_W, dtype=F32)[:, None] * inv_freq[None, :]

        def over_grid(fn):
            r = jnp.broadcast_to(fn(ang_r)[:, None, :], (rows, GRID_W, quarter))
            c = jnp.broadcast_to(fn(ang_c)[None, :, :], (rows, GRID_W, quarter))
            return jnp.concatenate([r, r, c, c], axis=-1).reshape(n_lat, rot_dim)

        cos, sin = over_grid(jnp.cos), over_grid(jnp.sin)
        lower = (jnp.arange(rot_dim) % (2 * quarter)) < quarter
        sin_m = jnp.where(lower[None, :], -sin, 0.0)
        sin_p = jnp.where(lower[None, :], 0.0, sin)
        pad = LANE - rot_dim
        cos = jnp.pad(cos, ((0, n_ctx), (0, pad)), constant_values=1.0)
        sin_m = jnp.pad(sin_m, ((0, n_ctx), (0, pad)))
        sin_p = jnp.pad(sin_p, ((0, n_ctx), (0, pad)))
        return [cos, sin_m, sin_p]

    return jnp.stack(tabs(A_HEAD_DIM) + tabs(B_ROPE_DIM))


def _layer_weights(i, w_in, w_uq, w_ukv, w_o):
    d = w_in.shape[1]
    q_w = A_WIDTH + B_Q_RANK
    wi = w_in[i]
    qa, cq = wi[:, :A_WIDTH], wi[:, A_WIDTH:q_w]
    o1, o2, o3 = q_w + A_KV_WIDTH, q_w + 2 * A_KV_WIDTH, q_w + 2 * A_KV_WIDTH + B_KV_RANK
    ka, va, ckv, kr = wi[:, q_w:o1], wi[:, o1:o2], wi[:, o2:o3], wi[:, o3:]
    win = jnp.concatenate([qa, ka, va, cq, ckv, kr, jnp.zeros((d, LANE - B_ROPE_DIM), F32)], axis=1).astype(BF16)
    uq = w_uq[i].reshape(B_Q_RANK, B_HEADS, B_NOPE_DIM + B_ROPE_DIM)
    uq = jnp.pad(uq, ((0, 0), (0, 0), (0, B_QK_PAD - B_NOPE_DIM - B_ROPE_DIM)))
    wuq = uq.reshape(B_Q_RANK, B_HEADS * B_QK_PAD).astype(BF16)
    ukv = w_ukv[i].reshape(B_KV_RANK, B_HEADS, B_NOPE_DIM + B_V_DIM)
    wukn = ukv[:, :, :B_NOPE_DIM].reshape(B_KV_RANK, B_HEADS * B_NOPE_DIM).astype(BF16)
    wuvt = ukv[:, :, B_NOPE_DIM:].reshape(B_KV_RANK, B_HEADS * B_V_DIM).T.astype(BF16)
    return win, wuq, wukn, wuvt, w_o[i].astype(BF16)


def _routing_slots(route, n_lat, tm_e, n_tiles):
    e12 = route[:, 0:2].astype(jnp.int32).T.reshape(-1)
    onehot = (e12[:, None] == jnp.arange(N_EXPERTS)[None, :]).astype(jnp.int32)
    csum = jnp.cumsum(onehot, axis=0)
    rank = jnp.sum(csum * onehot, axis=1) - 1
    counts = csum[-1]
    tiles_per = (counts + tm_e - 1) // tm_e
    tile_end = jnp.cumsum(tiles_per)
    start = (tile_end - tiles_per) * tm_e
    dest = (jnp.sum(start[None, :] * onehot, axis=1) + rank).astype(jnp.int32)
    n_used = tile_end[-1].astype(jnp.int32).reshape(1)
    t_idx = jnp.arange(n_tiles, dtype=jnp.int32)
    tile_expert = jnp.minimum(jnp.sum((t_idx[:, None] >= tile_end[None, :]).astype(jnp.int32), axis=1),
                              N_EXPERTS - 1).astype(jnp.int32)
    token = jnp.tile(jnp.arange(n_lat, dtype=jnp.int32), 2)
    slot_src = jnp.zeros((n_tiles * tm_e,), jnp.int32).at[dest].set(token, unique_indices=True)
    return dest.reshape(2, n_lat), slot_src.reshape(n_tiles, 1, tm_e), tile_expert, n_used


def _tile(n, candidates):
    for c in candidates:
        if n % c == 0:
            return c
    raise ValueError(f"no supported tile size for {n} rows")


def kernel(x, c, ctx, c_ctx, w_ada, b_ada, pre_attn_g, post_attn_g, pre_ffn_g, post_ffn_g, w_in, attn_sink, q_norm_g,
           kv_norm_g, w_uq, w_ukv, grp_a_g, grp_b_g, w_o, ffn_w1, ffn_w3, ffn_w2, router_w, moe_w1, moe_w3, moe_w2):
    b, n_lat, d = x.shape
    n_ctx = ctx.shape[1]
    depth = w_ada.shape[0]
    assert b == 1 and c.shape[0] == 1, "single-sequence kernel"
    t_all = n_lat + n_ctx
    tm = _tile(t_all, (640, 256, 128))
    tq_a = 256
    tq_b = _tile(n_lat, (1024, 512, 256))
    tk_b = _tile(t_all, (1280, 256))
    tf = 512
    tm_r = _tile(n_lat, (512, 256))
    tm_c = 256
    assert n_ctx == tq_a and n_lat % tq_a == 0 and n_lat >= tq_a + 2 * WINDOW

    xs = jnp.concatenate([x[0], ctx[0]], axis=0)
    cc = jnp.stack([c[0], c_ctx])
    mods = _ada(cc, w_ada, b_ada)
    rope_tab = _rope_tables(n_lat, n_ctx)
    row2 = lambda v: v.reshape(1, -1)

    for i in range(depth):
        last = i == depth - 1
        mod = mods[i]
        win, wuq, wukn, wuvt, wo = _layer_weights(i, w_in, w_uq, w_ukv, w_o)
        qa, ka, va, qb, kb, vbt = _proj(xs, mod, row2(pre_attn_g[i]), win, row2(q_norm_g[i]), row2(kv_norm_g[i]),
                                        wuq, wukn, wuvt, rope_tab, n_lat=n_lat, tm=tm)
        sink_tab = jnp.broadcast_to((attn_sink[i] * LOG2E)[:, None], (A_HEADS, tq_a))
        oa = _attn_a(qa, ka, va, sink_tab, n_lat=n_lat, n_ctx=n_ctx, tq=tq_a)
        ob = _mla(qb, kb, vbt, jnp.zeros((t_all, B_WIDTH), BF16), q_row0=0, n_q=n_lat, k_row0=0, n_k=t_all,
                  tq=tq_b, tk=tk_b)
        ob = _mla(qb, kb, vbt, ob, q_row0=n_lat, n_q=n_ctx, k_row0=n_lat, n_k=n_ctx, tq=n_ctx, tk=n_ctx)
        xs = _attn_out(oa, ob, xs, mod, row2(grp_a_g[i]), row2(grp_b_g[i]), wo, row2(post_attn_g[i]),
                       n_lat=n_lat, tm=tm)
        jj = i // 2
        if i % 2 == 0:
            xs = _ffn_dense(xs, mod, row2(pre_ffn_g[i]), row2(post_ffn_g[i]), ffn_w1[jj].astype(BF16),
                            ffn_w3[jj].astype(BF16), ffn_w2[jj].astype(BF16), n_lat=n_lat, tm=tm, tf=tf)
        else:
            n_rows = n_lat
            assert last, "expert layers other than the last would also need the context rows routed"
            nj = moe_w1.shape[-1] // tf
            tm_e = -(-tm // (nj * ROW_CHUNK)) * ROW_CHUNK * nj
            n_tiles = (2 * n_rows + N_EXPERTS * (tm_e - 1)) // tm_e + 1
            rw_top = _bf16_part(router_w[jj])
            rw_hi, rw_lo = rw_top.astype(BF16), (router_w[jj] - rw_top).astype(BF16)
            zeros = jnp.zeros((d, LANE - 2 * N_EXPERTS), BF16)
            rw_pad = jnp.stack([jnp.concatenate([rw_hi, rw_lo, zeros], axis=1),
                                jnp.concatenate([rw_hi, jnp.zeros_like(rw_lo), zeros], axis=1)])
            route, hp = _route(xs, mod, row2(pre_ffn_g[i]), rw_pad, n_lat=n_rows, tm=tm_r)
            dest, slot_src, tile_expert, n_used = _routing_slots(route, n_rows, tm_e, n_tiles)
            dest_t = dest.reshape(2, n_rows // tm_c, tm_c).transpose(1, 0, 2)
            yg = _ffn_moe(hp, slot_src, tile_expert, n_used, moe_w1[jj], moe_w3[jj], moe_w2[jj], tm=tm_e, tf=tf)
            xs = _combine(yg, dest_t, route, xs, mod, row2(post_ffn_g[i]), n_lat=n_rows, tm=tm_c)
    return xs[:n_lat].reshape(b, n_lat, d)
```

```python
import functools
import math

import jax
import jax.numpy as jnp
from jax import lax
from jax.experimental import pallas as pl
from jax.experimental.pallas import tpu as pltpu

GRID_W = 64
EPS = 1e-6
NEG_INF = -1e30
ROPE_BASE = 10000.0
A_HEADS = 8
A_KV_HEADS = 2
A_HEAD_DIM = 128
WINDOW = 128
B_HEADS = 8
B_Q_RANK = 384
B_KV_RANK = 256
B_NOPE_DIM = 128
B_ROPE_DIM = 64
B_V_DIM = 128
N_EXPERTS = 8
A_WIDTH = A_HEADS * A_HEAD_DIM
A_KV_WIDTH = A_KV_HEADS * A_HEAD_DIM
B_WIDTH = B_HEADS * B_V_DIM
B_QK_PAD = 256
B_VT_ROWS = B_V_DIM + 16

LOG2E = math.log2(math.e)
LANE = 128
VMEM_LIMIT_BYTES = 60 * 1024 * 1024

F32 = jnp.float32
BF16 = jnp.bfloat16
NT_DIMS = (((1,), (1,)), ((), ()))


def _params(sem, vmem=VMEM_LIMIT_BYTES):
    return pltpu.CompilerParams(dimension_semantics=sem, vmem_limit_bytes=vmem)


def _const_spec(shape):
    nd = len(shape)
    return pl.BlockSpec(shape, lambda *_: (0,) * nd, pipeline_mode=pl.Buffered(1))


def _rms(x):
    return x * lax.rsqrt(jnp.mean(x * x, axis=-1, keepdims=True) + EPS)


DMA_UNROLL = 8
ROW_CHUNK = 16
ROW_UNROLL = 8


def _row_chunks(tm, body):
    def step(r, carry):
        body(pl.ds(pl.multiple_of(r * ROW_CHUNK, ROW_CHUNK), ROW_CHUNK), r * ROW_CHUNK)
        return carry

    lax.fori_loop(0, tm // ROW_CHUNK, step, 0, unroll=ROW_UNROLL)


def _mod_row(tile_row0, r0, n_lat):
    return jnp.where(tile_row0 + r0 >= n_lat, 1, 0)


def _norm_mod_rows(x_ref, g_ref, mod_ref, k_shift, tile_row0, n_lat, h_ref, gs_ref):
    tm, d = x_ref.shape
    gs_ref[...] = g_ref[...] * (1.0 + mod_ref[:, (k_shift + 1) * d:(k_shift + 2) * d])

    def body(rows, r0):
        mrow = _mod_row(tile_row0, r0, n_lat)
        h = _rms(x_ref[rows, :]) * gs_ref[pl.ds(mrow, 1), :]
        h_ref[rows, :] = (h + mod_ref[pl.ds(mrow, 1), k_shift * d:(k_shift + 1) * d]).astype(BF16)

    _row_chunks(tm, body)


def _resid_rows(o_ref, y_ref, x_ref, pg_ref, mod_ref, k_gate, tile_row0, n_lat, gs_ref):
    tm, d = x_ref.shape
    gs_ref[...] = pg_ref[...] * mod_ref[:, k_gate * d:(k_gate + 1) * d]

    def body(rows, r0):
        mrow = _mod_row(tile_row0, r0, n_lat)
        o_ref[rows, :] = x_ref[rows, :] + gs_ref[pl.ds(mrow, 1), :] * _rms(y_ref[rows, :])

    _row_chunks(tm, body)


def _is_ctx_rows(tile_index, tm, n_lat):
    row = tile_index * tm + lax.broadcasted_iota(jnp.int32, (tm, 1), 0)
    return row >= n_lat


def _mod_select(mod_ref, k, d, is_ctx):
    return jnp.where(is_ctx, mod_ref[1:2, k * d:(k + 1) * d], mod_ref[0:1, k * d:(k + 1) * d])


ADA_ROWS = 2


def _ada_kernel(c_ref, w_ref, b_ref, o_ref):
    tn = w_ref.shape[-1]
    o_ref[...] = jnp.zeros(o_ref.shape, o_ref.dtype)
    for r in range(ADA_ROWS):
        c = c_ref[r]
        a = c * (1.0 / (1.0 + jnp.exp(-c)))
        for l in range(tn // LANE):
            cols = slice(l * LANE, (l + 1) * LANE)
            o_ref[r:r + 1, cols] = _col_sum(a * w_ref[:, cols]) + b_ref[:, cols]


def _ada(cc, w_ada, b_ada):
    depth, d, n = w_ada.shape
    tn = n // 8
    c_lanes = jnp.broadcast_to(cc[:, :, None], (ADA_ROWS, d, LANE))
    return pl.pallas_call(
        _ada_kernel,
        grid=(depth, n // tn),
        in_specs=[pl.BlockSpec((ADA_ROWS, d, LANE), lambda l, j: (0, 0, 0)),
                  pl.BlockSpec((None, d, tn), lambda l, j: (l, 0, j)),
                  pl.BlockSpec((None, 1, tn), lambda l, j: (l, 0, j))],
        out_specs=pl.BlockSpec((None, 8, tn), lambda l, j: (l, 0, j)),
        out_shape=jax.ShapeDtypeStruct((depth, 8, n), F32),
        compiler_params=_params(("arbitrary", "arbitrary")),
        name="ada_mod",
    )(c_lanes, w_ada, b_ada.reshape(depth, 1, n))


def _rope(t, cos, sin_m, sin_p, quarter):
    n = t.shape[-1]
    return t * cos + pltpu.roll(t, n - quarter, 1) * sin_m + pltpu.roll(t, quarter, 1) * sin_p


def _proj_kernel(x_ref, mod_ref, g_ref, win_ref, qg_ref, kvg_ref, wuq_ref, wukn_ref, wuvt_ref, rope_ref,
                 qa_ref, ka_ref, va_ref, qb_ref, kb_ref, vbt_ref, *, tm, n_lat, scale_a, scale_b):
    d = x_ref.shape[-1]
    is_ctx = _is_ctx_rows(pl.program_id(0), tm, n_lat)
    h = _rms(x_ref[...]) * g_ref[...] * (1.0 + _mod_select(mod_ref, 1, d, is_ctx)) + _mod_select(mod_ref, 0, d, is_ctx)
    proj = jnp.dot(h.astype(BF16), win_ref[...], preferred_element_type=F32)
    cos_a, sinm_a, sinp_a = rope_ref[0], rope_ref[1], rope_ref[2]
    cos_b, sinm_b, sinp_b = rope_ref[3], rope_ref[4], rope_ref[5]
    qa_q = A_HEAD_DIM // 4
    b_q = B_ROPE_DIM // 4
    cos_as, sinm_as, sinp_as = cos_a * scale_a, sinm_a * scale_a, sinp_a * scale_a
    for hh in range(A_HEADS):
        t = proj[:, hh * LANE:(hh + 1) * LANE]
        qa_ref[:, hh * LANE:(hh + 1) * LANE] = _rope(t, cos_as, sinm_as, sinp_as, qa_q).astype(BF16)
    o = A_WIDTH
    for hh in range(A_KV_HEADS):
        t = proj[:, o + hh * LANE:o + (hh + 1) * LANE]
        ka_ref[:, hh * LANE:(hh + 1) * LANE] = _rope(t, cos_a, sinm_a, sinp_a, qa_q).astype(BF16)
    o += A_KV_WIDTH
    va_ref[...] = proj[:, o:o + A_KV_WIDTH].astype(BF16)
    o += A_KV_WIDTH
    cq = (_rms(proj[:, o:o + B_Q_RANK]) * qg_ref[...]).astype(BF16)
    o += B_Q_RANK
    qm = jnp.dot(cq, wuq_ref[...], preferred_element_type=F32)
    cos_bs, sinm_bs, sinp_bs = cos_b * scale_b, sinm_b * scale_b, sinp_b * scale_b
    for hh in range(B_HEADS):
        b0 = hh * B_QK_PAD
        qb_ref[hh, :, 0:LANE] = (qm[:, b0:b0 + LANE] * scale_b).astype(BF16)
        qb_ref[hh, :, LANE:2 * LANE] = _rope(qm[:, b0 + LANE:b0 + 2 * LANE], cos_bs, sinm_bs, sinp_bs, b_q).astype(BF16)
    ckv = (_rms(proj[:, o:o + B_KV_RANK]) * kvg_ref[...]).astype(BF16)
    o += B_KV_RANK
    kr = _rope(proj[:, o:o + LANE], cos_b, sinm_b, sinp_b, b_q).astype(BF16)
    kn = jnp.dot(ckv, wukn_ref[...], preferred_element_type=F32)
    vt = lax.dot_general(wuvt_ref[...], ckv, NT_DIMS, preferred_element_type=F32)
    for hh in range(B_HEADS):
        kb_ref[hh, :, 0:LANE] = kn[:, hh * LANE:(hh + 1) * LANE].astype(BF16)
        kb_ref[hh, :, LANE:2 * LANE] = kr
        vbt_ref[hh, 0:B_V_DIM, :] = vt[hh * B_V_DIM:(hh + 1) * B_V_DIM, :].astype(BF16)
        vbt_ref[hh, B_V_DIM:B_VT_ROWS, :] = jnp.ones((B_VT_ROWS - B_V_DIM, tm), BF16)


def _proj(xs, mod, g, win, qg, kvg, wuq, wukn, wuvt, rope_tab, *, n_lat, tm):
    t_all, d = xs.shape
    row = lambda i: (i, 0)
    kern = functools.partial(_proj_kernel, tm=tm, n_lat=n_lat, scale_a=LOG2E / math.sqrt(A_HEAD_DIM),
                             scale_b=LOG2E / math.sqrt(B_NOPE_DIM + B_ROPE_DIM))
    return pl.pallas_call(
        kern,
        grid=(t_all // tm,),
        in_specs=[pl.BlockSpec((tm, d), row), _const_spec(mod.shape), _const_spec(g.shape), _const_spec(win.shape),
                  _const_spec(qg.shape), _const_spec(kvg.shape), _const_spec(wuq.shape), _const_spec(wukn.shape),
                  _const_spec(wuvt.shape), pl.BlockSpec((6, tm, LANE), lambda i: (0, i, 0))],
        out_specs=[pl.BlockSpec((tm, A_WIDTH), row), pl.BlockSpec((tm, A_KV_WIDTH), row),
                   pl.BlockSpec((tm, A_KV_WIDTH), row),
                   pl.BlockSpec((B_HEADS, tm, B_QK_PAD), lambda i: (0, i, 0)),
                   pl.BlockSpec((B_HEADS, tm, B_QK_PAD), lambda i: (0, i, 0)),
                   pl.BlockSpec((B_HEADS, B_VT_ROWS, tm), lambda i: (0, 0, i))],
        out_shape=[jax.ShapeDtypeStruct((t_all, A_WIDTH), BF16), jax.ShapeDtypeStruct((t_all, A_KV_WIDTH), BF16),
                   jax.ShapeDtypeStruct((t_all, A_KV_WIDTH), BF16),
                   jax.ShapeDtypeStruct((B_HEADS, t_all, B_QK_PAD), BF16),
                   jax.ShapeDtypeStruct((B_HEADS, t_all, B_QK_PAD), BF16),
                   jax.ShapeDtypeStruct((B_HEADS, B_VT_ROWS, t_all), BF16)],
        compiler_params=_params(("arbitrary",)),
        name="pre_attn_proj",
    )(xs, mod, g, win, qg, kvg, wuq, wukn, wuvt, rope_tab)


def _attn_a_kernel(q_ref, k_ref, v_ref, sink_ref, o_ref, s_ref, *, tq, n_lat, n_ctx):
    i = pl.program_id(0)
    n_lat_tiles = n_lat // tq
    win = tq + 2 * WINDOW
    grp = A_HEADS // A_KV_HEADS
    tn_dims = (((0,), (0,)), ((), ()))

    def run_tile(keys, vals, mask):
        n_keys = keys[0].shape[0]

        def scores(hh, slot):
            q = q_ref[:, hh * A_HEAD_DIM:(hh + 1) * A_HEAD_DIM]
            s_ref[slot, 0:n_keys, :] = lax.dot_general(keys[hh // grp], q, NT_DIMS, preferred_element_type=F32)

        def head_out(hh, slot):
            s = s_ref[slot, 0:n_keys, :]
            if mask is not None:
                s = jnp.concatenate([jnp.where(mask, s[0:win], NEG_INF), s[win:n_keys]], axis=0)
            sink = sink_ref[hh:hh + 1, :]
            m = jnp.maximum(_col_max(s), sink)
            p = jnp.exp2(s - m)
            den = _col_sum(p) + jnp.exp2(sink - m)
            o = lax.dot_general(vals[hh // grp], p.astype(BF16), tn_dims, preferred_element_type=F32)
            o_ref[:, hh * A_HEAD_DIM:(hh + 1) * A_HEAD_DIM] = (o * (1.0 / den)).T.astype(o_ref.dtype)

        ahead = s_ref.shape[0] - 1
        for hh in range(min(ahead, A_HEADS)):
            scores(hh, hh)
        for hh in range(A_HEADS):
            if hh + ahead < A_HEADS:
                scores(hh + ahead, (hh + ahead) % (ahead + 1))
            head_out(hh, hh % (ahead + 1))

    def kv_head(x, g):
        return x[:, g * A_HEAD_DIM:(g + 1) * A_HEAD_DIM]

    kc = k_ref[n_lat:n_lat + n_ctx, :]
    vc = v_ref[n_lat:n_lat + n_ctx, :]

    @pl.when(i < n_lat_tiles)
    def _latent():
        q0 = i * tq
        ws = pl.multiple_of(jnp.clip(q0 - WINDOW, 0, n_lat - win), WINDOW)
        kw = k_ref[pl.ds(ws, win), :]
        vw = v_ref[pl.ds(ws, win), :]
        kpos = ws + lax.broadcasted_iota(jnp.int32, (win, tq), 0)
        qpos = q0 + lax.broadcasted_iota(jnp.int32, (win, tq), 1)
        mask = jnp.abs(qpos - kpos) <= WINDOW
        keys = [jnp.concatenate([kv_head(kw, g), kv_head(kc, g)], axis=0) for g in range(A_KV_HEADS)]
        vals = [jnp.concatenate([kv_head(vw, g), kv_head(vc, g)], axis=0) for g in range(A_KV_HEADS)]
        run_tile(keys, vals, mask)

    @pl.when(i >= n_lat_tiles)
    def _context():
        run_tile([kv_head(kc, g) for g in range(A_KV_HEADS)], [kv_head(vc, g) for g in range(A_KV_HEADS)], None)


def _attn_a(qa, ka, va, sink_tab, *, n_lat, n_ctx, tq):
    t_all = qa.shape[0]
    kern = functools.partial(_attn_a_kernel, tq=tq, n_lat=n_lat, n_ctx=n_ctx)
    return pl.pallas_call(
        kern,
        grid=(t_all // tq,),
        in_specs=[pl.BlockSpec((tq, A_WIDTH), lambda i: (i, 0)), _const_spec(ka.shape), _const_spec(va.shape),
                  _const_spec(sink_tab.shape)],
        out_specs=pl.BlockSpec((tq, A_WIDTH), lambda i: (i, 0)),
        out_shape=jax.ShapeDtypeStruct((t_all, A_WIDTH), BF16),
        scratch_shapes=[pltpu.VMEM((4, tq + 2 * WINDOW + n_ctx, tq), F32)],
        compiler_params=_params(("arbitrary",)),
        name="attn_a",
    )(qa, ka, va, sink_tab)


MLA_STRIP = 256
MLA_UNITS_PER_TRIP = 16


def _col_fold(st, pair_op):
    rows = st.shape[0]
    groups = 4 if rows % 32 == 0 else 1
    part = st[0:rows // groups]
    for g in range(1, groups):
        part = pair_op(part, st[g * (rows // groups):(g + 1) * (rows // groups)])
    return part


def _col_max(st):
    return jnp.max(_col_fold(st, jnp.maximum), axis=0, keepdims=True)


def _col_sum(st):
    return jnp.sum(_col_fold(st, jnp.add), axis=0, keepdims=True)


def _mla_kernel(q_ref, k_ref, vt_ref, prev_hbm, o_ref, acc_ref, m_ref, s_ref, *, nk, tq):
    del prev_hbm
    j = pl.program_id(1)
    ns = tq // MLA_STRIP

    @pl.when(j == 0)
    def _init():
        m_ref[...] = jnp.full(m_ref.shape, NEG_INF, F32)
        acc_ref[...] = jnp.zeros(acc_ref.shape, F32)

    tk = k_ref.shape[1]
    k_half = tk // 2
    v_half = (tk // MLA_STRIP + 1) // 2 * MLA_STRIP if tk > MLA_STRIP else tk

    def scores(hh, c, slot):
        q = q_ref[hh, c * MLA_STRIP:(c + 1) * MLA_STRIP, :]
        for r0, r1 in ((0, k_half), (k_half, tk)):
            s_ref[slot, r0:r1, :] = lax.dot_general(k_ref[hh, r0:r1, :], q, NT_DIMS, preferred_element_type=F32)

    def softmax_pv(hh, c, slot):
        cs = slice(c * MLA_STRIP, (c + 1) * MLA_STRIP)
        st = s_ref[slot]
        m_old = m_ref[hh, :, cs]
        m_new = jnp.maximum(m_old, _col_max(st))
        alpha = jnp.exp2(m_old - m_new)
        p = jnp.exp2(st - m_new).astype(BF16)
        pv = jnp.dot(vt_ref[hh, :, 0:v_half], p[0:v_half], preferred_element_type=F32)
        if v_half < tk:
            pv = pv + jnp.dot(vt_ref[hh, :, v_half:tk], p[v_half:tk], preferred_element_type=F32)
        acc_ref[hh, :, cs] = alpha * acc_ref[hh, :, cs] + pv
        m_ref[hh, :, cs] = m_new

    pipelined = ns % 2 == 0

    def head(hh, carry):
        for c in range(ns):
            if not pipelined:
                scores(hh, c, 0)
            elif c + 1 < ns:
                scores(hh, c + 1, (c + 1) % 2)
            else:
                scores(jnp.minimum(hh + 1, B_HEADS - 1), 0, 0)
            softmax_pv(hh, c, c % 2 if pipelined else 0)
        return carry

    if pipelined:
        scores(0, 0, 0)
    lax.fori_loop(0, B_HEADS, head, 0, unroll=max(1, min(B_HEADS, MLA_UNITS_PER_TRIP // ns)))

    @pl.when(j == nk - 1)
    def _finish():
        for hh in range(B_HEADS):
            o = acc_ref[hh, 0:B_V_DIM, :] * (1.0 / acc_ref[hh, B_V_DIM:B_V_DIM + 1, :])
            o_ref[:, hh * B_V_DIM:(hh + 1) * B_V_DIM] = o.T.astype(o_ref.dtype)


def _mla(qb, kb, vbt, prev, *, q_row0, n_q, k_row0, n_k, tq, tk):
    t_all = qb.shape[1]
    nq, nk = n_q // tq, n_k // tk
    qo, ko = q_row0 // tq, k_row0 // tk
    kern = functools.partial(_mla_kernel, nk=nk, tq=tq)
    return pl.pallas_call(
        kern,
        grid=(nq, nk),
        in_specs=[pl.BlockSpec((B_HEADS, tq, B_QK_PAD), lambda i, j: (0, qo + i, 0)),
                  pl.BlockSpec((B_HEADS, tk, B_QK_PAD), lambda i, j: (0, ko + j, 0)),
                  pl.BlockSpec((B_HEADS, B_VT_ROWS, tk), lambda i, j: (0, 0, ko + j)),
                  pl.BlockSpec(memory_space=pl.ANY)],
        out_specs=pl.BlockSpec((tq, B_WIDTH), lambda i, j: (qo + i, 0)),
        out_shape=jax.ShapeDtypeStruct((t_all, B_WIDTH), BF16),
        scratch_shapes=[pltpu.VMEM((B_HEADS, B_VT_ROWS, tq), F32), pltpu.VMEM((B_HEADS, 1, tq), F32),
                        pltpu.VMEM((2, tk, MLA_STRIP), F32)],
        input_output_aliases={3: 0},
        compiler_params=_params(("arbitrary", "arbitrary")),
        name="mla_attn",
    )(qb, kb, vbt, prev)


def _out_kernel(oa_ref, ob_ref, x_ref, mod_ref, ga_ref, gb_ref, wo_ref, pg_ref, o_ref, *, tm, n_lat):
    d = x_ref.shape[-1]
    is_ctx = _is_ctx_rows(pl.program_id(0), tm, n_lat)
    a = (_rms(oa_ref[...].astype(F32)) * ga_ref[...]).astype(BF16)
    b = (_rms(ob_ref[...].astype(F32)) * gb_ref[...]).astype(BF16)
    y = (jnp.dot(a, wo_ref[0:A_WIDTH, :], preferred_element_type=F32)
         + jnp.dot(b, wo_ref[A_WIDTH:A_WIDTH + B_WIDTH, :], preferred_element_type=F32))
    o_ref[...] = x_ref[...] + _mod_select(mod_ref, 2, d, is_ctx) * (_rms(y) * pg_ref[...])


def _attn_out(oa, ob, xs, mod, ga, gb, wo, pg, *, n_lat, tm):
    t_all, d = xs.shape
    row = lambda i: (i, 0)
    kern = functools.partial(_out_kernel, tm=tm, n_lat=n_lat)
    return pl.pallas_call(
        kern,
        grid=(t_all // tm,),
        in_specs=[pl.BlockSpec((tm, A_WIDTH), row), pl.BlockSpec((tm, B_WIDTH), row), pl.BlockSpec((tm, d), row),
                  _const_spec(mod.shape), _const_spec(ga.shape), _const_spec(gb.shape), _const_spec(wo.shape),
                  _const_spec(pg.shape)],
        out_specs=pl.BlockSpec((tm, d), row),
        out_shape=jax.ShapeDtypeStruct((t_all, d), F32),
        compiler_params=_params(("arbitrary",)),
        name="attn_out_mix",
    )(oa, ob, xs, mod, ga, gb, wo, pg)


def _swiglu_step(build_h, w1_ref, w3_ref, w2_ref, o_ref, h_ref, alongside=None):
    j = pl.program_id(1)

    @pl.when(j == 0)
    def _prologue():
        build_h()
        o_ref[...] = jnp.zeros(o_ref.shape, o_ref.dtype)

    if alongside is not None:
        alongside()
    h = h_ref[...]
    a = jnp.dot(h, w1_ref[...].astype(BF16), preferred_element_type=F32)
    b = jnp.dot(h, w3_ref[...].astype(BF16), preferred_element_type=F32)
    u = (a * (1.0 / (1.0 + jnp.exp(-a))) * b).astype(BF16)
    o_ref[...] += jnp.dot(u, w2_ref[...].astype(BF16), preferred_element_type=F32)


def _ffn_dense_kernel(x_ref, mod_ref, g_ref, pg_ref, w1_ref, w3_ref, w2_ref, o_ref, h_ref, gs_ref, acc_ref, *, tm,
                      n_lat, nj):
    tile_row0 = pl.program_id(0) * tm
    build_h = functools.partial(_norm_mod_rows, x_ref, g_ref, mod_ref, 3, tile_row0, n_lat, h_ref, gs_ref)
    _swiglu_step(build_h, w1_ref, w3_ref, w2_ref, acc_ref, h_ref)

    @pl.when(pl.program_id(1) == nj - 1)
    def _epilogue():
        _resid_rows(o_ref, acc_ref, x_ref, pg_ref, mod_ref, 5, tile_row0, n_lat, gs_ref)


def _ffn_dense(xs, mod, g, pg, w1, w3, w2, *, n_lat, tm, tf):
    t_all, d = xs.shape
    dff = w1.shape[1]
    nj = dff // tf
    kern = functools.partial(_ffn_dense_kernel, tm=tm, n_lat=n_lat, nj=nj)
    return pl.pallas_call(
        kern,
        grid=(t_all // tm, nj),
        in_specs=[pl.BlockSpec((tm, d), lambda i, j: (i, 0)), _const_spec(mod.shape), _const_spec(g.shape),
                  _const_spec(pg.shape),
                  pl.BlockSpec((d, tf), lambda i, j: (0, j)), pl.BlockSpec((d, tf), lambda i, j: (0, j)),
                  pl.BlockSpec((tf, d), lambda i, j: (j, 0))],
        out_specs=pl.BlockSpec((tm, d), lambda i, j: (i, 0)),
        out_shape=jax.ShapeDtypeStruct((t_all, d), F32),
        scratch_shapes=[pltpu.VMEM((tm, d), BF16), pltpu.VMEM((8, d), F32), pltpu.VMEM((tm, d), F32)],
        compiler_params=_params(("arbitrary", "arbitrary")),
        name="ffn_dense",
    )(xs, mod, g, pg, w1, w3, w2)


def _ffn_moe_kernel(te_ref, nu_ref, cur_ref, nxt_ref, x_hbm, w1_ref, w3_ref, w2_ref, o_ref, h_ref, xbuf, sem, *, tm,
                    nj):
    t, j = pl.program_id(0), pl.program_id(1)
    n_used = nu_ref[0]
    used = t < n_used
    slot = t % 2
    rows_per_step = tm // nj

    def row_copy(src_ref, r, s):
        return pltpu.make_async_copy(x_hbm.at[pl.ds(src_ref[0, 0, r], 1)], xbuf.at[s, pl.ds(r, 1)], sem.at[s])

    @pl.when((t == 0) & (j == 0))
    def _first_tile():
        lax.fori_loop(0, tm, lambda r, c: (row_copy(cur_ref, r, 0).start(), c)[1], 0, unroll=DMA_UNROLL)

    @pl.when((t <= n_used) & (j == 0))
    def _rows_arrived():
        lax.fori_loop(0, tm, lambda r, c: (row_copy(cur_ref, r, slot).wait(), c)[1], 0, unroll=DMA_UNROLL)

    @pl.when(used)
    def _used():
        def request_next_tile():
            for k in range(rows_per_step):
                row_copy(nxt_ref, j * rows_per_step + k, 1 - slot).start()

        def build_h():
            half = h_ref.shape[1] // 2
            lo, hi = _unpack_bf16_halves(xbuf[slot])
            h_ref[:, 0:half] = lo
            h_ref[:, half:2 * half] = hi

        _swiglu_step(build_h, w1_ref, w3_ref, w2_ref, o_ref, h_ref, alongside=request_next_tile)

    @pl.when(jnp.logical_not(used) & (j == 0))
    def _idle():
        o_ref[...] = jnp.zeros(o_ref.shape, o_ref.dtype)


def _ffn_moe(hp, slot_src, tile_expert, n_used, w1, w3, w2, *, tm, tf):
    d = 2 * hp.shape[1]
    n_tiles = slot_src.shape[0]
    dff = w1.shape[2]
    nj = dff // tf
    assert tm % nj == 0, "the next tile's row requests are spread evenly over the d_ff chunks"

    def t_eff(t, nu):
        return jnp.minimum(t, nu[0] - 1)

    def j_eff(t, j, nu):
        return jnp.where(t < nu[0], j, nj - 1)

    smem_rows = lambda fn: pl.BlockSpec((1, 1, tm), fn, memory_space=pltpu.SMEM)
    grid_spec = pltpu.PrefetchScalarGridSpec(
        num_scalar_prefetch=2,
        grid=(n_tiles, nj),
        in_specs=[smem_rows(lambda t, j, te, nu: (t, 0, 0)),
                  smem_rows(lambda t, j, te, nu: (jnp.minimum(t + 1, n_tiles - 1), 0, 0)),
                  pl.BlockSpec(memory_space=pl.ANY),
                  pl.BlockSpec((None, d, tf), lambda t, j, te, nu: (te[t_eff(t, nu)], 0, j_eff(t, j, nu))),
                  pl.BlockSpec((None, d, tf), lambda t, j, te, nu: (te[t_eff(t, nu)], 0, j_eff(t, j, nu))),
                  pl.BlockSpec((None, tf, d), lambda t, j, te, nu: (te[t_eff(t, nu)], j_eff(t, j, nu), 0))],
        out_specs=pl.BlockSpec((tm, d), lambda t, j, te, nu: (t, 0)),
        scratch_shapes=[pltpu.VMEM((tm, d), BF16), pltpu.VMEM((2, tm, d // 2), jnp.uint32),
                        pltpu.SemaphoreType.DMA((2,))],
    )
    return pl.pallas_call(
        functools.partial(_ffn_moe_kernel, tm=tm, nj=nj),
        grid_spec=grid_spec,
        out_shape=jax.ShapeDtypeStruct((n_tiles * tm, d), F32),
        compiler_params=_params(("arbitrary", "arbitrary")),
        name="ffn_moe",
    )(tile_expert, n_used, slot_src, slot_src, hp, w1, w3, w2)


def _pack_bf16_halves(h):
    half = h.shape[1] // 2
    bits = lax.bitcast_convert_type(h.astype(BF16).astype(F32), jnp.uint32)
    return (bits[:, 0:half] >> 16) | bits[:, half:2 * half]


def _unpack_bf16_halves(u):
    lo = lax.bitcast_convert_type(u << 16, F32).astype(BF16)
    hi = lax.bitcast_convert_type(u & jnp.uint32(0xFFFF0000), F32).astype(BF16)
    return lo, hi


def _bf16_part(x):
    return lax.bitcast_convert_type(lax.bitcast_convert_type(x, jnp.uint32) & jnp.uint32(0xFFFF0000), F32)


def _route_kernel(x_ref, mod_ref, g_ref, rw_ref, o_ref, hp_ref):
    d = x_ref.shape[-1]
    h = _rms(x_ref[...]) * g_ref[...] * (1.0 + mod_ref[0:1, 4 * d:5 * d]) + mod_ref[0:1, 3 * d:4 * d]
    hp_ref[...] = _pack_bf16_halves(h)
    h_top = _bf16_part(h)
    h_hi, h_lo = h_top.astype(BF16), (h - h_top).astype(BF16)
    hi = jnp.dot(h_hi, rw_ref[0], preferred_element_type=F32)
    logits = hi + pltpu.roll(hi, LANE - N_EXPERTS, 1) + jnp.dot(h_lo, rw_ref[1], preferred_element_type=F32)
    lane = lax.broadcasted_iota(jnp.int32, logits.shape, 1)
    lg = jnp.where(lane < N_EXPERTS, logits, -jnp.inf)
    m1 = jnp.max(lg, axis=-1, keepdims=True)
    i1 = jnp.min(jnp.where(lg == m1, lane, LANE), axis=-1, keepdims=True)
    lg2 = jnp.where(lane == i1, -jnp.inf, lg)
    m2 = jnp.max(lg2, axis=-1, keepdims=True)
    i2 = jnp.min(jnp.where(lg2 == m2, lane, LANE), axis=-1, keepdims=True)
    e = jnp.exp(m2 - m1)
    w1 = 1.0 / (1.0 + e)
    w2 = e * w1
    o_ref[...] = jnp.where(lane == 0, i1.astype(F32),
                           jnp.where(lane == 1, i2.astype(F32), jnp.where(lane == 2, w1, jnp.where(lane == 3, w2, 0.0))))


def _route(xs, mod, g, rw_pad, *, n_lat, tm):
    d = xs.shape[1]
    return pl.pallas_call(
        _route_kernel,
        grid=(n_lat // tm,),
        in_specs=[pl.BlockSpec((tm, d), lambda i: (i, 0)), _const_spec(mod.shape), _const_spec(g.shape),
                  _const_spec(rw_pad.shape)],
        out_specs=[pl.BlockSpec((tm, LANE), lambda i: (i, 0)), pl.BlockSpec((tm, d // 2), lambda i: (i, 0))],
        out_shape=[jax.ShapeDtypeStruct((n_lat, LANE), F32), jax.ShapeDtypeStruct((n_lat, d // 2), jnp.uint32)],
        compiler_params=_params(("arbitrary",)),
        name="moe_route",
    )(xs, mod, g, rw_pad)


def _combine_kernel(cur_ref, nxt_ref, rt_ref, x_ref, mod_ref, pg_ref, y_hbm, o_ref, buf, sem, *, tm, n_steps):
    i = pl.program_id(0)
    slot = i % 2

    def row_copy(dest_ref, r, k, s):
        return pltpu.make_async_copy(y_hbm.at[pl.ds(dest_ref[0, k, r], 1)], buf.at[s, k, pl.ds(r, 1)], sem.at[s])

    def start_tile0(r, c):
        row_copy(cur_ref, r, 0, 0).start()
        row_copy(cur_ref, r, 1, 0).start()
        return c

    def wait(r, c):
        row_copy(cur_ref, r, 0, slot).wait()
        row_copy(cur_ref, r, 1, slot).wait()
        return c

    @pl.when(i == 0)
    def _first_tile():
        lax.fori_loop(0, tm, start_tile0, 0, unroll=DMA_UNROLL)

    lax.fori_loop(0, tm, wait, 0, unroll=DMA_UNROLL)

    @pl.when(i < n_steps)
    def _tile():
        for r in range(tm):
            row_copy(nxt_ref, r, 0, 1 - slot).start()
            row_copy(nxt_ref, r, 1, 1 - slot).start()
        d = x_ref.shape[-1]
        rt = rt_ref[...]
        y = rt[:, 2:3] * buf[slot, 0] + rt[:, 3:4] * buf[slot, 1]
        o_ref[...] = x_ref[...] + mod_ref[0:1, 5 * d:6 * d] * (_rms(y) * pg_ref[...])


def _combine(yg, dest, route, xs, mod, pg, *, n_lat, tm):
    d = xs.shape[1]
    n_steps = n_lat // tm
    kern = functools.partial(_combine_kernel, tm=tm, n_steps=n_steps)
    tile = lambda i: jnp.minimum(i, n_steps - 1)
    nxt_tile = lambda i: jnp.minimum(i + 1, n_steps - 1)
    return pl.pallas_call(
        kern,
        grid=(n_steps + 1,),
        in_specs=[pl.BlockSpec((1, 2, tm), lambda i: (tile(i), 0, 0), memory_space=pltpu.SMEM),
                  pl.BlockSpec((1, 2, tm), lambda i: (nxt_tile(i), 0, 0), memory_space=pltpu.SMEM),
                  pl.BlockSpec((tm, LANE), lambda i: (tile(i), 0)), pl.BlockSpec((tm, d), lambda i: (tile(i), 0)),
                  _const_spec(mod.shape), _const_spec(pg.shape), pl.BlockSpec(memory_space=pl.ANY)],
        out_specs=pl.BlockSpec((tm, d), lambda i: (tile(i), 0)),
        out_shape=jax.ShapeDtypeStruct((n_lat, d), F32),
        scratch_shapes=[pltpu.VMEM((2, 2, tm, d), F32), pltpu.SemaphoreType.DMA((2,))],
        compiler_params=_params(("arbitrary",)),
        name="moe_combine",
    )(dest, dest, route, xs, mod, pg, yg)


def _rope_tables(n_lat, n_ctx):
    rows = n_lat // GRID_W

    def tabs(rot_dim):
        axis_dim = rot_dim // 2
        quarter = rot_dim // 4
        inv_freq = ROPE_BASE ** (-jnp.arange(0, axis_dim, 2, dtype=F32) / axis_dim)
        ang_r = jnp.arange(rows, dtype=F32)[:, None] * inv_freq[None, :]
        ang_c = jnp.arange(GRID_W, dtype=F32)[:, None] * inv_freq[None, :]

        def over_grid(fn):
            r = jnp.broadcast_to(fn(ang_r)[:, None, :], (rows, GRID_W, quarter))
            c = jnp.broadcast_to(fn(ang_c)[None, :, :], (rows, GRID_W, quarter))
            return jnp.concatenate([r, r, c, c], axis=-1).reshape(n_lat, rot_dim)

        cos, sin = over_grid(jnp.cos), over_grid(jnp.sin)
        lower = (jnp.arange(rot_dim) % (2 * quarter)) < quarter
        sin_m = jnp.where(lower[None, :], -sin, 0.0)
        sin_p = jnp.where(lower[None, :], 0.0, sin)
        pad = LANE - rot_dim
        cos = jnp.pad(cos, ((0, n_ctx), (0, pad)), constant_values=1.0)
        sin_m = jnp.pad(sin_m, ((0, n_ctx), (0, pad)))
        sin_p = jnp.pad(sin_p, ((0, n_ctx), (0, pad)))
        return [cos, sin_m, sin_p]

    return jnp.stack(tabs(A_HEAD_DIM) + tabs(B_ROPE_DIM))


def _layer_weights(i, w_in, w_uq, w_ukv, w_o):
    d = w_in.shape[1]
    q_w = A_WIDTH + B_Q_RANK
    wi = w_in[i]
    qa, cq = wi[:, :A_WIDTH], wi[:, A_WIDTH:q_w]
    o1, o2, o3 = q_w + A_KV_WIDTH, q_w + 2 * A_KV_WIDTH, q_w + 2 * A_KV_WIDTH + B_KV_RANK
    ka, va, ckv, kr = wi[:, q_w:o1], wi[:, o1:o2], wi[:, o2:o3], wi[:, o3:]
    win = jnp.concatenate([qa, ka, va, cq, ckv, kr, jnp.zeros((d, LANE - B_ROPE_DIM), F32)], axis=1).astype(BF16)
    uq = w_uq[i].reshape(B_Q_RANK, B_HEADS, B_NOPE_DIM + B_ROPE_DIM)
    uq = jnp.pad(uq, ((0, 0), (0, 0), (0, B_QK_PAD - B_NOPE_DIM - B_ROPE_DIM)))
    wuq = uq.reshape(B_Q_RANK, B_HEADS * B_QK_PAD).astype(BF16)
    ukv = w_ukv[i].reshape(B_KV_RANK, B_HEADS, B_NOPE_DIM + B_V_DIM)
    wukn = ukv[:, :, :B_NOPE_DIM].reshape(B_KV_RANK, B_HEADS * B_NOPE_DIM).astype(BF16)
    wuvt = ukv[:, :, B_NOPE_DIM:].reshape(B_KV_RANK, B_HEADS * B_V_DIM).T.astype(BF16)
    return win, wuq, wukn, wuvt, w_o[i].astype(BF16)


def _routing_slots(route, n_lat, tm_e, n_tiles):
    e12 = route[:, 0:2].astype(jnp.int32).T.reshape(-1)
    onehot = (e12[:, None] == jnp.arange(N_EXPERTS)[None, :]).astype(jnp.int32)
    csum = jnp.cumsum(onehot, axis=0)
    rank = jnp.sum(csum * onehot, axis=1) - 1
    counts = csum[-1]
    tiles_per = (counts + tm_e - 1) // tm_e
    tile_end = jnp.cumsum(tiles_per)
    start = (tile_end - tiles_per) * tm_e
    dest = (jnp.sum(start[None, :] * onehot, axis=1) + rank).astype(jnp.int32)
    n_used = tile_end[-1].astype(jnp.int32).reshape(1)
    t_idx = jnp.arange(n_tiles, dtype=jnp.int32)
    tile_expert = jnp.minimum(jnp.sum((t_idx[:, None] >= tile_end[None, :]).astype(jnp.int32), axis=1),
                              N_EXPERTS - 1).astype(jnp.int32)
    token = jnp.tile(jnp.arange(n_lat, dtype=jnp.int32), 2)
    slot_src = jnp.zeros((n_tiles * tm_e,), jnp.int32).at[dest].set(token, unique_indices=True)
    return dest.reshape(2, n_lat), slot_src.reshape(n_tiles, 1, tm_e), tile_expert, n_used


def _tile(n, candidates):
    for c in candidates:
        if n % c == 0:
            return c
    raise ValueError(f"no supported tile size for {n} rows")


def kernel(x, c, ctx, c_ctx, w_ada, b_ada, pre_attn_g, post_attn_g, pre_ffn_g, post_ffn_g, w_in, attn_sink, q_norm_g,
           kv_norm_g, w_uq, w_ukv, grp_a_g, grp_b_g, w_o, ffn_w1, ffn_w3, ffn_w2, router_w, moe_w1, moe_w3, moe_w2):
    b, n_lat, d = x.shape
    n_ctx = ctx.shape[1]
    depth = w_ada.shape[0]
    assert b == 1 and c.shape[0] == 1, "single-sequence kernel"
    t_all = n_lat + n_ctx
    tm = _tile(t_all, (640, 256, 128))
    tq_a = 256
    tq_b = _tile(n_lat, (2048, 1024, 512, 256))
    tk_b = _tile(t_all, (1280, 256))
    tf = 512
    tm_r = _tile(n_lat, (512, 256))
    tm_c = 256
    assert n_ctx == tq_a and n_lat % tq_a == 0 and n_lat >= tq_a + 2 * WINDOW

    xs = jnp.concatenate([x[0], ctx[0]], axis=0)
    cc = jnp.stack([c[0], c_ctx])
    mods = _ada(cc, w_ada, b_ada)
    rope_tab = _rope_tables(n_lat, n_ctx)
    row2 = lambda v: v.reshape(1, -1)

    for i in range(depth):
        last = i == depth - 1
        mod = mods[i]
        win, wuq, wukn, wuvt, wo = _layer_weights(i, w_in, w_uq, w_ukv, w_o)
        qa, ka, va, qb, kb, vbt = _proj(xs, mod, row2(pre_attn_g[i]), win, row2(q_norm_g[i]), row2(kv_norm_g[i]),
                                        wuq, wukn, wuvt, rope_tab, n_lat=n_lat, tm=tm)
        sink_tab = jnp.broadcast_to((attn_sink[i] * LOG2E)[:, None], (A_HEADS, tq_a))
        oa = _attn_a(qa, ka, va, sink_tab, n_lat=n_lat, n_ctx=n_ctx, tq=tq_a)
        ob = _mla(qb, kb, vbt, jnp.zeros((t_all, B_WIDTH), BF16), q_row0=0, n_q=n_lat, k_row0=0, n_k=t_all,
                  tq=tq_b, tk=tk_b)
        ob = _mla(qb, kb, vbt, ob, q_row0=n_lat, n_q=n_ctx, k_row0=n_lat, n_k=n_ctx, tq=n_ctx, tk=n_ctx)
        xs = _attn_out(oa, ob, xs, mod, row2(grp_a_g[i]), row2(grp_b_g[i]), wo, row2(post_attn_g[i]),
                       n_lat=n_lat, tm=tm)
        jj = i // 2
        if i % 2 == 0:
            xs = _ffn_dense(xs, mod, row2(pre_ffn_g[i]), row2(post_ffn_g[i]), ffn_w1[jj].astype(BF16),
                            ffn_w3[jj].astype(BF16), ffn_w2[jj].astype(BF16), n_lat=n_lat, tm=tm, tf=tf)
        else:
            n_rows = n_lat
            assert last, "expert layers other than the last would also need the context rows routed"
            nj = moe_w1.shape[-1] // tf
            tm_e = -(-tm // (nj * ROW_CHUNK)) * ROW_CHUNK * nj
            n_tiles = (2 * n_rows + N_EXPERTS * (tm_e - 1)) // tm_e + 1
            rw_top = _bf16_part(router_w[jj])
            rw_hi, rw_lo = rw_top.astype(BF16), (router_w[jj] - rw_top).astype(BF16)
            zeros = jnp.zeros((d, LANE - 2 * N_EXPERTS), BF16)
            rw_pad = jnp.stack([jnp.concatenate([rw_hi, rw_lo, zeros], axis=1),
                                jnp.concatenate([rw_hi, jnp.zeros_like(rw_lo), zeros], axis=1)])
            route, hp = _route(xs, mod, row2(pre_ffn_g[i]), rw_pad, n_lat=n_rows, tm=tm_r)
            dest, slot_src, tile_expert, n_used = _routing_slots(route, n_rows, tm_e, n_tiles)
            dest_t = dest.reshape(2, n_rows // tm_c, tm_c).transpose(1, 0, 2)
            yg = _ffn_moe(hp, slot_src, tile_expert, n_used, moe_w1[jj], moe_w3[jj], moe_w2[jj], tm=tm_e, tf=tf)
            xs = _combine(yg, dest_t, route, xs, mod, row2(post_ffn_g[i]), n_lat=n_rows, tm=tm_c)
    return xs[:n_lat].reshape(b, n_lat, d)
```

```python
import functools
import math

import jax
import jax.numpy as jnp
from jax import lax
from jax.experimental import pallas as pl
from jax.experimental.pallas import tpu as pltpu

GRID_W = 64
EPS = 1e-6
NEG_INF = -1e30
ROPE_BASE = 10000.0
A_HEADS = 8
A_KV_HEADS = 2
A_HEAD_DIM = 128
WINDOW = 128
B_HEADS = 8
B_Q_RANK = 384
B_KV_RANK = 256
B_NOPE_DIM = 128
B_ROPE_DIM = 64
B_V_DIM = 128
N_EXPERTS = 8
A_WIDTH = A_HEADS * A_HEAD_DIM
A_KV_WIDTH = A_KV_HEADS * A_HEAD_DIM
B_WIDTH = B_HEADS * B_V_DIM
B_QK_PAD = 256
B_VT_ROWS = B_V_DIM + 16

LOG2E = math.log2(math.e)
LANE = 128
VMEM_LIMIT_BYTES = 60 * 1024 * 1024

F32 = jnp.float32
BF16 = jnp.bfloat16
NT_DIMS = (((1,), (1,)), ((), ()))


def _params(sem, vmem=VMEM_LIMIT_BYTES):
    return pltpu.CompilerParams(dimension_semantics=sem, vmem_limit_bytes=vmem)


def _const_spec(shape):
    nd = len(shape)
    return pl.BlockSpec(shape, lambda *_: (0,) * nd, pipeline_mode=pl.Buffered(1))


def _rms(x):
    return x * lax.rsqrt(jnp.mean(x * x, axis=-1, keepdims=True) + EPS)


DMA_UNROLL = 8
ROW_CHUNK = 16
ROW_UNROLL = 8


def _row_chunks(tm, body):
    def step(r, carry):
        body(pl.ds(pl.multiple_of(r * ROW_CHUNK, ROW_CHUNK), ROW_CHUNK), r * ROW_CHUNK)
        return carry

    lax.fori_loop(0, tm // ROW_CHUNK, step, 0, unroll=ROW_UNROLL)


def _mod_row(tile_row0, r0, n_lat):
    return jnp.where(tile_row0 + r0 >= n_lat, 1, 0)


def _norm_mod_rows(x_ref, g_ref, mod_ref, k_shift, tile_row0, n_lat, h_ref, gs_ref):
    tm, d = x_ref.shape
    gs_ref[...] = g_ref[...] * (1.0 + mod_ref[:, (k_shift + 1) * d:(k_shift + 2) * d])

    def body(rows, r0):
        mrow = _mod_row(tile_row0, r0, n_lat)
        h = _rms(x_ref[rows, :]) * gs_ref[pl.ds(mrow, 1), :]
        h_ref[rows, :] = (h + mod_ref[pl.ds(mrow, 1), k_shift * d:(k_shift + 1) * d]).astype(BF16)

    _row_chunks(tm, body)


def _resid_rows(o_ref, y_ref, x_ref, pg_ref, mod_ref, k_gate, tile_row0, n_lat, gs_ref):
    tm, d = x_ref.shape
    gs_ref[...] = pg_ref[...] * mod_ref[:, k_gate * d:(k_gate + 1) * d]

    def body(rows, r0):
        mrow = _mod_row(tile_row0, r0, n_lat)
        o_ref[rows, :] = x_ref[rows, :] + gs_ref[pl.ds(mrow, 1), :] * _rms(y_ref[rows, :])

    _row_chunks(tm, body)


def _is_ctx_rows(tile_index, tm, n_lat):
    row = tile_index * tm + lax.broadcasted_iota(jnp.int32, (tm, 1), 0)
    return row >= n_lat


def _mod_select(mod_ref, k, d, is_ctx):
    return jnp.where(is_ctx, mod_ref[1:2, k * d:(k + 1) * d], mod_ref[0:1, k * d:(k + 1) * d])


ADA_ROWS = 2
ADA_K_CHUNK = 128


def _ada_kernel(c_ref, w_ref, b_ref, o_ref):
    d, tn = w_ref.shape
    o_ref[...] = jnp.zeros(o_ref.shape, o_ref.dtype)
    for r in range(ADA_ROWS):
        part = [jnp.zeros((8, LANE), F32) for _ in range(tn // LANE)]
        for k0 in range(0, d, ADA_K_CHUNK):
            c = c_ref[r, k0:k0 + ADA_K_CHUNK, :]
            a = c * (1.0 / (1.0 + jnp.exp(-c)))
            for l in range(tn // LANE):
                prod = a * w_ref[k0:k0 + ADA_K_CHUNK, l * LANE:(l + 1) * LANE]
                part[l] = part[l] + jnp.sum(prod.reshape(ADA_K_CHUNK // 8, 8, LANE), axis=0)
        for l in range(tn // LANE):
            cols = slice(l * LANE, (l + 1) * LANE)
            o_ref[r:r + 1, cols] = jnp.sum(part[l], axis=0, keepdims=True) + b_ref[:, cols]


def _ada(cc, w_ada, b_ada):
    depth, d, n = w_ada.shape
    tn = n // 8
    c_lanes = jnp.broadcast_to(cc[:, :, None], (ADA_ROWS, d, LANE))
    return pl.pallas_call(
        _ada_kernel,
        grid=(depth, n // tn),
        in_specs=[pl.BlockSpec((ADA_ROWS, d, LANE), lambda l, j: (0, 0, 0)),
                  pl.BlockSpec((None, d, tn), lambda l, j: (l, 0, j)),
                  pl.BlockSpec((None, 1, tn), lambda l, j: (l, 0, j))],
        out_specs=pl.BlockSpec((None, 8, tn), lambda l, j: (l, 0, j)),
        out_shape=jax.ShapeDtypeStruct((depth, 8, n), F32),
        compiler_params=_params(("arbitrary", "arbitrary")),
        name="ada_mod",
    )(c_lanes, w_ada, b_ada.reshape(depth, 1, n))


def _rope(t, cos, sin_m, sin_p, quarter):
    n = t.shape[-1]
    return t * cos + pltpu.roll(t, n - quarter, 1) * sin_m + pltpu.roll(t, quarter, 1) * sin_p


def _proj_kernel(x_ref, mod_ref, g_ref, win_ref, qg_ref, kvg_ref, wuq_ref, wukn_ref, wuvt_ref, rope_ref,
                 qa_ref, ka_ref, va_ref, qb_ref, kb_ref, vbt_ref, *, tm, n_lat, scale_a, scale_b):
    d = x_ref.shape[-1]
    is_ctx = _is_ctx_rows(pl.program_id(0), tm, n_lat)
    h = _rms(x_ref[...]) * g_ref[...] * (1.0 + _mod_select(mod_ref, 1, d, is_ctx)) + _mod_select(mod_ref, 0, d, is_ctx)
    proj = jnp.dot(h.astype(BF16), win_ref[...], preferred_element_type=F32)
    cos_a, sinm_a, sinp_a = rope_ref[0], rope_ref[1], rope_ref[2]
    cos_b, sinm_b, sinp_b = rope_ref[3], rope_ref[4], rope_ref[5]
    qa_q = A_HEAD_DIM // 4
    b_q = B_ROPE_DIM // 4
    cos_as, sinm_as, sinp_as = cos_a * scale_a, sinm_a * scale_a, sinp_a * scale_a
    for hh in range(A_HEADS):
        t = proj[:, hh * LANE:(hh + 1) * LANE]
        qa_ref[:, hh * LANE:(hh + 1) * LANE] = _rope(t, cos_as, sinm_as, sinp_as, qa_q).astype(BF16)
    o = A_WIDTH
    for hh in range(A_KV_HEADS):
        t = proj[:, o + hh * LANE:o + (hh + 1) * LANE]
        ka_ref[:, hh * LANE:(hh + 1) * LANE] = _rope(t, cos_a, sinm_a, sinp_a, qa_q).astype(BF16)
    o += A_KV_WIDTH
    va_ref[...] = proj[:, o:o + A_KV_WIDTH].astype(BF16)
    o += A_KV_WIDTH
    cq = (_rms(proj[:, o:o + B_Q_RANK]) * qg_ref[...]).astype(BF16)
    o += B_Q_RANK
    qm = jnp.dot(cq, wuq_ref[...], preferred_element_type=F32)
    cos_bs, sinm_bs, sinp_bs = cos_b * scale_b, sinm_b * scale_b, sinp_b * scale_b
    for hh in range(B_HEADS):
        b0 = hh * B_QK_PAD
        qb_ref[hh, :, 0:LANE] = (qm[:, b0:b0 + LANE] * scale_b).astype(BF16)
        qb_ref[hh, :, LANE:2 * LANE] = _rope(qm[:, b0 + LANE:b0 + 2 * LANE], cos_bs, sinm_bs, sinp_bs, b_q).astype(BF16)
    ckv = (_rms(proj[:, o:o + B_KV_RANK]) * kvg_ref[...]).astype(BF16)
    o += B_KV_RANK
    kr = _rope(proj[:, o:o + LANE], cos_b, sinm_b, sinp_b, b_q).astype(BF16)
    kn = jnp.dot(ckv, wukn_ref[...], preferred_element_type=F32)
    vt = lax.dot_general(wuvt_ref[...], ckv, NT_DIMS, preferred_element_type=F32)
    for hh in range(B_HEADS):
        kb_ref[hh, :, 0:LANE] = kn[:, hh * LANE:(hh + 1) * LANE].astype(BF16)
        kb_ref[hh, :, LANE:2 * LANE] = kr
        vbt_ref[hh, 0:B_V_DIM, :] = vt[hh * B_V_DIM:(hh + 1) * B_V_DIM, :].astype(BF16)
        vbt_ref[hh, B_V_DIM:B_VT_ROWS, :] = jnp.ones((B_VT_ROWS - B_V_DIM, tm), BF16)


def _proj(xs, mod, g, win, qg, kvg, wuq, wukn, wuvt, rope_tab, *, n_lat, tm):
    t_all, d = xs.shape
    row = lambda i: (i, 0)
    kern = functools.partial(_proj_kernel, tm=tm, n_lat=n_lat, scale_a=LOG2E / math.sqrt(A_HEAD_DIM),
                             scale_b=LOG2E / math.sqrt(B_NOPE_DIM + B_ROPE_DIM))
    return pl.pallas_call(
        kern,
        grid=(t_all // tm,),
        in_specs=[pl.BlockSpec((tm, d), row), _const_spec(mod.shape), _const_spec(g.shape), _const_spec(win.shape),
                  _const_spec(qg.shape), _const_spec(kvg.shape), _const_spec(wuq.shape), _const_spec(wukn.shape),
                  _const_spec(wuvt.shape), pl.BlockSpec((6, tm, LANE), lambda i: (0, i, 0))],
        out_specs=[pl.BlockSpec((tm, A_WIDTH), row), pl.BlockSpec((tm, A_KV_WIDTH), row),
                   pl.BlockSpec((tm, A_KV_WIDTH), row),
                   pl.BlockSpec((B_HEADS, tm, B_QK_PAD), lambda i: (0, i, 0)),
                   pl.BlockSpec((B_HEADS, tm, B_QK_PAD), lambda i: (0, i, 0)),
                   pl.BlockSpec((B_HEADS, B_VT_ROWS, tm), lambda i: (0, 0, i))],
        out_shape=[jax.ShapeDtypeStruct((t_all, A_WIDTH), BF16), jax.ShapeDtypeStruct((t_all, A_KV_WIDTH), BF16),
                   jax.ShapeDtypeStruct((t_all, A_KV_WIDTH), BF16),
                   jax.ShapeDtypeStruct((B_HEADS, t_all, B_QK_PAD), BF16),
                   jax.ShapeDtypeStruct((B_HEADS, t_all, B_QK_PAD), BF16),
                   jax.ShapeDtypeStruct((B_HEADS, B_VT_ROWS, t_all), BF16)],
        compiler_params=_params(("arbitrary",)),
        name="pre_attn_proj",
    )(xs, mod, g, win, qg, kvg, wuq, wukn, wuvt, rope_tab)


def _attn_a_kernel(q_ref, k_ref, v_ref, sink_ref, o_ref, s_ref, *, tq, n_lat, n_ctx):
    i = pl.program_id(0)
    n_lat_tiles = n_lat // tq
    win = tq + 2 * WINDOW
    grp = A_HEADS // A_KV_HEADS
    tn_dims = (((0,), (0,)), ((), ()))

    def run_tile(keys, vals, mask):
        n_keys = keys[0].shape[0]

        def scores(hh, slot):
            q = q_ref[:, hh * A_HEAD_DIM:(hh + 1) * A_HEAD_DIM]
            s_ref[slot, 0:n_keys, :] = lax.dot_general(keys[hh // grp], q, NT_DIMS, preferred_element_type=F32)

        def head_out(hh, slot):
            s = s_ref[slot, 0:n_keys, :]
            if mask is not None:
                s = jnp.concatenate([jnp.where(mask, s[0:win], NEG_INF), s[win:n_keys]], axis=0)
            sink = sink_ref[hh:hh + 1, :]
            m = jnp.maximum(_col_max(s), sink)
            p = jnp.exp2(s - m)
            den = _col_sum(p) + jnp.exp2(sink - m)
            o = lax.dot_general(vals[hh // grp], p.astype(BF16), tn_dims, preferred_element_type=F32)
            o_ref[:, hh * A_HEAD_DIM:(hh + 1) * A_HEAD_DIM] = (o * (1.0 / den)).T.astype(o_ref.dtype)

        ahead = s_ref.shape[0] - 1
        for hh in range(min(ahead, A_HEADS)):
            scores(hh, hh)
        for hh in range(A_HEADS):
            if hh + ahead < A_HEADS:
                scores(hh + ahead, (hh + ahead) % (ahead + 1))
            head_out(hh, hh % (ahead + 1))

    def kv_head(x, g):
        return x[:, g * A_HEAD_DIM:(g + 1) * A_HEAD_DIM]

    kc = k_ref[n_lat:n_lat + n_ctx, :]
    vc = v_ref[n_lat:n_lat + n_ctx, :]

    @pl.when(i < n_lat_tiles)
    def _latent():
        q0 = i * tq
        ws = pl.multiple_of(jnp.clip(q0 - WINDOW, 0, n_lat - win), WINDOW)
        kw = k_ref[pl.ds(ws, win), :]
        vw = v_ref[pl.ds(ws, win), :]
        kpos = ws + lax.broadcasted_iota(jnp.int32, (win, tq), 0)
        qpos = q0 + lax.broadcasted_iota(jnp.int32, (win, tq), 1)
        mask = jnp.abs(qpos - kpos) <= WINDOW
        keys = [jnp.concatenate([kv_head(kw, g), kv_head(kc, g)], axis=0) for g in range(A_KV_HEADS)]
        vals = [jnp.concatenate([kv_head(vw, g), kv_head(vc, g)], axis=0) for g in range(A_KV_HEADS)]
        run_tile(keys, vals, mask)

    @pl.when(i >= n_lat_tiles)
    def _context():
        run_tile([kv_head(kc, g) for g in range(A_KV_HEADS)], [kv_head(vc, g) for g in range(A_KV_HEADS)], None)


def _attn_a(qa, ka, va, sink_tab, *, n_lat, n_ctx, tq):
    t_all = qa.shape[0]
    kern = functools.partial(_attn_a_kernel, tq=tq, n_lat=n_lat, n_ctx=n_ctx)
    return pl.pallas_call(
        kern,
        grid=(t_all // tq,),
        in_specs=[pl.BlockSpec((tq, A_WIDTH), lambda i: (i, 0)), _const_spec(ka.shape), _const_spec(va.shape),
                  _const_spec(sink_tab.shape)],
        out_specs=pl.BlockSpec((tq, A_WIDTH), lambda i: (i, 0)),
        out_shape=jax.ShapeDtypeStruct((t_all, A_WIDTH), BF16),
        scratch_shapes=[pltpu.VMEM((4, tq + 2 * WINDOW + n_ctx, tq), F32)],
        compiler_params=_params(("arbitrary",)),
        name="attn_a",
    )(qa, ka, va, sink_tab)


MLA_STRIP = 256
MLA_UNITS_PER_TRIP = 16


def _col_fold(st, pair_op):
    rows = st.shape[0]
    groups = 4 if rows % 32 == 0 else 1
    part = st[0:rows // groups]
    for g in range(1, groups):
        part = pair_op(part, st[g * (rows // groups):(g + 1) * (rows // groups)])
    return part


def _col_max(st):
    return jnp.max(_col_fold(st, jnp.maximum), axis=0, keepdims=True)


def _col_sum(st):
    return jnp.sum(_col_fold(st, jnp.add), axis=0, keepdims=True)


def _mla_kernel(q_ref, k_ref, vt_ref, prev_hbm, o_ref, acc_ref, m_ref, s_ref, *, nk, tq):
    del prev_hbm
    j = pl.program_id(1)
    ns = tq // MLA_STRIP

    @pl.when(j == 0)
    def _init():
        m_ref[...] = jnp.full(m_ref.shape, NEG_INF, F32)
        acc_ref[...] = jnp.zeros(acc_ref.shape, F32)

    tk = k_ref.shape[1]
    k_half = tk // 2
    v_half = (tk // MLA_STRIP + 1) // 2 * MLA_STRIP if tk > MLA_STRIP else tk

    def scores(hh, c, slot):
        q = q_ref[hh, c * MLA_STRIP:(c + 1) * MLA_STRIP, :]
        for r0, r1 in ((0, k_half), (k_half, tk)):
            s_ref[slot, r0:r1, :] = lax.dot_general(k_ref[hh, r0:r1, :], q, NT_DIMS, preferred_element_type=F32)

    def softmax_pv(hh, c, slot):
        cs = slice(c * MLA_STRIP, (c + 1) * MLA_STRIP)
        st = s_ref[slot]
        m_old = m_ref[hh, :, cs]
        m_new = jnp.maximum(m_old, _col_max(st))
        alpha = jnp.exp2(m_old - m_new)
        p = jnp.exp2(st - m_new).astype(BF16)
        pv = jnp.dot(vt_ref[hh, :, 0:v_half], p[0:v_half], preferred_element_type=F32)
        if v_half < tk:
            pv = pv + jnp.dot(vt_ref[hh, :, v_half:tk], p[v_half:tk], preferred_element_type=F32)
        acc_ref[hh, :, cs] = alpha * acc_ref[hh, :, cs] + pv
        m_ref[hh, :, cs] = m_new

    pipelined = ns % 2 == 0

    def head(hh, carry):
        for c in range(ns):
            if not pipelined:
                scores(hh, c, 0)
            elif c + 1 < ns:
                scores(hh, c + 1, (c + 1) % 2)
            else:
                scores(jnp.minimum(hh + 1, B_HEADS - 1), 0, 0)
            softmax_pv(hh, c, c % 2 if pipelined else 0)
        return carry

    if pipelined:
        scores(0, 0, 0)
    lax.fori_loop(0, B_HEADS, head, 0, unroll=max(1, min(B_HEADS, MLA_UNITS_PER_TRIP // ns)))

    @pl.when(j == nk - 1)
    def _finish():
        for hh in range(B_HEADS):
            o = acc_ref[hh, 0:B_V_DIM, :] * (1.0 / acc_ref[hh, B_V_DIM:B_V_DIM + 1, :])
            o_ref[:, hh * B_V_DIM:(hh + 1) * B_V_DIM] = o.T.astype(o_ref.dtype)


def _mla(qb, kb, vbt, prev, *, q_row0, n_q, k_row0, n_k, tq, tk):
    t_all = qb.shape[1]
    nq, nk = n_q // tq, n_k // tk
    qo, ko = q_row0 // tq, k_row0 // tk
    kern = functools.partial(_mla_kernel, nk=nk, tq=tq)
    return pl.pallas_call(
        kern,
        grid=(nq, nk),
        in_specs=[pl.BlockSpec((B_HEADS, tq, B_QK_PAD), lambda i, j: (0, qo + i, 0)),
                  pl.BlockSpec((B_HEADS, tk, B_QK_PAD), lambda i, j: (0, ko + j, 0)),
                  pl.BlockSpec((B_HEADS, B_VT_ROWS, tk), lambda i, j: (0, 0, ko + j)),
                  pl.BlockSpec(memory_space=pl.ANY)],
        out_specs=pl.BlockSpec((tq, B_WIDTH), lambda i, j: (qo + i, 0)),
        out_shape=jax.ShapeDtypeStruct((t_all, B_WIDTH), BF16),
        scratch_shapes=[pltpu.VMEM((B_HEADS, B_VT_ROWS, tq), F32), pltpu.VMEM((B_HEADS, 1, tq), F32),
                        pltpu.VMEM((2, tk, MLA_STRIP), F32)],
        input_output_aliases={3: 0},
        compiler_params=_params(("arbitrary", "arbitrary")),
        name="mla_attn",
    )(qb, kb, vbt, prev)


def _out_kernel(oa_ref, ob_ref, x_ref, mod_ref, ga_ref, gb_ref, wo_ref, pg_ref, o_ref, *, tm, n_lat):
    d = x_ref.shape[-1]
    is_ctx = _is_ctx_rows(pl.program_id(0), tm, n_lat)
    a = (_rms(oa_ref[...].astype(F32)) * ga_ref[...]).astype(BF16)
    b = (_rms(ob_ref[...].astype(F32)) * gb_ref[...]).astype(BF16)
    y = (jnp.dot(a, wo_ref[0:A_WIDTH, :], preferred_element_type=F32)
         + jnp.dot(b, wo_ref[A_WIDTH:A_WIDTH + B_WIDTH, :], preferred_element_type=F32))
    o_ref[...] = x_ref[...] + _mod_select(mod_ref, 2, d, is_ctx) * (_rms(y) * pg_ref[...])


def _attn_out(oa, ob, xs, mod, ga, gb, wo, pg, *, n_lat, tm):
    t_all, d = xs.shape
    row = lambda i: (i, 0)
    kern = functools.partial(_out_kernel, tm=tm, n_lat=n_lat)
    return pl.pallas_call(
        kern,
        grid=(t_all // tm,),
        in_specs=[pl.BlockSpec((tm, A_WIDTH), row), pl.BlockSpec((tm, B_WIDTH), row), pl.BlockSpec((tm, d), row),
                  _const_spec(mod.shape), _const_spec(ga.shape), _const_spec(gb.shape), _const_spec(wo.shape),
                  _const_spec(pg.shape)],
        out_specs=pl.BlockSpec((tm, d), row),
        out_shape=jax.ShapeDtypeStruct((t_all, d), F32),
        compiler_params=_params(("arbitrary",)),
        name="attn_out_mix",
    )(oa, ob, xs, mod, ga, gb, wo, pg)


def _swiglu_step(build_h, w1_ref, w3_ref, w2_ref, o_ref, h_ref, alongside=None):
    j = pl.program_id(1)

    @pl.when(j == 0)
    def _prologue():
        build_h()
        o_ref[...] = jnp.zeros(o_ref.shape, o_ref.dtype)

    if alongside is not None:
        alongside()
    h = h_ref[...]
    a = jnp.dot(h, w1_ref[...].astype(BF16), preferred_element_type=F32)
    b = jnp.dot(h, w3_ref[...].astype(BF16), preferred_element_type=F32)
    u = (a * (1.0 / (1.0 + jnp.exp(-a))) * b).astype(BF16)
    o_ref[...] += jnp.dot(u, w2_ref[...].astype(BF16), preferred_element_type=F32)


def _ffn_dense_kernel(x_ref, mod_ref, g_ref, pg_ref, w1_ref, w3_ref, w2_ref, o_ref, h_ref, gs_ref, acc_ref, *, tm,
                      n_lat, nj):
    tile_row0 = pl.program_id(0) * tm
    build_h = functools.partial(_norm_mod_rows, x_ref, g_ref, mod_ref, 3, tile_row0, n_lat, h_ref, gs_ref)
    _swiglu_step(build_h, w1_ref, w3_ref, w2_ref, acc_ref, h_ref)

    @pl.when(pl.program_id(1) == nj - 1)
    def _epilogue():
        _resid_rows(o_ref, acc_ref, x_ref, pg_ref, mod_ref, 5, tile_row0, n_lat, gs_ref)


def _ffn_dense(xs, mod, g, pg, w1, w3, w2, *, n_lat, tm, tf):
    t_all, d = xs.shape
    dff = w1.shape[1]
    nj = dff // tf
    kern = functools.partial(_ffn_dense_kernel, tm=tm, n_lat=n_lat, nj=nj)
    return pl.pallas_call(
        kern,
        grid=(t_all // tm, nj),
        in_specs=[pl.BlockSpec((tm, d), lambda i, j: (i, 0)), _const_spec(mod.shape), _const_spec(g.shape),
                  _const_spec(pg.shape),
                  pl.BlockSpec((d, tf), lambda i, j: (0, j)), pl.BlockSpec((d, tf), lambda i, j: (0, j)),
                  pl.BlockSpec((tf, d), lambda i, j: (j, 0))],
        out_specs=pl.BlockSpec((tm, d), lambda i, j: (i, 0)),
        out_shape=jax.ShapeDtypeStruct((t_all, d), F32),
        scratch_shapes=[pltpu.VMEM((tm, d), BF16), pltpu.VMEM((8, d), F32), pltpu.VMEM((tm, d), F32)],
        compiler_params=_params(("arbitrary", "arbitrary")),
        name="ffn_dense",
    )(xs, mod, g, pg, w1, w3, w2)


MOE_ROW_TILE_TARGET = 832


def _ffn_moe_kernel(te_ref, nu_ref, cur_ref, nxt_ref, x_hbm, w1_ref, w3_ref, w2_ref, o_ref, h_ref, xbuf, sem, *, tm,
                    nj):
    t, j = pl.program_id(0), pl.program_id(1)
    n_used = nu_ref[0]
    used = t < n_used
    slot = t % 2
    rows_per_step = tm // nj

    def row_copy(src_ref, r, s):
        return pltpu.make_async_copy(x_hbm.at[pl.ds(src_ref[0, 0, r], 1)], xbuf.at[s, pl.ds(r, 1)], sem.at[s])

    @pl.when((t == 0) & (j == 0))
    def _first_tile():
        lax.fori_loop(0, tm, lambda r, c: (row_copy(cur_ref, r, 0).start(), c)[1], 0, unroll=DMA_UNROLL)

    @pl.when((t <= n_used) & (j == 0))
    def _rows_arrived():
        lax.fori_loop(0, tm, lambda r, c: (row_copy(cur_ref, r, slot).wait(), c)[1], 0, unroll=DMA_UNROLL)

    @pl.when(used)
    def _used():
        def request_next_tile():
            for k in range(rows_per_step):
                row_copy(nxt_ref, j * rows_per_step + k, 1 - slot).start()

        def build_h():
            half = h_ref.shape[1] // 2
            lo, hi = _unpack_bf16_halves(xbuf[slot])
            h_ref[:, 0:half] = lo
            h_ref[:, half:2 * half] = hi

        _swiglu_step(build_h, w1_ref, w3_ref, w2_ref, o_ref, h_ref, alongside=request_next_tile)

    @pl.when(jnp.logical_not(used) & (j == 0))
    def _idle():
        o_ref[...] = jnp.zeros(o_ref.shape, o_ref.dtype)


def _ffn_moe(hp, slot_src, tile_expert, n_used, w1, w3, w2, *, tm, tf):
    d = 2 * hp.shape[1]
    n_tiles = slot_src.shape[0]
    dff = w1.shape[2]
    nj = dff // tf
    assert tm % nj == 0, "the next tile's row requests are spread evenly over the d_ff chunks"

    def t_eff(t, nu):
        return jnp.minimum(t, nu[0] - 1)

    def j_eff(t, j, nu):
        return jnp.where(t < nu[0], j, nj - 1)

    smem_rows = lambda fn: pl.BlockSpec((1, 1, tm), fn, memory_space=pltpu.SMEM)
    grid_spec = pltpu.PrefetchScalarGridSpec(
        num_scalar_prefetch=2,
        grid=(n_tiles, nj),
        in_specs=[smem_rows(lambda t, j, te, nu: (t, 0, 0)),
                  smem_rows(lambda t, j, te, nu: (jnp.minimum(t + 1, n_tiles - 1), 0, 0)),
                  pl.BlockSpec(memory_space=pl.ANY),
                  pl.BlockSpec((None, d, tf), lambda t, j, te, nu: (te[t_eff(t, nu)], 0, j_eff(t, j, nu))),
                  pl.BlockSpec((None, d, tf), lambda t, j, te, nu: (te[t_eff(t, nu)], 0, j_eff(t, j, nu))),
                  pl.BlockSpec((None, tf, d), lambda t, j, te, nu: (te[t_eff(t, nu)], j_eff(t, j, nu), 0))],
        out_specs=pl.BlockSpec((tm, d), lambda t, j, te, nu: (t, 0)),
        scratch_shapes=[pltpu.VMEM((tm, d), BF16), pltpu.VMEM((2, tm, d // 2), jnp.uint32),
                        pltpu.SemaphoreType.DMA((2,))],
    )
    return pl.pallas_call(
        functools.partial(_ffn_moe_kernel, tm=tm, nj=nj),
        grid_spec=grid_spec,
        out_shape=jax.ShapeDtypeStruct((n_tiles * tm, d), F32),
        compiler_params=_params(("arbitrary", "arbitrary")),
        name="ffn_moe",
    )(tile_expert, n_used, slot_src, slot_src, hp, w1, w3, w2)


def _pack_bf16_halves(h):
    half = h.shape[1] // 2
    bits = lax.bitcast_convert_type(h.astype(BF16).astype(F32), jnp.uint32)
    return (bits[:, 0:half] >> 16) | bits[:, half:2 * half]


def _unpack_bf16_halves(u):
    lo = lax.bitcast_convert_type(u << 16, F32).astype(BF16)
    hi = lax.bitcast_convert_type(u & jnp.uint32(0xFFFF0000), F32).astype(BF16)
    return lo, hi


def _bf16_part(x):
    return lax.bitcast_convert_type(lax.bitcast_convert_type(x, jnp.uint32) & jnp.uint32(0xFFFF0000), F32)


def _route_kernel(x_ref, mod_ref, g_ref, rw_ref, o_ref, hp_ref):
    d = x_ref.shape[-1]
    h = _rms(x_ref[...]) * g_ref[...] * (1.0 + mod_ref[0:1, 4 * d:5 * d]) + mod_ref[0:1, 3 * d:4 * d]
    hp_ref[...] = _pack_bf16_halves(h)
    h_top = _bf16_part(h)
    h_hi, h_lo = h_top.astype(BF16), (h - h_top).astype(BF16)
    hi = jnp.dot(h_hi, rw_ref[0], preferred_element_type=F32)
    logits = hi + pltpu.roll(hi, LANE - N_EXPERTS, 1) + jnp.dot(h_lo, rw_ref[1], preferred_element_type=F32)
    lane = lax.broadcasted_iota(jnp.int32, logits.shape, 1)
    lg = jnp.where(lane < N_EXPERTS, logits, -jnp.inf)
    m1 = jnp.max(lg, axis=-1, keepdims=True)
    i1 = jnp.min(jnp.where(lg == m1, lane, LANE), axis=-1, keepdims=True)
    lg2 = jnp.where(lane == i1, -jnp.inf, lg)
    m2 = jnp.max(lg2, axis=-1, keepdims=True)
    i2 = jnp.min(jnp.where(lg2 == m2, lane, LANE), axis=-1, keepdims=True)
    e = jnp.exp(m2 - m1)
    w1 = 1.0 / (1.0 + e)
    w2 = e * w1
    o_ref[...] = jnp.where(lane == 0, i1.astype(F32),
                           jnp.where(lane == 1, i2.astype(F32), jnp.where(lane == 2, w1, jnp.where(lane == 3, w2, 0.0))))


def _route(xs, mod, g, rw_pad, *, n_lat, tm):
    d = xs.shape[1]
    return pl.pallas_call(
        _route_kernel,
        grid=(n_lat // tm,),
        in_specs=[pl.BlockSpec((tm, d), lambda i: (i, 0)), _const_spec(mod.shape), _const_spec(g.shape),
                  _const_spec(rw_pad.shape)],
        out_specs=[pl.BlockSpec((tm, LANE), lambda i: (i, 0)), pl.BlockSpec((tm, d // 2), lambda i: (i, 0))],
        out_shape=[jax.ShapeDtypeStruct((n_lat, LANE), F32), jax.ShapeDtypeStruct((n_lat, d // 2), jnp.uint32)],
        compiler_params=_params(("arbitrary",)),
        name="moe_route",
    )(xs, mod, g, rw_pad)


def _combine_kernel(cur_ref, nxt_ref, rt_ref, x_ref, mod_ref, pg_ref, y_hbm, o_ref, buf, sem, *, tm, n_steps):
    i = pl.program_id(0)
    slot = i % 2

    def row_copy(dest_ref, r, k, s):
        return pltpu.make_async_copy(y_hbm.at[pl.ds(dest_ref[0, k, r], 1)], buf.at[s, k, pl.ds(r, 1)], sem.at[s])

    def start_tile0(r, c):
        row_copy(cur_ref, r, 0, 0).start()
        row_copy(cur_ref, r, 1, 0).start()
        return c

    def wait(r, c):
        row_copy(cur_ref, r, 0, slot).wait()
        row_copy(cur_ref, r, 1, slot).wait()
        return c

    @pl.when(i == 0)
    def _first_tile():
        lax.fori_loop(0, tm, start_tile0, 0, unroll=DMA_UNROLL)

    lax.fori_loop(0, tm, wait, 0, unroll=DMA_UNROLL)

    @pl.when(i < n_steps)
    def _tile():
        for r in range(tm):
            row_copy(nxt_ref, r, 0, 1 - slot).start()
            row_copy(nxt_ref, r, 1, 1 - slot).start()
        d = x_ref.shape[-1]
        rt = rt_ref[...]
        y = rt[:, 2:3] * buf[slot, 0] + rt[:, 3:4] * buf[slot, 1]
        o_ref[...] = x_ref[...] + mod_ref[0:1, 5 * d:6 * d] * (_rms(y) * pg_ref[...])


def _combine(yg, dest, route, xs, mod, pg, *, n_lat, tm):
    d = xs.shape[1]
    n_steps = n_lat // tm
    kern = functools.partial(_combine_kernel, tm=tm, n_steps=n_steps)
    tile = lambda i: jnp.minimum(i, n_steps - 1)
    nxt_tile = lambda i: jnp.minimum(i + 1, n_steps - 1)
    return pl.pallas_call(
        kern,
        grid=(n_steps + 1,),
        in_specs=[pl.BlockSpec((1, 2, tm), lambda i: (tile(i), 0, 0), memory_space=pltpu.SMEM),
                  pl.BlockSpec((1, 2, tm), lambda i: (nxt_tile(i), 0, 0), memory_space=pltpu.SMEM),
                  pl.BlockSpec((tm, LANE), lambda i: (tile(i), 0)), pl.BlockSpec((tm, d), lambda i: (tile(i), 0)),
                  _const_spec(mod.shape), _const_spec(pg.shape), pl.BlockSpec(memory_space=pl.ANY)],
        out_specs=pl.BlockSpec((tm, d), lambda i: (tile(i), 0)),
        out_shape=jax.ShapeDtypeStruct((n_lat, d), F32),
        scratch_shapes=[pltpu.VMEM((2, 2, tm, d), F32), pltpu.SemaphoreType.DMA((2,))],
        compiler_params=_params(("arbitrary",)),
        name="moe_combine",
    )(dest, dest, route, xs, mod, pg, yg)


def _rope_tables(n_lat, n_ctx):
    rows = n_lat // GRID_W

    def tabs(rot_dim):
        axis_dim = rot_dim // 2
        quarter = rot_dim // 4
        inv_freq = ROPE_BASE ** (-jnp.arange(0, axis_dim, 2, dtype=F32) / axis_dim)
        ang_r = jnp.arange(rows, dtype=F32)[:, None] * inv_freq[None, :]
        ang_c = jnp.arange(GRID_W, dtype=F32)[:, None] * inv_freq[None, :]

        def over_grid(fn):
            r = jnp.broadcast_to(fn(ang_r)[:, None, :], (rows, GRID_W, quarter))
            c = jnp.broadcast_to(fn(ang_c)[None, :, :], (rows, GRID_W, quarter))
            return jnp.concatenate([r, r, c, c], axis=-1).reshape(n_lat, rot_dim)

        cos, sin = over_grid(jnp.cos), over_grid(jnp.sin)
        lower = (jnp.arange(rot_dim) % (2 * quarter)) < quarter
        sin_m = jnp.where(lower[None, :], -sin, 0.0)
        sin_p = jnp.where(lower[None, :], 0.0, sin)
        pad = LANE - rot_dim
        cos = jnp.pad(cos, ((0, n_ctx), (0, pad)), constant_values=1.0)
        sin_m = jnp.pad(sin_m, ((0, n_ctx), (0, pad)))
        sin_p = jnp.pad(sin_p, ((0, n_ctx), (0, pad)))
        return [cos, sin_m, sin_p]

    return jnp.stack(tabs(A_HEAD_DIM) + tabs(B_ROPE_DIM))


def _layer_weights(i, w_in, w_uq, w_ukv, w_o):
    d = w_in.shape[1]
    q_w = A_WIDTH + B_Q_RANK
    wi = w_in[i]
    qa, cq = wi[:, :A_WIDTH], wi[:, A_WIDTH:q_w]
    o1, o2, o3 = q_w + A_KV_WIDTH, q_w + 2 * A_KV_WIDTH, q_w + 2 * A_KV_WIDTH + B_KV_RANK
    ka, va, ckv, kr = wi[:, q_w:o1], wi[:, o1:o2], wi[:, o2:o3], wi[:, o3:]
    win = jnp.concatenate([qa, ka, va, cq, ckv, kr, jnp.zeros((d, LANE - B_ROPE_DIM), F32)], axis=1).astype(BF16)
    uq = w_uq[i].reshape(B_Q_RANK, B_HEADS, B_NOPE_DIM + B_ROPE_DIM)
    uq = jnp.pad(uq, ((0, 0), (0, 0), (0, B_QK_PAD - B_NOPE_DIM - B_ROPE_DIM)))
    wuq = uq.reshape(B_Q_RANK, B_HEADS * B_QK_PAD).astype(BF16)
    ukv = w_ukv[i].reshape(B_KV_RANK, B_HEADS, B_NOPE_DIM + B_V_DIM)
    wukn = ukv[:, :, :B_NOPE_DIM].reshape(B_KV_RANK, B_HEADS * B_NOPE_DIM).astype(BF16)
    wuvt = ukv[:, :, B_NOPE_DIM:].reshape(B_KV_RANK, B_HEADS * B_V_DIM).T.astype(BF16)
    return win, wuq, wukn, wuvt, w_o[i].astype(BF16)


def _routing_slots(route, n_lat, tm_e, n_tiles):
    e12 = route[:, 0:2].astype(jnp.int32).T.reshape(-1)
    onehot = (e12[:, None] == jnp.arange(N_EXPERTS)[None, :]).astype(jnp.int32)
    csum = jnp.cumsum(onehot, axis=0)
    rank = jnp.sum(csum * onehot, axis=1) - 1
    counts = csum[-1]
    tiles_per = (counts + tm_e - 1) // tm_e
    tile_end = jnp.cumsum(tiles_per)
    start = (tile_end - tiles_per) * tm_e
    dest = (jnp.sum(start[None, :] * onehot, axis=1) + rank).astype(jnp.int32)
    n_used = tile_end[-1].astype(jnp.int32).reshape(1)
    t_idx = jnp.arange(n_tiles, dtype=jnp.int32)
    tile_expert = jnp.minimum(jnp.sum((t_idx[:, None] >= tile_end[None, :]).astype(jnp.int32), axis=1),
                              N_EXPERTS - 1).astype(jnp.int32)
    token = jnp.tile(jnp.arange(n_lat, dtype=jnp.int32), 2)
    slot_src = jnp.zeros((n_tiles * tm_e,), jnp.int32).at[dest].set(token, unique_indices=True)
    return dest.reshape(2, n_lat), slot_src.reshape(n_tiles, 1, tm_e), tile_expert, n_used


def _tile(n, candidates):
    for c in candidates:
        if n % c == 0:
            return c
    raise ValueError(f"no supported tile size for {n} rows")


def kernel(x, c, ctx, c_ctx, w_ada, b_ada, pre_attn_g, post_attn_g, pre_ffn_g, post_ffn_g, w_in, attn_sink, q_norm_g,
           kv_norm_g, w_uq, w_ukv, grp_a_g, grp_b_g, w_o, ffn_w1, ffn_w3, ffn_w2, router_w, moe_w1, moe_w3, moe_w2):
    b, n_lat, d = x.shape
    n_ctx = ctx.shape[1]
    depth = w_ada.shape[0]
    assert b == 1 and c.shape[0] == 1, "single-sequence kernel"
    t_all = n_lat + n_ctx
    tm = _tile(t_all, (640, 256, 128))
    tq_a = 256
    tq_b = _tile(n_lat, (2048, 1024, 512, 256))
    tk_b = _tile(t_all, (1280, 256))
    tf = 512
    tm_r = _tile(n_lat, (512, 256))
    tm_c = 256
    assert n_ctx == tq_a and n_lat % tq_a == 0 and n_lat >= tq_a + 2 * WINDOW

    xs = jnp.concatenate([x[0], ctx[0]], axis=0)
    cc = jnp.stack([c[0], c_ctx])
    mods = _ada(cc, w_ada, b_ada)
    rope_tab = _rope_tables(n_lat, n_ctx)
    row2 = lambda v: v.reshape(1, -1)

    for i in range(depth):
        last = i == depth - 1
        mod = mods[i]
        win, wuq, wukn, wuvt, wo = _layer_weights(i, w_in, w_uq, w_ukv, w_o)
        qa, ka, va, qb, kb, vbt = _proj(xs, mod, row2(pre_attn_g[i]), win, row2(q_norm_g[i]), row2(kv_norm_g[i]),
                                        wuq, wukn, wuvt, rope_tab, n_lat=n_lat, tm=tm)
        sink_tab = jnp.broadcast_to((attn_sink[i] * LOG2E)[:, None], (A_HEADS, tq_a))
        oa = _attn_a(qa, ka, va, sink_tab, n_lat=n_lat, n_ctx=n_ctx, tq=tq_a)
        ob = _mla(qb, kb, vbt, jnp.zeros((t_all, B_WIDTH), BF16), q_row0=0, n_q=n_lat, k_row0=0, n_k=t_all,
                  tq=tq_b, tk=tk_b)
        ob = _mla(qb, kb, vbt, ob, q_row0=n_lat, n_q=n_ctx, k_row0=n_lat, n_k=n_ctx, tq=n_ctx, tk=n_ctx)
        xs = _attn_out(oa, ob, xs, mod, row2(grp_a_g[i]), row2(grp_b_g[i]), wo, row2(post_attn_g[i]),
                       n_lat=n_lat, tm=tm)
        jj = i // 2
        if i % 2 == 0:
            xs = _ffn_dense(xs, mod, row2(pre_ffn_g[i]), row2(post_ffn_g[i]), ffn_w1[jj].astype(BF16),
                            ffn_w3[jj].astype(BF16), ffn_w2[jj].astype(BF16), n_lat=n_lat, tm=tm, tf=tf)
        else:
            n_rows = n_lat
            assert last, "expert layers other than the last would also need the context rows routed"
            nj = moe_w1.shape[-1] // tf
            tm_e = -(-MOE_ROW_TILE_TARGET // (nj * ROW_CHUNK)) * ROW_CHUNK * nj
            n_tiles = (2 * n_rows + N_EXPERTS * (tm_e - 1)) // tm_e + 1
            rw_top = _bf16_part(router_w[jj])
            rw_hi, rw_lo = rw_top.astype(BF16), (router_w[jj] - rw_top).astype(BF16)
            zeros = jnp.zeros((d, LANE - 2 * N_EXPERTS), BF16)
            rw_pad = jnp.stack([jnp.concatenate([rw_hi, rw_lo, zeros], axis=1),
                                jnp.concatenate([rw_hi, jnp.zeros_like(rw_lo), zeros], axis=1)])
            route, hp = _route(xs, mod, row2(pre_ffn_g[i]), rw_pad, n_lat=n_rows, tm=tm_r)
            dest, slot_src, tile_expert, n_used = _routing_slots(route, n_rows, tm_e, n_tiles)
            dest_t = dest.reshape(2, n_rows // tm_c, tm_c).transpose(1, 0, 2)
            yg = _ffn_moe(hp, slot_src, tile_expert, n_used, moe_w1[jj], moe_w3[jj], moe_w2[jj], tm=tm_e, tf=tf)
            xs = _combine(yg, dest_t, route, xs, mod, row2(post_ffn_g[i]), n_lat=n_rows, tm=tm_c)
    return xs[:n_lat].reshape(b, n_lat, d)
```

```python
import functools
import math

import jax
import jax.numpy as jnp
from jax import lax
from jax.experimental import pallas as pl
from jax.experimental.pallas import tpu as pltpu

GRID_W = 64
EPS = 1e-6
NEG_INF = -1e30
ROPE_BASE = 10000.0
A_HEADS = 8
A_KV_HEADS = 2
A_HEAD_DIM = 128
WINDOW = 128
B_HEADS = 8
B_Q_RANK = 384
B_KV_RANK = 256
B_NOPE_DIM = 128
B_ROPE_DIM = 64
B_V_DIM = 128
N_EXPERTS = 8
A_WIDTH = A_HEADS * A_HEAD_DIM
A_KV_WIDTH = A_KV_HEADS * A_HEAD_DIM
B_WIDTH = B_HEADS * B_V_DIM
B_QK_PAD = 256
B_VT_ROWS = B_V_DIM + 16

LOG2E = math.log2(math.e)
LANE = 128
VMEM_LIMIT_BYTES = 60 * 1024 * 1024

F32 = jnp.float32
BF16 = jnp.bfloat16
NT_DIMS = (((1,), (1,)), ((), ()))


def _params(sem, vmem=VMEM_LIMIT_BYTES):
    return pltpu.CompilerParams(dimension_semantics=sem, vmem_limit_bytes=vmem)


def _const_spec(shape):
    nd = len(shape)
    return pl.BlockSpec(shape, lambda *_: (0,) * nd, pipeline_mode=pl.Buffered(1))


def _rms(x):
    return x * lax.rsqrt(jnp.mean(x * x, axis=-1, keepdims=True) + EPS)


DMA_UNROLL = 8
ROW_CHUNK = 16
ROW_UNROLL = 8


def _row_chunks(tm, body):
    def step(r, carry):
        body(pl.ds(pl.multiple_of(r * ROW_CHUNK, ROW_CHUNK), ROW_CHUNK), r * ROW_CHUNK)
        return carry

    lax.fori_loop(0, tm // ROW_CHUNK, step, 0, unroll=ROW_UNROLL)


def _mod_row(tile_row0, r0, n_lat):
    return jnp.where(tile_row0 + r0 >= n_lat, 1, 0)


def _norm_mod_rows(x_ref, g_ref, mod_ref, k_shift, tile_row0, n_lat, h_ref, gs_ref):
    tm, d = x_ref.shape
    gs_ref[...] = g_ref[...] * (1.0 + mod_ref[:, (k_shift + 1) * d:(k_shift + 2) * d])

    def body(rows, r0):
        mrow = _mod_row(tile_row0, r0, n_lat)
        h = _rms(x_ref[rows, :]) * gs_ref[pl.ds(mrow, 1), :]
        h_ref[rows, :] = (h + mod_ref[pl.ds(mrow, 1), k_shift * d:(k_shift + 1) * d]).astype(BF16)

    _row_chunks(tm, body)


def _resid_rows(o_ref, y_ref, x_ref, pg_ref, mod_ref, k_gate, tile_row0, n_lat, gs_ref):
    tm, d = x_ref.shape
    gs_ref[...] = pg_ref[...] * mod_ref[:, k_gate * d:(k_gate + 1) * d]

    def body(rows, r0):
        mrow = _mod_row(tile_row0, r0, n_lat)
        o_ref[rows, :] = x_ref[rows, :] + gs_ref[pl.ds(mrow, 1), :] * _rms(y_ref[rows, :])

    _row_chunks(tm, body)


def _is_ctx_rows(tile_index, tm, n_lat):
    row = tile_index * tm + lax.broadcasted_iota(jnp.int32, (tm, 1), 0)
    return row >= n_lat


def _mod_select(mod_ref, k, d, is_ctx):
    return jnp.where(is_ctx, mod_ref[1:2, k * d:(k + 1) * d], mod_ref[0:1, k * d:(k + 1) * d])


ADA_ROWS = 2


def _ada_kernel(c_ref, w_ref, b_ref, o_ref):
    tn = w_ref.shape[-1]
    o_ref[...] = jnp.zeros(o_ref.shape, o_ref.dtype)
    for r in range(ADA_ROWS):
        c = c_ref[r]
        a = c * (1.0 / (1.0 + jnp.exp(-c)))
        for l in range(tn // LANE):
            cols = slice(l * LANE, (l + 1) * LANE)
            o_ref[r:r + 1, cols] = _col_sum(a * w_ref[:, cols]) + b_ref[:, cols]


def _ada(cc, w_ada, b_ada):
    depth, d, n = w_ada.shape
    tn = n // 8
    c_lanes = jnp.broadcast_to(cc[:, :, None], (ADA_ROWS, d, LANE))
    return pl.pallas_call(
        _ada_kernel,
        grid=(depth, n // tn),
        in_specs=[pl.BlockSpec((ADA_ROWS, d, LANE), lambda l, j: (0, 0, 0)),
                  pl.BlockSpec((None, d, tn), lambda l, j: (l, 0, j)),
                  pl.BlockSpec((None, 1, tn), lambda l, j: (l, 0, j))],
        out_specs=pl.BlockSpec((None, 8, tn), lambda l, j: (l, 0, j)),
        out_shape=jax.ShapeDtypeStruct((depth, 8, n), F32),
        compiler_params=_params(("arbitrary", "arbitrary")),
        name="ada_mod",
    )(c_lanes, w_ada, b_ada.reshape(depth, 1, n))


def _rope(t, cos, sin_m, sin_p, quarter):
    n = t.shape[-1]
    return t * cos + pltpu.roll(t, n - quarter, 1) * sin_m + pltpu.roll(t, quarter, 1) * sin_p


def _proj_kernel(x_ref, mod_ref, g_ref, win_ref, qg_ref, kvg_ref, wuq_ref, wukn_ref, wuvt_ref, rope_ref,
                 qa_ref, ka_ref, va_ref, qb_ref, kb_ref, vbt_ref, *, tm, n_lat, scale_a, scale_b):
    d = x_ref.shape[-1]
    is_ctx = _is_ctx_rows(pl.program_id(0), tm, n_lat)
    h = _rms(x_ref[...]) * g_ref[...] * (1.0 + _mod_select(mod_ref, 1, d, is_ctx)) + _mod_select(mod_ref, 0, d, is_ctx)
    proj = jnp.dot(h.astype(BF16), win_ref[...], preferred_element_type=F32)
    cos_a, sinm_a, sinp_a = rope_ref[0], rope_ref[1], rope_ref[2]
    cos_b, sinm_b, sinp_b = rope_ref[3], rope_ref[4], rope_ref[5]
    qa_q = A_HEAD_DIM // 4
    b_q = B_ROPE_DIM // 4
    cos_as, sinm_as, sinp_as = cos_a * scale_a, sinm_a * scale_a, sinp_a * scale_a
    for hh in range(A_HEADS):
        t = proj[:, hh * LANE:(hh + 1) * LANE]
        qa_ref[:, hh * LANE:(hh + 1) * LANE] = _rope(t, cos_as, sinm_as, sinp_as, qa_q).astype(BF16)
    o = A_WIDTH
    for hh in range(A_KV_HEADS):
        t = proj[:, o + hh * LANE:o + (hh + 1) * LANE]
        ka_ref[:, hh * LANE:(hh + 1) * LANE] = _rope(t, cos_a, sinm_a, sinp_a, qa_q).astype(BF16)
    o += A_KV_WIDTH
    va_ref[...] = proj[:, o:o + A_KV_WIDTH].astype(BF16)
    o += A_KV_WIDTH
    cq = (_rms(proj[:, o:o + B_Q_RANK]) * qg_ref[...]).astype(BF16)
    o += B_Q_RANK
    qm = jnp.dot(cq, wuq_ref[...], preferred_element_type=F32)
    cos_bs, sinm_bs, sinp_bs = cos_b * scale_b, sinm_b * scale_b, sinp_b * scale_b
    for hh in range(B_HEADS):
        b0 = hh * B_QK_PAD
        qb_ref[hh, :, 0:LANE] = (qm[:, b0:b0 + LANE] * scale_b).astype(BF16)
        qb_ref[hh, :, LANE:2 * LANE] = _rope(qm[:, b0 + LANE:b0 + 2 * LANE], cos_bs, sinm_bs, sinp_bs, b_q).astype(BF16)
    ckv = (_rms(proj[:, o:o + B_KV_RANK]) * kvg_ref[...]).astype(BF16)
    o += B_KV_RANK
    kr = _rope(proj[:, o:o + LANE], cos_b, sinm_b, sinp_b, b_q).astype(BF16)
    kn = jnp.dot(ckv, wukn_ref[...], preferred_element_type=F32)
    vt = lax.dot_general(wuvt_ref[...], ckv, NT_DIMS, preferred_element_type=F32)
    for hh in range(B_HEADS):
        kb_ref[hh, :, 0:LANE] = kn[:, hh * LANE:(hh + 1) * LANE].astype(BF16)
        kb_ref[hh, :, LANE:2 * LANE] = kr
        vbt_ref[hh, 0:B_V_DIM, :] = vt[hh * B_V_DIM:(hh + 1) * B_V_DIM, :].astype(BF16)
        vbt_ref[hh, B_V_DIM:B_VT_ROWS, :] = jnp.ones((B_VT_ROWS - B_V_DIM, tm), BF16)


def _proj(xs, mod, g, win, qg, kvg, wuq, wukn, wuvt, rope_tab, *, n_lat, tm):
    t_all, d = xs.shape
    row = lambda i: (i, 0)
    kern = functools.partial(_proj_kernel, tm=tm, n_lat=n_lat, scale_a=LOG2E / math.sqrt(A_HEAD_DIM),
                             scale_b=LOG2E / math.sqrt(B_NOPE_DIM + B_ROPE_DIM))
    return pl.pallas_call(
        kern,
        grid=(t_all // tm,),
        in_specs=[pl.BlockSpec((tm, d), row), _const_spec(mod.shape), _const_spec(g.shape), _const_spec(win.shape),
                  _const_spec(qg.shape), _const_spec(kvg.shape), _const_spec(wuq.shape), _const_spec(wukn.shape),
                  _const_spec(wuvt.shape), pl.BlockSpec((6, tm, LANE), lambda i: (0, i, 0))],
        out_specs=[pl.BlockSpec((tm, A_WIDTH), row), pl.BlockSpec((tm, A_KV_WIDTH), row),
                   pl.BlockSpec((tm, A_KV_WIDTH), row),
                   pl.BlockSpec((B_HEADS, tm, B_QK_PAD), lambda i: (0, i, 0)),
                   pl.BlockSpec((B_HEADS, tm, B_QK_PAD), lambda i: (0, i, 0)),
                   pl.BlockSpec((B_HEADS, B_VT_ROWS, tm), lambda i: (0, 0, i))],
        out_shape=[jax.ShapeDtypeStruct((t_all, A_WIDTH), BF16), jax.ShapeDtypeStruct((t_all, A_KV_WIDTH), BF16),
                   jax.ShapeDtypeStruct((t_all, A_KV_WIDTH), BF16),
                   jax.ShapeDtypeStruct((B_HEADS, t_all, B_QK_PAD), BF16),
                   jax.ShapeDtypeStruct((B_HEADS, t_all, B_QK_PAD), BF16),
                   jax.ShapeDtypeStruct((B_HEADS, B_VT_ROWS, t_all), BF16)],
        compiler_params=_params(("arbitrary",)),
        name="pre_attn_proj",
    )(xs, mod, g, win, qg, kvg, wuq, wukn, wuvt, rope_tab)


def _attn_a_kernel(q_ref, k_ref, v_ref, sink_ref, o_ref, s_ref, *, tq, n_lat, n_ctx):
    i = pl.program_id(0)
    n_lat_tiles = n_lat // tq
    win = tq + 2 * WINDOW
    grp = A_HEADS // A_KV_HEADS
    tn_dims = (((0,), (0,)), ((), ()))

    def run_tile(keys, vals, mask):
        n_keys = keys[0].shape[0]

        def scores(hh, slot):
            q = q_ref[:, hh * A_HEAD_DIM:(hh + 1) * A_HEAD_DIM]
            s_ref[slot, 0:n_keys, :] = lax.dot_general(keys[hh // grp], q, NT_DIMS, preferred_element_type=F32)

        def head_out(hh, slot):
            s = s_ref[slot, 0:n_keys, :]
            if mask is not None:
                s = jnp.concatenate([jnp.where(mask, s[0:win], NEG_INF), s[win:n_keys]], axis=0)
            sink = sink_ref[hh:hh + 1, :]
            m = jnp.maximum(_col_max(s), sink)
            p = jnp.exp2(s - m)
            den = _col_sum(p) + jnp.exp2(sink - m)
            o = lax.dot_general(vals[hh // grp], p.astype(BF16), tn_dims, preferred_element_type=F32)
            o_ref[:, hh * A_HEAD_DIM:(hh + 1) * A_HEAD_DIM] = (o * (1.0 / den)).T.astype(o_ref.dtype)

        ahead = s_ref.shape[0] - 1
        for hh in range(min(ahead, A_HEADS)):
            scores(hh, hh)
        for hh in range(A_HEADS):
            if hh + ahead < A_HEADS:
                scores(hh + ahead, (hh + ahead) % (ahead + 1))
            head_out(hh, hh % (ahead + 1))

    def kv_head(x, g):
        return x[:, g * A_HEAD_DIM:(g + 1) * A_HEAD_DIM]

    kc = k_ref[n_lat:n_lat + n_ctx, :]
    vc = v_ref[n_lat:n_lat + n_ctx, :]

    @pl.when(i < n_lat_tiles)
    def _latent():
        q0 = i * tq
        ws = pl.multiple_of(jnp.clip(q0 - WINDOW, 0, n_lat - win), WINDOW)
        kw = k_ref[pl.ds(ws, win), :]
        vw = v_ref[pl.ds(ws, win), :]
        kpos = ws + lax.broadcasted_iota(jnp.int32, (win, tq), 0)
        qpos = q0 + lax.broadcasted_iota(jnp.int32, (win, tq), 1)
        mask = jnp.abs(qpos - kpos) <= WINDOW
        keys = [jnp.concatenate([kv_head(kw, g), kv_head(kc, g)], axis=0) for g in range(A_KV_HEADS)]
        vals = [jnp.concatenate([kv_head(vw, g), kv_head(vc, g)], axis=0) for g in range(A_KV_HEADS)]
        run_tile(keys, vals, mask)

    @pl.when(i >= n_lat_tiles)
    def _context():
        run_tile([kv_head(kc, g) for g in range(A_KV_HEADS)], [kv_head(vc, g) for g in range(A_KV_HEADS)], None)


def _attn_a(qa, ka, va, sink_tab, *, n_lat, n_ctx, tq):
    t_all = qa.shape[0]
    kern = functools.partial(_attn_a_kernel, tq=tq, n_lat=n_lat, n_ctx=n_ctx)
    return pl.pallas_call(
        kern,
        grid=(t_all // tq,),
        in_specs=[pl.BlockSpec((tq, A_WIDTH), lambda i: (i, 0)), _const_spec(ka.shape), _const_spec(va.shape),
                  _const_spec(sink_tab.shape)],
        out_specs=pl.BlockSpec((tq, A_WIDTH), lambda i: (i, 0)),
        out_shape=jax.ShapeDtypeStruct((t_all, A_WIDTH), BF16),
        scratch_shapes=[pltpu.VMEM((4, tq + 2 * WINDOW + n_ctx, tq), F32)],
        compiler_params=_params(("arbitrary",)),
        name="attn_a",
    )(qa, ka, va, sink_tab)


MLA_STRIP = 256
MLA_UNITS_PER_TRIP = 32


def _col_fold(st, pair_op):
    rows = st.shape[0]
    groups = 4 if rows % 32 == 0 else 1
    part = st[0:rows // groups]
    for g in range(1, groups):
        part = pair_op(part, st[g * (rows // groups):(g + 1) * (rows // groups)])
    return part


def _col_max(st):
    return jnp.max(_col_fold(st, jnp.maximum), axis=0, keepdims=True)


def _col_sum(st):
    return jnp.sum(_col_fold(st, jnp.add), axis=0, keepdims=True)


def _mla_kernel(q_ref, k_ref, vt_ref, prev_hbm, o_ref, acc_ref, m_ref, s_ref, *, nk, tq):
    del prev_hbm
    j = pl.program_id(1)
    ns = tq // MLA_STRIP

    @pl.when(j == 0)
    def _init():
        m_ref[...] = jnp.full(m_ref.shape, NEG_INF, F32)
        acc_ref[...] = jnp.zeros(acc_ref.shape, F32)

    tk = k_ref.shape[1]
    k_half = tk // 2
    v_half = (tk // MLA_STRIP + 1) // 2 * MLA_STRIP if tk > MLA_STRIP else tk

    def scores(hh, c, slot):
        q = q_ref[hh, c * MLA_STRIP:(c + 1) * MLA_STRIP, :]
        for r0, r1 in ((0, k_half), (k_half, tk)):
            s_ref[slot, r0:r1, :] = lax.dot_general(k_ref[hh, r0:r1, :], q, NT_DIMS, preferred_element_type=F32)

    def softmax_pv(hh, c, slot):
        cs = slice(c * MLA_STRIP, (c + 1) * MLA_STRIP)
        st = s_ref[slot]
        m_old = m_ref[hh, :, cs]
        m_new = jnp.maximum(m_old, _col_max(st))
        alpha = jnp.exp2(m_old - m_new)
        p = jnp.exp2(st - m_new).astype(BF16)
        pv = jnp.dot(vt_ref[hh, :, 0:v_half], p[0:v_half], preferred_element_type=F32)
        if v_half < tk:
            pv = pv + jnp.dot(vt_ref[hh, :, v_half:tk], p[v_half:tk], preferred_element_type=F32)
        acc_ref[hh, :, cs] = alpha * acc_ref[hh, :, cs] + pv
        m_ref[hh, :, cs] = m_new

    pipelined = ns % 2 == 0

    def head(hh, carry):
        for c in range(ns):
            if not pipelined:
                scores(hh, c, 0)
            elif c + 1 < ns:
                scores(hh, c + 1, (c + 1) % 2)
            else:
                scores(jnp.minimum(hh + 1, B_HEADS - 1), 0, 0)
            softmax_pv(hh, c, c % 2 if pipelined else 0)
        return carry

    if pipelined:
        scores(0, 0, 0)
    lax.fori_loop(0, B_HEADS, head, 0, unroll=max(1, min(B_HEADS, MLA_UNITS_PER_TRIP // ns)))

    @pl.when(j == nk - 1)
    def _finish():
        for hh in range(B_HEADS):
            o = acc_ref[hh, 0:B_V_DIM, :] * (1.0 / acc_ref[hh, B_V_DIM:B_V_DIM + 1, :])
            o_ref[:, hh * B_V_DIM:(hh + 1) * B_V_DIM] = o.T.astype(o_ref.dtype)


def _mla(qb, kb, vbt, prev, *, q_row0, n_q, k_row0, n_k, tq, tk):
    t_all = qb.shape[1]
    nq, nk = n_q // tq, n_k // tk
    qo, ko = q_row0 // tq, k_row0 // tk
    kern = functools.partial(_mla_kernel, nk=nk, tq=tq)
    return pl.pallas_call(
        kern,
        grid=(nq, nk),
        in_specs=[pl.BlockSpec((B_HEADS, tq, B_QK_PAD), lambda i, j: (0, qo + i, 0)),
                  pl.BlockSpec((B_HEADS, tk, B_QK_PAD), lambda i, j: (0, ko + j, 0)),
                  pl.BlockSpec((B_HEADS, B_VT_ROWS, tk), lambda i, j: (0, 0, ko + j)),
                  pl.BlockSpec(memory_space=pl.ANY)],
        out_specs=pl.BlockSpec((tq, B_WIDTH), lambda i, j: (qo + i, 0)),
        out_shape=jax.ShapeDtypeStruct((t_all, B_WIDTH), BF16),
        scratch_shapes=[pltpu.VMEM((B_HEADS, B_VT_ROWS, tq), F32), pltpu.VMEM((B_HEADS, 1, tq), F32),
                        pltpu.VMEM((2, tk, MLA_STRIP), F32)],
        input_output_aliases={3: 0},
        compiler_params=_params(("arbitrary", "arbitrary")),
        name="mla_attn",
    )(qb, kb, vbt, prev)


def _out_kernel(oa_ref, ob_ref, x_ref, mod_ref, ga_ref, gb_ref, wo_ref, pg_ref, o_ref, *, tm, n_lat):
    d = x_ref.shape[-1]
    is_ctx = _is_ctx_rows(pl.program_id(0), tm, n_lat)
    a = (_rms(oa_ref[...].astype(F32)) * ga_ref[...]).astype(BF16)
    b = (_rms(ob_ref[...].astype(F32)) * gb_ref[...]).astype(BF16)
    y = (jnp.dot(a, wo_ref[0:A_WIDTH, :], preferred_element_type=F32)
         + jnp.dot(b, wo_ref[A_WIDTH:A_WIDTH + B_WIDTH, :], preferred_element_type=F32))
    o_ref[...] = x_ref[...] + _mod_select(mod_ref, 2, d, is_ctx) * (_rms(y) * pg_ref[...])


def _attn_out(oa, ob, xs, mod, ga, gb, wo, pg, *, n_lat, tm):
    t_all, d = xs.shape
    row = lambda i: (i, 0)
    kern = functools.partial(_out_kernel, tm=tm, n_lat=n_lat)
    return pl.pallas_call(
        kern,
        grid=(t_all // tm,),
        in_specs=[pl.BlockSpec((tm, A_WIDTH), row), pl.BlockSpec((tm, B_WIDTH), row), pl.BlockSpec((tm, d), row),
                  _const_spec(mod.shape), _const_spec(ga.shape), _const_spec(gb.shape), _const_spec(wo.shape),
                  _const_spec(pg.shape)],
        out_specs=pl.BlockSpec((tm, d), row),
        out_shape=jax.ShapeDtypeStruct((t_all, d), F32),
        compiler_params=_params(("arbitrary",)),
        name="attn_out_mix",
    )(oa, ob, xs, mod, ga, gb, wo, pg)


def _swiglu_step(build_h, w1_ref, w3_ref, w2_ref, o_ref, h_ref, alongside=None):
    j = pl.program_id(1)

    @pl.when(j == 0)
    def _prologue():
        build_h()
        o_ref[...] = jnp.zeros(o_ref.shape, o_ref.dtype)

    if alongside is not None:
        alongside()
    h = h_ref[...]
    a = jnp.dot(h, w1_ref[...].astype(BF16), preferred_element_type=F32)
    b = jnp.dot(h, w3_ref[...].astype(BF16), preferred_element_type=F32)
    u = (a * (1.0 / (1.0 + jnp.exp(-a))) * b).astype(BF16)
    o_ref[...] += jnp.dot(u, w2_ref[...].astype(BF16), preferred_element_type=F32)


def _ffn_dense_kernel(x_ref, mod_ref, g_ref, pg_ref, w1_ref, w3_ref, w2_ref, o_ref, h_ref, gs_ref, acc_ref, *, tm,
                      n_lat, nj):
    tile_row0 = pl.program_id(0) * tm
    build_h = functools.partial(_norm_mod_rows, x_ref, g_ref, mod_ref, 3, tile_row0, n_lat, h_ref, gs_ref)
    _swiglu_step(build_h, w1_ref, w3_ref, w2_ref, acc_ref, h_ref)

    @pl.when(pl.program_id(1) == nj - 1)
    def _epilogue():
        _resid_rows(o_ref, acc_ref, x_ref, pg_ref, mod_ref, 5, tile_row0, n_lat, gs_ref)


def _ffn_dense(xs, mod, g, pg, w1, w3, w2, *, n_lat, tm, tf):
    t_all, d = xs.shape
    dff = w1.shape[1]
    nj = dff // tf
    kern = functools.partial(_ffn_dense_kernel, tm=tm, n_lat=n_lat, nj=nj)
    return pl.pallas_call(
        kern,
        grid=(t_all // tm, nj),
        in_specs=[pl.BlockSpec((tm, d), lambda i, j: (i, 0)), _const_spec(mod.shape), _const_spec(g.shape),
                  _const_spec(pg.shape),
                  pl.BlockSpec((d, tf), lambda i, j: (0, j)), pl.BlockSpec((d, tf), lambda i, j: (0, j)),
                  pl.BlockSpec((tf, d), lambda i, j: (j, 0))],
        out_specs=pl.BlockSpec((tm, d), lambda i, j: (i, 0)),
        out_shape=jax.ShapeDtypeStruct((t_all, d), F32),
        scratch_shapes=[pltpu.VMEM((tm, d), BF16), pltpu.VMEM((8, d), F32), pltpu.VMEM((tm, d), F32)],
        compiler_params=_params(("arbitrary", "arbitrary")),
        name="ffn_dense",
    )(xs, mod, g, pg, w1, w3, w2)


def _ffn_moe_kernel(te_ref, nu_ref, cur_ref, nxt_ref, x_hbm, w1_ref, w3_ref, w2_ref, o_ref, h_ref, xbuf, sem, *, tm,
                    nj):
    t, j = pl.program_id(0), pl.program_id(1)
    n_used = nu_ref[0]
    used = t < n_used
    slot = t % 2
    rows_per_step = tm // nj

    def row_copy(src_ref, r, s):
        return pltpu.make_async_copy(x_hbm.at[pl.ds(src_ref[0, 0, r], 1)], xbuf.at[s, pl.ds(r, 1)], sem.at[s])

    @pl.when((t == 0) & (j == 0))
    def _first_tile():
        lax.fori_loop(0, tm, lambda r, c: (row_copy(cur_ref, r, 0).start(), c)[1], 0, unroll=DMA_UNROLL)

    @pl.when((t <= n_used) & (j == 0))
    def _rows_arrived():
        lax.fori_loop(0, tm, lambda r, c: (row_copy(cur_ref, r, slot).wait(), c)[1], 0, unroll=DMA_UNROLL)

    @pl.when(used)
    def _used():
        def request_next_tile():
            for k in range(rows_per_step):
                row_copy(nxt_ref, j * rows_per_step + k, 1 - slot).start()

        def build_h():
            half = h_ref.shape[1] // 2
            lo, hi = _unpack_bf16_halves(xbuf[slot])
            h_ref[:, 0:half] = lo
            h_ref[:, half:2 * half] = hi

        _swiglu_step(build_h, w1_ref, w3_ref, w2_ref, o_ref, h_ref, alongside=request_next_tile)

    @pl.when(jnp.logical_not(used) & (j == 0))
    def _idle():
        o_ref[...] = jnp.zeros(o_ref.shape, o_ref.dtype)


def _ffn_moe(hp, slot_src, tile_expert, n_used, w1, w3, w2, *, tm, tf):
    d = 2 * hp.shape[1]
    n_tiles = slot_src.shape[0]
    dff = w1.shape[2]
    nj = dff // tf
    assert tm % nj == 0, "the next tile's row requests are spread evenly over the d_ff chunks"

    def t_eff(t, nu):
        return jnp.minimum(t, nu[0] - 1)

    def j_eff(t, j, nu):
        return jnp.where(t < nu[0], j, nj - 1)

    smem_rows = lambda fn: pl.BlockSpec((1, 1, tm), fn, memory_space=pltpu.SMEM)
    grid_spec = pltpu.PrefetchScalarGridSpec(
        num_scalar_prefetch=2,
        grid=(n_tiles, nj),
        in_specs=[smem_rows(lambda t, j, te, nu: (t, 0, 0)),
                  smem_rows(lambda t, j, te, nu: (jnp.minimum(t + 1, n_tiles - 1), 0, 0)),
                  pl.BlockSpec(memory_space=pl.ANY),
                  pl.BlockSpec((None, d, tf), lambda t, j, te, nu: (te[t_eff(t, nu)], 0, j_eff(t, j, nu))),
                  pl.BlockSpec((None, d, tf), lambda t, j, te, nu: (te[t_eff(t, nu)], 0, j_eff(t, j, nu))),
                  pl.BlockSpec((None, tf, d), lambda t, j, te, nu: (te[t_eff(t, nu)], j_eff(t, j, nu), 0))],
        out_specs=pl.BlockSpec((tm, d), lambda t, j, te, nu: (t, 0)),
        scratch_shapes=[pltpu.VMEM((tm, d), BF16), pltpu.VMEM((2, tm, d // 2), jnp.uint32),
                        pltpu.SemaphoreType.DMA((2,))],
    )
    return pl.pallas_call(
        functools.partial(_ffn_moe_kernel, tm=tm, nj=nj),
        grid_spec=grid_spec,
        out_shape=jax.ShapeDtypeStruct((n_tiles * tm, d), F32),
        compiler_params=_params(("arbitrary", "arbitrary")),
        name="ffn_moe",
    )(tile_expert, n_used, slot_src, slot_src, hp, w1, w3, w2)


def _pack_bf16_halves(h):
    half = h.shape[1] // 2
    bits = lax.bitcast_convert_type(h.astype(BF16).astype(F32), jnp.uint32)
    return (bits[:, 0:half] >> 16) | bits[:, half:2 * half]


def _unpack_bf16_halves(u):
    lo = lax.bitcast_convert_type(u << 16, F32).astype(BF16)
    hi = lax.bitcast_convert_type(u & jnp.uint32(0xFFFF0000), F32).astype(BF16)
    return lo, hi


def _bf16_part(x):
    return lax.bitcast_convert_type(lax.bitcast_convert_type(x, jnp.uint32) & jnp.uint32(0xFFFF0000), F32)


def _route_kernel(x_ref, mod_ref, g_ref, rw_ref, o_ref, hp_ref):
    d = x_ref.shape[-1]
    h = _rms(x_ref[...]) * g_ref[...] * (1.0 + mod_ref[0:1, 4 * d:5 * d]) + mod_ref[0:1, 3 * d:4 * d]
    hp_ref[...] = _pack_bf16_halves(h)
    h_top = _bf16_part(h)
    h_hi, h_lo = h_top.astype(BF16), (h - h_top).astype(BF16)
    hi = jnp.dot(h_hi, rw_ref[0], preferred_element_type=F32)
    logits = hi + pltpu.roll(hi, LANE - N_EXPERTS, 1) + jnp.dot(h_lo, rw_ref[1], preferred_element_type=F32)
    lane = lax.broadcasted_iota(jnp.int32, logits.shape, 1)
    lg = jnp.where(lane < N_EXPERTS, logits, -jnp.inf)
    m1 = jnp.max(lg, axis=-1, keepdims=True)
    i1 = jnp.min(jnp.where(lg == m1, lane, LANE), axis=-1, keepdims=True)
    lg2 = jnp.where(lane == i1, -jnp.inf, lg)
    m2 = jnp.max(lg2, axis=-1, keepdims=True)
    i2 = jnp.min(jnp.where(lg2 == m2, lane, LANE), axis=-1, keepdims=True)
    e = jnp.exp(m2 - m1)
    w1 = 1.0 / (1.0 + e)
    w2 = e * w1
    o_ref[...] = jnp.where(lane == 0, i1.astype(F32),
                           jnp.where(lane == 1, i2.astype(F32), jnp.where(lane == 2, w1, jnp.where(lane == 3, w2, 0.0))))


def _route(xs, mod, g, rw_pad, *, n_lat, tm):
    d = xs.shape[1]
    return pl.pallas_call(
        _route_kernel,
        grid=(n_lat // tm,),
        in_specs=[pl.BlockSpec((tm, d), lambda i: (i, 0)), _const_spec(mod.shape), _const_spec(g.shape),
                  _const_spec(rw_pad.shape)],
        out_specs=[pl.BlockSpec((tm, LANE), lambda i: (i, 0)), pl.BlockSpec((tm, d // 2), lambda i: (i, 0))],
        out_shape=[jax.ShapeDtypeStruct((n_lat, LANE), F32), jax.ShapeDtypeStruct((n_lat, d // 2), jnp.uint32)],
        compiler_params=_params(("arbitrary",)),
        name="moe_route",
    )(xs, mod, g, rw_pad)


def _combine_kernel(cur_ref, nxt_ref, rt_ref, x_ref, mod_ref, pg_ref, y_hbm, o_ref, buf, sem, *, tm, n_steps):
    i = pl.program_id(0)
    slot = i % 2

    def row_copy(dest_ref, r, k, s):
        return pltpu.make_async_copy(y_hbm.at[pl.ds(dest_ref[0, k, r], 1)], buf.at[s, k, pl.ds(r, 1)], sem.at[s])

    def start_tile0(r, c):
        row_copy(cur_ref, r, 0, 0).start()
        row_copy(cur_ref, r, 1, 0).start()
        return c

    def wait(r, c):
        row_copy(cur_ref, r, 0, slot).wait()
        row_copy(cur_ref, r, 1, slot).wait()
        return c

    @pl.when(i == 0)
    def _first_tile():
        lax.fori_loop(0, tm, start_tile0, 0, unroll=DMA_UNROLL)

    lax.fori_loop(0, tm, wait, 0, unroll=DMA_UNROLL)

    @pl.when(i < n_steps)
    def _tile():
        for r in range(tm):
            row_copy(nxt_ref, r, 0, 1 - slot).start()
            row_copy(nxt_ref, r, 1, 1 - slot).start()
        d = x_ref.shape[-1]
        rt = rt_ref[...]
        y = rt[:, 2:3] * buf[slot, 0] + rt[:, 3:4] * buf[slot, 1]
        o_ref[...] = x_ref[...] + mod_ref[0:1, 5 * d:6 * d] * (_rms(y) * pg_ref[...])


def _combine(yg, dest, route, xs, mod, pg, *, n_lat, tm):
    d = xs.shape[1]
    n_steps = n_lat // tm
    kern = functools.partial(_combine_kernel, tm=tm, n_steps=n_steps)
    tile = lambda i: jnp.minimum(i, n_steps - 1)
    nxt_tile = lambda i: jnp.minimum(i + 1, n_steps - 1)
    return pl.pallas_call(
        kern,
        grid=(n_steps + 1,),
        in_specs=[pl.BlockSpec((1, 2, tm), lambda i: (tile(i), 0, 0), memory_space=pltpu.SMEM),
                  pl.BlockSpec((1, 2, tm), lambda i: (nxt_tile(i), 0, 0), memory_space=pltpu.SMEM),
                  pl.BlockSpec((tm, LANE), lambda i: (tile(i), 0)), pl.BlockSpec((tm, d), lambda i: (tile(i), 0)),
                  _const_spec(mod.shape), _const_spec(pg.shape), pl.BlockSpec(memory_space=pl.ANY)],
        out_specs=pl.BlockSpec((tm, d), lambda i: (tile(i), 0)),
        out_shape=jax.ShapeDtypeStruct((n_lat, d), F32),
        scratch_shapes=[pltpu.VMEM((2, 2, tm, d), F32), pltpu.SemaphoreType.DMA((2,))],
        compiler_params=_params(("arbitrary",)),
        name="moe_combine",
    )(dest, dest, route, xs, mod, pg, yg)


def _rope_tables(n_lat, n_ctx):
    rows = n_lat // GRID_W

    def tabs(rot_dim):
        axis_dim = rot_dim // 2
        quarter = rot_dim // 4
        inv_freq = ROPE_BASE ** (-jnp.arange(0, axis_dim, 2, dtype=F32) / axis_dim)
        ang_r = jnp.arange(rows, dtype=F32)[:, None] * inv_freq[None, :]
        ang_c = jnp.arange(GRID_W, dtype=F32)[:, None] * inv_freq[None, :]

        def over_grid(fn):
            r = jnp.broadcast_to(fn(ang_r)[:, None, :], (rows, GRID_W, quarter))
            c = jnp.broadcast_to(fn(ang_c)[None, :, :], (rows, GRID_W, quarter))
            return jnp.concatenate([r, r, c, c], axis=-1).reshape(n_lat, rot_dim)

        cos, sin = over_grid(jnp.cos), over_grid(jnp.sin)
        lower = (jnp.arange(rot_dim) % (2 * quarter)) < quarter
        sin_m = jnp.where(lower[None, :], -sin, 0.0)
        sin_p = jnp.where(lower[None, :], 0.0, sin)
        pad = LANE - rot_dim
        cos = jnp.pad(cos, ((0, n_ctx), (0, pad)), constant_values=1.0)
        sin_m = jnp.pad(sin_m, ((0, n_ctx), (0, pad)))
        sin_p = jnp.pad(sin_p, ((0, n_ctx), (0, pad)))
        return [cos, sin_m, sin_p]

    return jnp.stack(tabs(A_HEAD_DIM) + tabs(B_ROPE_DIM))


def _layer_weights(i, w_in, w_uq, w_ukv, w_o):
    d = w_in.shape[1]
    q_w = A_WIDTH + B_Q_RANK
    wi = w_in[i]
    qa, cq = wi[:, :A_WIDTH], wi[:, A_WIDTH:q_w]
    o1, o2, o3 = q_w + A_KV_WIDTH, q_w + 2 * A_KV_WIDTH, q_w + 2 * A_KV_WIDTH + B_KV_RANK
    ka, va, ckv, kr = wi[:, q_w:o1], wi[:, o1:o2], wi[:, o2:o3], wi[:, o3:]
    win = jnp.concatenate([qa, ka, va, cq, ckv, kr, jnp.zeros((d, LANE - B_ROPE_DIM), F32)], axis=1).astype(BF16)
    uq = w_uq[i].reshape(B_Q_RANK, B_HEADS, B_NOPE_DIM + B_ROPE_DIM)
    uq = jnp.pad(uq, ((0, 0), (0, 0), (0, B_QK_PAD - B_NOPE_DIM - B_ROPE_DIM)))
    wuq = uq.reshape(B_Q_RANK, B_HEADS * B_QK_PAD).astype(BF16)
    ukv = w_ukv[i].reshape(B_KV_RANK, B_HEADS, B_NOPE_DIM + B_V_DIM)
    wukn = ukv[:, :, :B_NOPE_DIM].reshape(B_KV_RANK, B_HEADS * B_NOPE_DIM).astype(BF16)
    wuvt = ukv[:, :, B_NOPE_DIM:].reshape(B_KV_RANK, B_HEADS * B_V_DIM).T.astype(BF16)
    return win, wuq, wukn, wuvt, w_o[i].astype(BF16)


def _routing_slots(route, n_lat, tm_e, n_tiles):
    e12 = route[:, 0:2].astype(jnp.int32).T.reshape(-1)
    onehot = (e12[:, None] == jnp.arange(N_EXPERTS)[None, :]).astype(jnp.int32)
    csum = jnp.cumsum(onehot, axis=0)
    rank = jnp.sum(csum * onehot, axis=1) - 1
    counts = csum[-1]
    tiles_per = (counts + tm_e - 1) // tm_e
    tile_end = jnp.cumsum(tiles_per)
    start = (tile_end - tiles_per) * tm_e
    dest = (jnp.sum(start[None, :] * onehot, axis=1) + rank).astype(jnp.int32)
    n_used = tile_end[-1].astype(jnp.int32).reshape(1)
    t_idx = jnp.arange(n_tiles, dtype=jnp.int32)
    tile_expert = jnp.minimum(jnp.sum((t_idx[:, None] >= tile_end[None, :]).astype(jnp.int32), axis=1),
                              N_EXPERTS - 1).astype(jnp.int32)
    token = jnp.tile(jnp.arange(n_lat, dtype=jnp.int32), 2)
    slot_src = jnp.zeros((n_tiles * tm_e,), jnp.int32).at[dest].set(token, unique_indices=True)
    return dest.reshape(2, n_lat), slot_src.reshape(n_tiles, 1, tm_e), tile_expert, n_used


def _tile(n, candidates):
    for c in candidates:
        if n % c == 0:
            return c
    raise ValueError(f"no supported tile size for {n} rows")


def kernel(x, c, ctx, c_ctx, w_ada, b_ada, pre_attn_g, post_attn_g, pre_ffn_g, post_ffn_g, w_in, attn_sink, q_norm_g,
           kv_norm_g, w_uq, w_ukv, grp_a_g, grp_b_g, w_o, ffn_w1, ffn_w3, ffn_w2, router_w, moe_w1, moe_w3, moe_w2):
    b, n_lat, d = x.shape
    n_ctx = ctx.shape[1]
    depth = w_ada.shape[0]
    assert b == 1 and c.shape[0] == 1, "single-sequence kernel"
    t_all = n_lat + n_ctx
    tm = _tile(t_all, (640, 256, 128))
    tq_a = 256
    tq_b = _tile(n_lat, (2048, 1024, 512, 256))
    tk_b = _tile(t_all, (1280, 256))
    tf = 512
    tm_r = _tile(n_lat, (512, 256))
    tm_c = 256
    assert n_ctx == tq_a and n_lat % tq_a == 0 and n_lat >= tq_a + 2 * WINDOW

    xs = jnp.concatenate([x[0], ctx[0]], axis=0)
    cc = jnp.stack([c[0], c_ctx])
    mods = _ada(cc, w_ada, b_ada)
    rope_tab = _rope_tables(n_lat, n_ctx)
    row2 = lambda v: v.reshape(1, -1)

    for i in range(depth):
        last = i == depth - 1
        mod = mods[i]
        win, wuq, wukn, wuvt, wo = _layer_weights(i, w_in, w_uq, w_ukv, w_o)
        qa, ka, va, qb, kb, vbt = _proj(xs, mod, row2(pre_attn_g[i]), win, row2(q_norm_g[i]), row2(kv_norm_g[i]),
                                        wuq, wukn, wuvt, rope_tab, n_lat=n_lat, tm=tm)
        sink_tab = jnp.broadcast_to((attn_sink[i] * LOG2E)[:, None], (A_HEADS, tq_a))
        oa = _attn_a(qa, ka, va, sink_tab, n_lat=n_lat, n_ctx=n_ctx, tq=tq_a)
        ob = _mla(qb, kb, vbt, jnp.zeros((t_all, B_WIDTH), BF16), q_row0=0, n_q=n_lat, k_row0=0, n_k=t_all,
                  tq=tq_b, tk=tk_b)
        ob = _mla(qb, kb, vbt, ob, q_row0=n_lat, n_q=n_ctx, k_row0=n_lat, n_k=n_ctx, tq=n_ctx, tk=n_ctx)
        xs = _attn_out(oa, ob, xs, mod, row2(grp_a_g[i]), row2(grp_b_g[i]), wo, row2(post_attn_g[i]),
                       n_lat=n_lat, tm=tm)
        jj = i // 2
        if i % 2 == 0:
            xs = _ffn_dense(xs, mod, row2(pre_ffn_g[i]), row2(post_ffn_g[i]), ffn_w1[jj].astype(BF16),
                            ffn_w3[jj].astype(BF16), ffn_w2[jj].astype(BF16), n_lat=n_lat, tm=tm, tf=tf)
        else:
            n_rows = n_lat
            assert last, "expert layers other than the last would also need the context rows routed"
            nj = moe_w1.shape[-1] // tf
            tm_e = -(-tm // (nj * ROW_CHUNK)) * ROW_CHUNK * nj
            n_tiles = (2 * n_rows + N_EXPERTS * (tm_e - 1)) // tm_e + 1
            rw_top = _bf16_part(router_w[jj])
            rw_hi, rw_lo = rw_top.astype(BF16), (router_w[jj] - rw_top).astype(BF16)
            zeros = jnp.zeros((d, LANE - 2 * N_EXPERTS), BF16)
            rw_pad = jnp.stack([jnp.concatenate([rw_hi, rw_lo, zeros], axis=1),
                                jnp.concatenate([rw_hi, jnp.zeros_like(rw_lo), zeros], axis=1)])
            route, hp = _route(xs, mod, row2(pre_ffn_g[i]), rw_pad, n_lat=n_rows, tm=tm_r)
            dest, slot_src, tile_expert, n_used = _routing_slots(route, n_rows, tm_e, n_tiles)
            dest_t = dest.reshape(2, n_rows // tm_c, tm_c).transpose(1, 0, 2)
            yg = _ffn_moe(hp, slot_src, tile_expert, n_used, moe_w1[jj], moe_w3[jj], moe_w2[jj], tm=tm_e, tf=tf)
            xs = _combine(yg, dest_t, route, xs, mod, row2(post_ffn_g[i]), n_lat=n_rows, tm=tm_c)
    return xs[:n_lat].reshape(b, n_lat, d)
```

```python
import functools
import math

import jax
import jax.numpy as jnp
from jax import lax
from jax.experimental import pallas as pl
from jax.experimental.pallas import tpu as pltpu

GRID_W = 64
EPS = 1e-6
NEG_INF = -1e30
ROPE_BASE = 10000.0
A_HEADS = 8
A_KV_HEADS = 2
A_HEAD_DIM = 128
WINDOW = 128
B_HEADS = 8
B_Q_RANK = 384
B_KV_RANK = 256
B_NOPE_DIM = 128
B_ROPE_DIM = 64
B_V_DIM = 128
N_EXPERTS = 8
A_WIDTH = A_HEADS * A_HEAD_DIM
A_KV_WIDTH = A_KV_HEADS * A_HEAD_DIM
B_WIDTH = B_HEADS * B_V_DIM
B_QK_PAD = 256
B_VT_ROWS = B_V_DIM + 16

LOG2E = math.log2(math.e)
LANE = 128
VMEM_LIMIT_BYTES = 60 * 1024 * 1024

F32 = jnp.float32
BF16 = jnp.bfloat16
NT_DIMS = (((1,), (1,)), ((), ()))


def _params(sem, vmem=VMEM_LIMIT_BYTES):
    return pltpu.CompilerParams(dimension_semantics=sem, vmem_limit_bytes=vmem)


def _const_spec(shape):
    nd = len(shape)
    return pl.BlockSpec(shape, lambda *_: (0,) * nd, pipeline_mode=pl.Buffered(1))


def _rms(x):
    return x * lax.rsqrt(jnp.mean(x * x, axis=-1, keepdims=True) + EPS)


DMA_UNROLL = 8
ROW_CHUNK = 16
ROW_UNROLL = 8


def _row_chunks(tm, body):
    def step(r, carry):
        body(pl.ds(pl.multiple_of(r * ROW_CHUNK, ROW_CHUNK), ROW_CHUNK), r * ROW_CHUNK)
        return carry

    lax.fori_loop(0, tm // ROW_CHUNK, step, 0, unroll=ROW_UNROLL)


def _mod_row(tile_row0, r0, n_lat):
    return jnp.where(tile_row0 + r0 >= n_lat, 1, 0)


def _norm_mod_rows(x_ref, g_ref, mod_ref, k_shift, tile_row0, n_lat, h_ref, gs_ref):
    tm, d = x_ref.shape
    gs_ref[...] = g_ref[...] * (1.0 + mod_ref[:, (k_shift + 1) * d:(k_shift + 2) * d])

    def body(rows, r0):
        mrow = _mod_row(tile_row0, r0, n_lat)
        h = _rms(x_ref[rows, :]) * gs_ref[pl.ds(mrow, 1), :]
        h_ref[rows, :] = (h + mod_ref[pl.ds(mrow, 1), k_shift * d:(k_shift + 1) * d]).astype(BF16)

    _row_chunks(tm, body)


def _resid_rows(o_ref, y_ref, x_ref, pg_ref, mod_ref, k_gate, tile_row0, n_lat, gs_ref):
    tm, d = x_ref.shape
    gs_ref[...] = pg_ref[...] * mod_ref[:, k_gate * d:(k_gate + 1) * d]

    def body(rows, r0):
        mrow = _mod_row(tile_row0, r0, n_lat)
        o_ref[rows, :] = x_ref[rows, :] + gs_ref[pl.ds(mrow, 1), :] * _rms(y_ref[rows, :])

    _row_chunks(tm, body)


def _is_ctx_rows(tile_index, tm, n_lat):
    row = tile_index * tm + lax.broadcasted_iota(jnp.int32, (tm, 1), 0)
    return row >= n_lat


def _mod_select(mod_ref, k, d, is_ctx):
    return jnp.where(is_ctx, mod_ref[1:2, k * d:(k + 1) * d], mod_ref[0:1, k * d:(k + 1) * d])


ADA_ROWS = 2


def _ada_kernel(c_ref, w_ref, b_ref, o_ref):
    tn = w_ref.shape[-1]
    o_ref[...] = jnp.zeros(o_ref.shape, o_ref.dtype)
    for r in range(ADA_ROWS):
        c = c_ref[r]
        a = c * (1.0 / (1.0 + jnp.exp(-c)))
        for l in range(tn // LANE):
            cols = slice(l * LANE, (l + 1) * LANE)
            o_ref[r:r + 1, cols] = _col_sum(a * w_ref[:, cols]) + b_ref[:, cols]


def _ada(cc, w_ada, b_ada):
    depth, d, n = w_ada.shape
    tn = n // 8
    c_lanes = jnp.broadcast_to(cc[:, :, None], (ADA_ROWS, d, LANE))
    return pl.pallas_call(
        _ada_kernel,
        grid=(depth, n // tn),
        in_specs=[pl.BlockSpec((ADA_ROWS, d, LANE), lambda l, j: (0, 0, 0)),
                  pl.BlockSpec((None, d, tn), lambda l, j: (l, 0, j)),
                  pl.BlockSpec((None, 1, tn), lambda l, j: (l, 0, j))],
        out_specs=pl.BlockSpec((None, 8, tn), lambda l, j: (l, 0, j)),
        out_shape=jax.ShapeDtypeStruct((depth, 8, n), F32),
        compiler_params=_params(("arbitrary", "arbitrary")),
        name="ada_mod",
    )(c_lanes, w_ada, b_ada.reshape(depth, 1, n))


def _rope(t, cos, sin_m, sin_p, quarter):
    n = t.shape[-1]
    return t * cos + pltpu.roll(t, n - quarter, 1) * sin_m + pltpu.roll(t, quarter, 1) * sin_p


def _proj_kernel(x_ref, mod_ref, g_ref, win_ref, qg_ref, kvg_ref, wuq_ref, wukn_ref, wuvt_ref, rope_ref,
                 qa_ref, ka_ref, va_ref, qb_ref, kb_ref, vbt_ref, *, tm, n_lat, scale_a, scale_b):
    d = x_ref.shape[-1]
    is_ctx = _is_ctx_rows(pl.program_id(0), tm, n_lat)
    h = _rms(x_ref[...]) * g_ref[...] * (1.0 + _mod_select(mod_ref, 1, d, is_ctx)) + _mod_select(mod_ref, 0, d, is_ctx)
    proj = jnp.dot(h.astype(BF16), win_ref[...], preferred_element_type=F32)
    cos_a, sinm_a, sinp_a = rope_ref[0], rope_ref[1], rope_ref[2]
    cos_b, sinm_b, sinp_b = rope_ref[3], rope_ref[4], rope_ref[5]
    qa_q = A_HEAD_DIM // 4
    b_q = B_ROPE_DIM // 4
    cos_as, sinm_as, sinp_as = cos_a * scale_a, sinm_a * scale_a, sinp_a * scale_a
    for hh in range(A_HEADS):
        t = proj[:, hh * LANE:(hh + 1) * LANE]
        qa_ref[:, hh * LANE:(hh + 1) * LANE] = _rope(t, cos_as, sinm_as, sinp_as, qa_q).astype(BF16)
    o = A_WIDTH
    for hh in range(A_KV_HEADS):
        t = proj[:, o + hh * LANE:o + (hh + 1) * LANE]
        ka_ref[:, hh * LANE:(hh + 1) * LANE] = _rope(t, cos_a, sinm_a, sinp_a, qa_q).astype(BF16)
    o += A_KV_WIDTH
    va_ref[...] = proj[:, o:o + A_KV_WIDTH].astype(BF16)
    o += A_KV_WIDTH
    cq = (_rms(proj[:, o:o + B_Q_RANK]) * qg_ref[...]).astype(BF16)
    o += B_Q_RANK
    qm = jnp.dot(cq, wuq_ref[...], preferred_element_type=F32)
    cos_bs, sinm_bs, sinp_bs = cos_b * scale_b, sinm_b * scale_b, sinp_b * scale_b
    for hh in range(B_HEADS):
        b0 = hh * B_QK_PAD
        qb_ref[hh, :, 0:LANE] = (qm[:, b0:b0 + LANE] * scale_b).astype(BF16)
        qb_ref[hh, :, LANE:2 * LANE] = _rope(qm[:, b0 + LANE:b0 + 2 * LANE], cos_bs, sinm_bs, sinp_bs, b_q).astype(BF16)
    ckv = (_rms(proj[:, o:o + B_KV_RANK]) * kvg_ref[...]).astype(BF16)
    o += B_KV_RANK
    kr = _rope(proj[:, o:o + LANE], cos_b, sinm_b, sinp_b, b_q).astype(BF16)
    kn = jnp.dot(ckv, wukn_ref[...], preferred_element_type=F32)
    vt = lax.dot_general(wuvt_ref[...], ckv, NT_DIMS, preferred_element_type=F32)
    for hh in range(B_HEADS):
        kb_ref[hh, :, 0:LANE] = kn[:, hh * LANE:(hh + 1) * LANE].astype(BF16)
        kb_ref[hh, :, LANE:2 * LANE] = kr
        vbt_ref[hh, 0:B_V_DIM, :] = vt[hh * B_V_DIM:(hh + 1) * B_V_DIM, :].astype(BF16)
        vbt_ref[hh, B_V_DIM:B_VT_ROWS, :] = jnp.ones((B_VT_ROWS - B_V_DIM, tm), BF16)


def _proj(xs, mod, g, win, qg, kvg, wuq, wukn, wuvt, rope_tab, *, n_lat, tm):
    t_all, d = xs.shape
    row = lambda i: (i, 0)
    kern = functools.partial(_proj_kernel, tm=tm, n_lat=n_lat, scale_a=LOG2E / math.sqrt(A_HEAD_DIM),
                             scale_b=LOG2E / math.sqrt(B_NOPE_DIM + B_ROPE_DIM))
    return pl.pallas_call(
        kern,
        grid=(t_all // tm,),
        in_specs=[pl.BlockSpec((tm, d), row), _const_spec(mod.shape), _const_spec(g.shape), _const_spec(win.shape),
                  _const_spec(qg.shape), _const_spec(kvg.shape), _const_spec(wuq.shape), _const_spec(wukn.shape),
                  _const_spec(wuvt.shape), pl.BlockSpec((6, tm, LANE), lambda i: (0, i, 0))],
        out_specs=[pl.BlockSpec((tm, A_WIDTH), row), pl.BlockSpec((tm, A_KV_WIDTH), row),
                   pl.BlockSpec((tm, A_KV_WIDTH), row),
                   pl.BlockSpec((B_HEADS, tm, B_QK_PAD), lambda i: (0, i, 0)),
                   pl.BlockSpec((B_HEADS, tm, B_QK_PAD), lambda i: (0, i, 0)),
                   pl.BlockSpec((B_HEADS, B_VT_ROWS, tm), lambda i: (0, 0, i))],
        out_shape=[jax.ShapeDtypeStruct((t_all, A_WIDTH), BF16), jax.ShapeDtypeStruct((t_all, A_KV_WIDTH), BF16),
                   jax.ShapeDtypeStruct((t_all, A_KV_WIDTH), BF16),
                   jax.ShapeDtypeStruct((B_HEADS, t_all, B_QK_PAD), BF16),
                   jax.ShapeDtypeStruct((B_HEADS, t_all, B_QK_PAD), BF16),
                   jax.ShapeDtypeStruct((B_HEADS, B_VT_ROWS, t_all), BF16)],
        compiler_params=_params(("arbitrary",)),
        name="pre_attn_proj",
    )(xs, mod, g, win, qg, kvg, wuq, wukn, wuvt, rope_tab)


def _attn_a_kernel(q_ref, k_ref, v_ref, sink_ref, o_ref, s_ref, *, tq, n_lat, n_ctx):
    i = pl.program_id(0)
    n_lat_tiles = n_lat // tq
    win = tq + 2 * WINDOW
    grp = A_HEADS // A_KV_HEADS
    tn_dims = (((0,), (0,)), ((), ()))

    def run_tile(keys, vals, mask):
        n_keys = keys[0].shape[0]

        def scores(hh, slot):
            q = q_ref[:, hh * A_HEAD_DIM:(hh + 1) * A_HEAD_DIM]
            s_ref[slot, 0:n_keys, :] = lax.dot_general(keys[hh // grp], q, NT_DIMS, preferred_element_type=F32)

        def head_out(hh, slot):
            s = s_ref[slot, 0:n_keys, :]
            if mask is not None:
                s = jnp.concatenate([jnp.where(mask, s[0:win], NEG_INF), s[win:n_keys]], axis=0)
            sink = sink_ref[hh:hh + 1, :]
            m = jnp.maximum(_col_max(s), sink)
            p = jnp.exp2(s - m)
            den = _col_sum(p) + jnp.exp2(sink - m)
            o = lax.dot_general(vals[hh // grp], p.astype(BF16), tn_dims, preferred_element_type=F32)
            o_ref[:, hh * A_HEAD_DIM:(hh + 1) * A_HEAD_DIM] = (o * (1.0 / den)).T.astype(o_ref.dtype)

        ahead = s_ref.shape[0] - 1
        for hh in range(min(ahead, A_HEADS)):
            scores(hh, hh)
        for hh in range(A_HEADS):
            if hh + ahead < A_HEADS:
                scores(hh + ahead, (hh + ahead) % (ahead + 1))
            head_out(hh, hh % (ahead + 1))

    def kv_head(x, g):
        return x[:, g * A_HEAD_DIM:(g + 1) * A_HEAD_DIM]

    kc = k_ref[n_lat:n_lat + n_ctx, :]
    vc = v_ref[n_lat:n_lat + n_ctx, :]

    @pl.when(i < n_lat_tiles)
    def _latent():
        q0 = i * tq
        ws = pl.multiple_of(jnp.clip(q0 - WINDOW, 0, n_lat - win), WINDOW)
        kw = k_ref[pl.ds(ws, win), :]
        vw = v_ref[pl.ds(ws, win), :]
        kpos = ws + lax.broadcasted_iota(jnp.int32, (win, tq), 0)
        qpos = q0 + lax.broadcasted_iota(jnp.int32, (win, tq), 1)
        mask = jnp.abs(qpos - kpos) <= WINDOW
        keys = [jnp.concatenate([kv_head(kw, g), kv_head(kc, g)], axis=0) for g in range(A_KV_HEADS)]
        vals = [jnp.concatenate([kv_head(vw, g), kv_head(vc, g)], axis=0) for g in range(A_KV_HEADS)]
        run_tile(keys, vals, mask)

    @pl.when(i >= n_lat_tiles)
    def _context():
        run_tile([kv_head(kc, g) for g in range(A_KV_HEADS)], [kv_head(vc, g) for g in range(A_KV_HEADS)], None)


def _attn_a(qa, ka, va, sink_tab, *, n_lat, n_ctx, tq):
    t_all = qa.shape[0]
    kern = functools.partial(_attn_a_kernel, tq=tq, n_lat=n_lat, n_ctx=n_ctx)
    return pl.pallas_call(
        kern,
        grid=(t_all // tq,),
        in_specs=[pl.BlockSpec((tq, A_WIDTH), lambda i: (i, 0)), _const_spec(ka.shape), _const_spec(va.shape),
                  _const_spec(sink_tab.shape)],
        out_specs=pl.BlockSpec((tq, A_WIDTH), lambda i: (i, 0)),
        out_shape=jax.ShapeDtypeStruct((t_all, A_WIDTH), BF16),
        scratch_shapes=[pltpu.VMEM((4, tq + 2 * WINDOW + n_ctx, tq), F32)],
        compiler_params=_params(("arbitrary",)),
        name="attn_a",
    )(qa, ka, va, sink_tab)


MLA_STRIP = 256
MLA_UNITS_PER_TRIP = 32


def _col_fold(st, pair_op):
    rows = st.shape[0]
    groups = 4 if rows % 32 == 0 else 1
    part = st[0:rows // groups]
    for g in range(1, groups):
        part = pair_op(part, st[g * (rows // groups):(g + 1) * (rows // groups)])
    return part


def _col_max(st):
    return jnp.max(_col_fold(st, jnp.maximum), axis=0, keepdims=True)


def _col_sum(st):
    return jnp.sum(_col_fold(st, jnp.add), axis=0, keepdims=True)


def _mla_kernel(q_ref, k_ref, vt_ref, prev_hbm, o_ref, acc_ref, m_ref, s_ref, *, nk, tq):
    del prev_hbm
    j = pl.program_id(1)
    ns = tq // MLA_STRIP

    @pl.when(j == 0)
    def _init():
        m_ref[...] = jnp.full(m_ref.shape, NEG_INF, F32)
        acc_ref[...] = jnp.zeros(acc_ref.shape, F32)

    tk = k_ref.shape[1]
    k_half = tk // 2
    v_half = (tk // MLA_STRIP + 1) // 2 * MLA_STRIP if tk > MLA_STRIP else tk

    def scores(hh, c, slot):
        q = q_ref[hh, c * MLA_STRIP:(c + 1) * MLA_STRIP, :]
        for r0, r1 in ((0, k_half), (k_half, tk)):
            s_ref[slot, r0:r1, :] = lax.dot_general(k_ref[hh, r0:r1, :], q, NT_DIMS, preferred_element_type=F32)

    def softmax_pv(hh, c, slot):
        cs = slice(c * MLA_STRIP, (c + 1) * MLA_STRIP)
        st = s_ref[slot]
        m_old = m_ref[hh, :, cs]
        m_new = jnp.maximum(m_old, _col_max(st))
        alpha = jnp.exp2(m_old - m_new)
        p = jnp.exp2(st - m_new).astype(BF16)
        pv = jnp.dot(vt_ref[hh, :, 0:v_half], p[0:v_half], preferred_element_type=F32)
        if v_half < tk:
            pv = pv + jnp.dot(vt_ref[hh, :, v_half:tk], p[v_half:tk], preferred_element_type=F32)
        acc_ref[hh, :, cs] = alpha * acc_ref[hh, :, cs] + pv
        m_ref[hh, :, cs] = m_new

    pipelined = ns % 2 == 0

    def head(hh, carry):
        for c in range(ns):
            if not pipelined:
                scores(hh, c, 0)
            elif c + 1 < ns:
                scores(hh, c + 1, (c + 1) % 2)
            else:
                scores(jnp.minimum(hh + 1, B_HEADS - 1), 0, 0)
            softmax_pv(hh, c, c % 2 if pipelined else 0)
        return carry

    if pipelined:
        scores(0, 0, 0)
    lax.fori_loop(0, B_HEADS, head, 0, unroll=max(1, min(B_HEADS, MLA_UNITS_PER_TRIP // ns)))

    @pl.when(j == nk - 1)
    def _finish():
        for hh in range(B_HEADS):
            o = acc_ref[hh, 0:B_V_DIM, :] * (1.0 / acc_ref[hh, B_V_DIM:B_V_DIM + 1, :])
            o_ref[:, hh * B_V_DIM:(hh + 1) * B_V_DIM] = o.T.astype(o_ref.dtype)


def _mla(qb, kb, vbt, prev, *, q_row0, n_q, k_row0, n_k, tq, tk):
    t_all = qb.shape[1]
    nq, nk = n_q // tq, n_k // tk
    qo, ko = q_row0 // tq, k_row0 // tk
    kern = functools.partial(_mla_kernel, nk=nk, tq=tq)
    return pl.pallas_call(
        kern,
        grid=(nq, nk),
        in_specs=[pl.BlockSpec((B_HEADS, tq, B_QK_PAD), lambda i, j: (0, qo + i, 0)),
                  pl.BlockSpec((B_HEADS, tk, B_QK_PAD), lambda i, j: (0, ko + j, 0)),
                  pl.BlockSpec((B_HEADS, B_VT_ROWS, tk), lambda i, j: (0, 0, ko + j)),
                  pl.BlockSpec(memory_space=pl.ANY)],
        out_specs=pl.BlockSpec((tq, B_WIDTH), lambda i, j: (qo + i, 0)),
        out_shape=jax.ShapeDtypeStruct((t_all, B_WIDTH), BF16),
        scratch_shapes=[pltpu.VMEM((B_HEADS, B_VT_ROWS, tq), F32), pltpu.VMEM((B_HEADS, 1, tq), F32),
                        pltpu.VMEM((2, tk, MLA_STRIP), F32)],
        input_output_aliases={3: 0},
        compiler_params=_params(("arbitrary", "arbitrary")),
        name="mla_attn",
    )(qb, kb, vbt, prev)


def _out_kernel(oa_ref, ob_ref, x_ref, mod_ref, ga_ref, gb_ref, wo_ref, pg_ref, o_ref, *, tm, n_lat):
    d = x_ref.shape[-1]
    is_ctx = _is_ctx_rows(pl.program_id(0), tm, n_lat)
    a = (_rms(oa_ref[...].astype(F32)) * ga_ref[...]).astype(BF16)
    b = (_rms(ob_ref[...].astype(F32)) * gb_ref[...]).astype(BF16)
    y = (jnp.dot(a, wo_ref[0:A_WIDTH, :], preferred_element_type=F32)
         + jnp.dot(b, wo_ref[A_WIDTH:A_WIDTH + B_WIDTH, :], preferred_element_type=F32))
    o_ref[...] = x_ref[...] + _mod_select(mod_ref, 2, d, is_ctx) * (_rms(y) * pg_ref[...])


def _attn_out(oa, ob, xs, mod, ga, gb, wo, pg, *, n_lat, tm):
    t_all, d = xs.shape
    row = lambda i: (i, 0)
    kern = functools.partial(_out_kernel, tm=tm, n_lat=n_lat)
    return pl.pallas_call(
        kern,
        grid=(t_all // tm,),
        in_specs=[pl.BlockSpec((tm, A_WIDTH), row), pl.BlockSpec((tm, B_WIDTH), row), pl.BlockSpec((tm, d), row),
                  _const_spec(mod.shape), _const_spec(ga.shape), _const_spec(gb.shape), _const_spec(wo.shape),
                  _const_spec(pg.shape)],
        out_specs=pl.BlockSpec((tm, d), row),
        out_shape=jax.ShapeDtypeStruct((t_all, d), F32),
        compiler_params=_params(("arbitrary",)),
        name="attn_out_mix",
    )(oa, ob, xs, mod, ga, gb, wo, pg)


def _swiglu_step(build_h, w1_ref, w3_ref, w2_ref, o_ref, h_ref, alongside=None):
    j = pl.program_id(1)

    @pl.when(j == 0)
    def _prologue():
        build_h()
        o_ref[...] = jnp.zeros(o_ref.shape, o_ref.dtype)

    if alongside is not None:
        alongside()
    h = h_ref[...]
    a = jnp.dot(h, w1_ref[...].astype(BF16), preferred_element_type=F32)
    b = jnp.dot(h, w3_ref[...].astype(BF16), preferred_element_type=F32)
    u = (a * (1.0 / (1.0 + jnp.exp(-a))) * b).astype(BF16)
    o_ref[...] += jnp.dot(u, w2_ref[...].astype(BF16), preferred_element_type=F32)


def _ffn_dense_kernel(x_ref, mod_ref, g_ref, pg_ref, w1_ref, w3_ref, w2_ref, o_ref, h_ref, gs_ref, acc_ref, *, tm,
                      n_lat, nj):
    tile_row0 = pl.program_id(0) * tm
    build_h = functools.partial(_norm_mod_rows, x_ref, g_ref, mod_ref, 3, tile_row0, n_lat, h_ref, gs_ref)
    _swiglu_step(build_h, w1_ref, w3_ref, w2_ref, acc_ref, h_ref)

    @pl.when(pl.program_id(1) == nj - 1)
    def _epilogue():
        _resid_rows(o_ref, acc_ref, x_ref, pg_ref, mod_ref, 5, tile_row0, n_lat, gs_ref)


def _ffn_dense(xs, mod, g, pg, w1, w3, w2, *, n_lat, tm, tf):
    t_all, d = xs.shape
    dff = w1.shape[1]
    nj = dff // tf
    kern = functools.partial(_ffn_dense_kernel, tm=tm, n_lat=n_lat, nj=nj)
    return pl.pallas_call(
        kern,
        grid=(t_all // tm, nj),
        in_specs=[pl.BlockSpec((tm, d), lambda i, j: (i, 0)), _const_spec(mod.shape), _const_spec(g.shape),
                  _const_spec(pg.shape),
                  pl.BlockSpec((d, tf), lambda i, j: (0, j)), pl.BlockSpec((d, tf), lambda i, j: (0, j)),
                  pl.BlockSpec((tf, d), lambda i, j: (j, 0))],
        out_specs=pl.BlockSpec((tm, d), lambda i, j: (i, 0)),
        out_shape=jax.ShapeDtypeStruct((t_all, d), F32),
        scratch_shapes=[pltpu.VMEM((tm, d), BF16), pltpu.VMEM((8, d), F32), pltpu.VMEM((tm, d), F32)],
        compiler_params=_params(("arbitrary", "arbitrary")),
        name="ffn_dense",
    )(xs, mod, g, pg, w1, w3, w2)


def _ffn_moe_kernel(te_ref, nu_ref, cur_ref, nxt_ref, x_hbm, w1_ref, w3_ref, w2_ref, o_ref, h_ref, xbuf, sem, *, tm,
                    nj):
    t, j = pl.program_id(0), pl.program_id(1)
    n_used = nu_ref[0]
    used = t < n_used
    slot = t % 2
    rows_per_step = tm // nj

    def row_copy(src_ref, r, s):
        return pltpu.make_async_copy(x_hbm.at[pl.ds(src_ref[0, 0, r], 1)], xbuf.at[s, pl.ds(r, 1)], sem.at[s])

    @pl.when((t == 0) & (j == 0))
    def _first_tile():
        lax.fori_loop(0, tm, lambda r, c: (row_copy(cur_ref, r, 0).start(), c)[1], 0, unroll=DMA_UNROLL)

    @pl.when((t <= n_used) & (j == 0))
    def _rows_arrived():
        lax.fori_loop(0, tm, lambda r, c: (row_copy(cur_ref, r, slot).wait(), c)[1], 0, unroll=DMA_UNROLL)

    @pl.when(used)
    def _used():
        def request_next_tile():
            for k in range(rows_per_step):
                row_copy(nxt_ref, j * rows_per_step + k, 1 - slot).start()

        def build_h():
            half = h_ref.shape[1] // 2
            lo, hi = _unpack_bf16_halves(xbuf[slot])
            h_ref[:, 0:half] = lo
            h_ref[:, half:2 * half] = hi

        _swiglu_step(build_h, w1_ref, w3_ref, w2_ref, o_ref, h_ref, alongside=request_next_tile)

    @pl.when(jnp.logical_not(used) & (j == 0))
    def _idle():
        o_ref[...] = jnp.zeros(o_ref.shape, o_ref.dtype)


def _ffn_moe(hp, slot_src, tile_expert, n_used, w1, w3, w2, *, tm, tf):
    d = 2 * hp.shape[1]
    n_tiles = slot_src.shape[0]
    dff = w1.shape[2]
    nj = dff // tf
    assert tm % nj == 0, "the next tile's row requests are spread evenly over the d_ff chunks"

    def t_eff(t, nu):
        return jnp.minimum(t, nu[0] - 1)

    def j_eff(t, j, nu):
        return jnp.where(t < nu[0], j, nj - 1)

    smem_rows = lambda fn: pl.BlockSpec((1, 1, tm), fn, memory_space=pltpu.SMEM)
    grid_spec = pltpu.PrefetchScalarGridSpec(
        num_scalar_prefetch=2,
        grid=(n_tiles, nj),
        in_specs=[smem_rows(lambda t, j, te, nu: (t, 0, 0)),
                  smem_rows(lambda t, j, te, nu: (jnp.minimum(t + 1, n_tiles - 1), 0, 0)),
                  pl.BlockSpec(memory_space=pl.ANY),
                  pl.BlockSpec((None, d, tf), lambda t, j, te, nu: (te[t_eff(t, nu)], 0, j_eff(t, j, nu))),
                  pl.BlockSpec((None, d, tf), lambda t, j, te, nu: (te[t_eff(t, nu)], 0, j_eff(t, j, nu))),
                  pl.BlockSpec((None, tf, d), lambda t, j, te, nu: (te[t_eff(t, nu)], j_eff(t, j, nu), 0))],
        out_specs=pl.BlockSpec((tm, d), lambda t, j, te, nu: (t, 0)),
        scratch_shapes=[pltpu.VMEM((tm, d), BF16), pltpu.VMEM((2, tm, d // 2), jnp.uint32),
                        pltpu.SemaphoreType.DMA((2,))],
    )
    return pl.pallas_call(
        functools.partial(_ffn_moe_kernel, tm=tm, nj=nj),
        grid_spec=grid_spec,
        out_shape=jax.ShapeDtypeStruct((n_tiles * tm, d), F32),
        compiler_params=_params(("arbitrary", "arbitrary")),
        name="ffn_moe",
    )(tile_expert, n_used, slot_src, slot_src, hp, w1, w3, w2)


def _pack_bf16_halves(h):
    half = h.shape[1] // 2
    bits = lax.bitcast_convert_type(h.astype(BF16).astype(F32), jnp.uint32)
    return (bits[:, 0:half] >> 16) | bits[:, half:2 * half]


def _unpack_bf16_halves(u):
    lo = lax.bitcast_convert_type(u << 16, F32).astype(BF16)
    hi = lax.bitcast_convert_type(u & jnp.uint32(0xFFFF0000), F32).astype(BF16)
    return lo, hi


def _bf16_part(x):
    return lax.bitcast_convert_type(lax.bitcast_convert_type(x, jnp.uint32) & jnp.uint32(0xFFFF0000), F32)


def _route_kernel(x_ref, mod_ref, g_ref, rw_ref, o_ref, hp_ref):
    d = x_ref.shape[-1]
    h = _rms(x_ref[...]) * g_ref[...] * (1.0 + mod_ref[0:1, 4 * d:5 * d]) + mod_ref[0:1, 3 * d:4 * d]
    hp_ref[...] = _pack_bf16_halves(h)
    h_top = _bf16_part(h)
    h_hi, h_lo = h_top.astype(BF16), (h - h_top).astype(BF16)
    hi = jnp.dot(h_hi, rw_ref[0], preferred_element_type=F32)
    logits = hi + pltpu.roll(hi, LANE - N_EXPERTS, 1) + jnp.dot(h_lo, rw_ref[1], preferred_element_type=F32)
    lane = lax.broadcasted_iota(jnp.int32, logits.shape, 1)
    lg = jnp.where(lane < N_EXPERTS, logits, -jnp.inf)
    m1 = jnp.max(lg, axis=-1, keepdims=True)
    i1 = jnp.min(jnp.where(lg == m1, lane, LANE), axis=-1, keepdims=True)
    lg2 = jnp.where(lane == i1, -jnp.inf, lg)
    m2 = jnp.max(lg2, axis=-1, keepdims=True)
    i2 = jnp.min(jnp.where(lg2 == m2, lane, LANE), axis=-1, keepdims=True)
    e = jnp.exp(m2 - m1)
    w1 = 1.0 / (1.0 + e)
    w2 = e * w1
    o_ref[...] = jnp.where(lane == 0, i1.astype(F32),
                           jnp.where(lane == 1, i2.astype(F32), jnp.where(lane == 2, w1, jnp.where(lane == 3, w2, 0.0))))


def _route(xs, mod, g, rw_pad, *, n_lat, tm):
    d = xs.shape[1]
    return pl.pallas_call(
        _route_kernel,
        grid=(n_lat // tm,),
        in_specs=[pl.BlockSpec((tm, d), lambda i: (i, 0)), _const_spec(mod.shape), _const_spec(g.shape),
                  _const_spec(rw_pad.shape)],
        out_specs=[pl.BlockSpec((tm, LANE), lambda i: (i, 0)), pl.BlockSpec((tm, d // 2), lambda i: (i, 0))],
        out_shape=[jax.ShapeDtypeStruct((n_lat, LANE), F32), jax.ShapeDtypeStruct((n_lat, d // 2), jnp.uint32)],
        compiler_params=_params(("arbitrary",)),
        name="moe_route",
    )(xs, mod, g, rw_pad)


def _combine_kernel(cur_ref, nxt_ref, rt_ref, x_ref, mod_ref, pg_ref, y_hbm, o_ref, buf, sem, *, tm, n_steps):
    i = pl.program_id(0)
    slot = i % 2

    def row_copy(dest_ref, r, k, s):
        return pltpu.make_async_copy(y_hbm.at[pl.ds(dest_ref[0, k, r], 1)], buf.at[s, k, pl.ds(r, 1)], sem.at[s])

    def start_tile0(r, c):
        row_copy(cur_ref, r, 0, 0).start()
        row_copy(cur_ref, r, 1, 0).start()
        return c

    def wait(r, c):
        row_copy(cur_ref, r, 0, slot).wait()
        row_copy(cur_ref, r, 1, slot).wait()
        return c

    @pl.when(i == 0)
    def _first_tile():
        lax.fori_loop(0, tm, start_tile0, 0, unroll=DMA_UNROLL)

    lax.fori_loop(0, tm, wait, 0, unroll=DMA_UNROLL)

    @pl.when(i < n_steps)
    def _tile():
        for r in range(tm):
            row_copy(nxt_ref, r, 0, 1 - slot).start(priority=0)
            row_copy(nxt_ref, r, 1, 1 - slot).start(priority=1)
        d = x_ref.shape[-1]
        rt = rt_ref[...]
        y = rt[:, 2:3] * buf[slot, 0] + rt[:, 3:4] * buf[slot, 1]
        o_ref[...] = x_ref[...] + mod_ref[0:1, 5 * d:6 * d] * (_rms(y) * pg_ref[...])


def _combine(yg, dest, route, xs, mod, pg, *, n_lat, tm):
    d = xs.shape[1]
    n_steps = n_lat // tm
    kern = functools.partial(_combine_kernel, tm=tm, n_steps=n_steps)
    tile = lambda i: jnp.minimum(i, n_steps - 1)
    nxt_tile = lambda i: jnp.minimum(i + 1, n_steps - 1)
    return pl.pallas_call(
        kern,
        grid=(n_steps + 1,),
        in_specs=[pl.BlockSpec((1, 2, tm), lambda i: (tile(i), 0, 0), memory_space=pltpu.SMEM),
                  pl.BlockSpec((1, 2, tm), lambda i: (nxt_tile(i), 0, 0), memory_space=pltpu.SMEM),
                  pl.BlockSpec((tm, LANE), lambda i: (tile(i), 0)), pl.BlockSpec((tm, d), lambda i: (tile(i), 0)),
                  _const_spec(mod.shape), _const_spec(pg.shape), pl.BlockSpec(memory_space=pl.ANY)],
        out_specs=pl.BlockSpec((tm, d), lambda i: (tile(i), 0)),
        out_shape=jax.ShapeDtypeStruct((n_lat, d), F32),
        scratch_shapes=[pltpu.VMEM((2, 2, tm, d), F32), pltpu.SemaphoreType.DMA((2,))],
        compiler_params=_params(("arbitrary",)),
        name="moe_combine",
    )(dest, dest, route, xs, mod, pg, yg)


def _rope_tables(n_lat, n_ctx):
    rows = n_lat // GRID_W

    def tabs(rot_dim):
        axis_dim = rot_dim // 2
        quarter = rot_dim // 4
        inv_freq = ROPE_BASE ** (-jnp.arange(0, axis_dim, 2, dtype=F32) / axis_dim)
        ang_r = jnp.arange(rows, dtype=F32)[:, None] * inv_freq[None, :]
        ang_c = jnp.arange(GRID_W, dtype=F32)[:, None] * inv_freq[None, :]

        def over_grid(fn):
            r = jnp.broadcast_to(fn(ang_r)[:, None, :], (rows, GRID_W, quarter))
            c = jnp.broadcast_to(fn(ang_c)[None, :, :], (rows, GRID_W, quarter))
            return jnp.concatenate([r, r, c, c], axis=-1).reshape(n_lat, rot_dim)

        cos, sin = over_grid(jnp.cos), over_grid(jnp.sin)
        lower = (jnp.arange(rot_dim) % (2 * quarter)) < quarter
        sin_m = jnp.where(lower[None, :], -sin, 0.0)
        sin_p = jnp.where(lower[None, :], 0.0, sin)
        pad = LANE - rot_dim
        cos = jnp.pad(cos, ((0, n_ctx), (0, pad)), constant_values=1.0)
        sin_m = jnp.pad(sin_m, ((0, n_ctx), (0, pad)))
        sin_p = jnp.pad(sin_p, ((0, n_ctx), (0, pad)))
        return [cos, sin_m, sin_p]

    return jnp.stack(tabs(A_HEAD_DIM) + tabs(B_ROPE_DIM))


def _layer_weights(i, w_in, w_uq, w_ukv, w_o):
    d = w_in.shape[1]
    q_w = A_WIDTH + B_Q_RANK
    wi = w_in[i]
    qa, cq = wi[:, :A_WIDTH], wi[:, A_WIDTH:q_w]
    o1, o2, o3 = q_w + A_KV_WIDTH, q_w + 2 * A_KV_WIDTH, q_w + 2 * A_KV_WIDTH + B_KV_RANK
    ka, va, ckv, kr = wi[:, q_w:o1], wi[:, o1:o2], wi[:, o2:o3], wi[:, o3:]
    win = jnp.concatenate([qa, ka, va, cq, ckv, kr, jnp.zeros((d, LANE - B_ROPE_DIM), F32)], axis=1).astype(BF16)
    uq = w_uq[i].reshape(B_Q_RANK, B_HEADS, B_NOPE_DIM + B_ROPE_DIM)
    uq = jnp.pad(uq, ((0, 0), (0, 0), (0, B_QK_PAD - B_NOPE_DIM - B_ROPE_DIM)))
    wuq = uq.reshape(B_Q_RANK, B_HEADS * B_QK_PAD).astype(BF16)
    ukv = w_ukv[i].reshape(B_KV_RANK, B_HEADS, B_NOPE_DIM + B_V_DIM)
    wukn = ukv[:, :, :B_NOPE_DIM].reshape(B_KV_RANK, B_HEADS * B_NOPE_DIM).astype(BF16)
    wuvt = ukv[:, :, B_NOPE_DIM:].reshape(B_KV_RANK, B_HEADS * B_V_DIM).T.astype(BF16)
    return win, wuq, wukn, wuvt, w_o[i].astype(BF16)


def _routing_slots(route, n_lat, tm_e, n_tiles):
    e12 = route[:, 0:2].astype(jnp.int32).T.reshape(-1)
    onehot = (e12[:, None] == jnp.arange(N_EXPERTS)[None, :]).astype(jnp.int32)
    csum = jnp.cumsum(onehot, axis=0)
    rank = jnp.sum(csum * onehot, axis=1) - 1
    counts = csum[-1]
    tiles_per = (counts + tm_e - 1) // tm_e
    tile_end = jnp.cumsum(tiles_per)
    start = (tile_end - tiles_per) * tm_e
    dest = (jnp.sum(start[None, :] * onehot, axis=1) + rank).astype(jnp.int32)
    n_used = tile_end[-1].astype(jnp.int32).reshape(1)
    t_idx = jnp.arange(n_tiles, dtype=jnp.int32)
    tile_expert = jnp.minimum(jnp.sum((t_idx[:, None] >= tile_end[None, :]).astype(jnp.int32), axis=1),
                              N_EXPERTS - 1).astype(jnp.int32)
    token = jnp.tile(jnp.arange(n_lat, dtype=jnp.int32), 2)
    slot_src = jnp.zeros((n_tiles * tm_e,), jnp.int32).at[dest].set(token, unique_indices=True)
    return dest.reshape(2, n_lat), slot_src.reshape(n_tiles, 1, tm_e), tile_expert, n_used


def _tile(n, candidates):
    for c in candidates:
        if n % c == 0:
            return c
    raise ValueError(f"no supported tile size for {n} rows")


def kernel(x, c, ctx, c_ctx, w_ada, b_ada, pre_attn_g, post_attn_g, pre_ffn_g, post_ffn_g, w_in, attn_sink, q_norm_g,
           kv_norm_g, w_uq, w_ukv, grp_a_g, grp_b_g, w_o, ffn_w1, ffn_w3, ffn_w2, router_w, moe_w1, moe_w3, moe_w2):
    b, n_lat, d = x.shape
    n_ctx = ctx.shape[1]
    depth = w_ada.shape[0]
    assert b == 1 and c.shape[0] == 1, "single-sequence kernel"
    t_all = n_lat + n_ctx
    tm = _tile(t_all, (640, 256, 128))
    tq_a = 256
    tq_b = _tile(n_lat, (2048, 1024, 512, 256))
    tk_b = _tile(t_all, (1280, 256))
    tf = 512
    tm_r = _tile(n_lat, (512, 256))
    tm_c = 256
    assert n_ctx == tq_a and n_lat % tq_a == 0 and n_lat >= tq_a + 2 * WINDOW

    xs = jnp.concatenate([x[0], ctx[0]], axis=0)
    cc = jnp.stack([c[0], c_ctx])
    mods = _ada(cc, w_ada, b_ada)
    rope_tab = _rope_tables(n_lat, n_ctx)
    row2 = lambda v: v.reshape(1, -1)

    for i in range(depth):
        last = i == depth - 1
        mod = mods[i]
        win, wuq, wukn, wuvt, wo = _layer_weights(i, w_in, w_uq, w_ukv, w_o)
        qa, ka, va, qb, kb, vbt = _proj(xs, mod, row2(pre_attn_g[i]), win, row2(q_norm_g[i]), row2(kv_norm_g[i]),
                                        wuq, wukn, wuvt, rope_tab, n_lat=n_lat, tm=tm)
        sink_tab = jnp.broadcast_to((attn_sink[i] * LOG2E)[:, None], (A_HEADS, tq_a))
        oa = _attn_a(qa, ka, va, sink_tab, n_lat=n_lat, n_ctx=n_ctx, tq=tq_a)
        ob = _mla(qb, kb, vbt, jnp.zeros((t_all, B_WIDTH), BF16), q_row0=0, n_q=n_lat, k_row0=0, n_k=t_all,
                  tq=tq_b, tk=tk_b)
        ob = _mla(qb, kb, vbt, ob, q_row0=n_lat, n_q=n_ctx, k_row0=n_lat, n_k=n_ctx, tq=n_ctx, tk=n_ctx)
        xs = _attn_out(oa, ob, xs, mod, row2(grp_a_g[i]), row2(grp_b_g[i]), wo, row2(post_attn_g[i]),
                       n_lat=n_lat, tm=tm)
        jj = i // 2
        if i % 2 == 0:
            xs = _ffn_dense(xs, mod, row2(pre_ffn_g[i]), row2(post_ffn_g[i]), ffn_w1[jj].astype(BF16),
                            ffn_w3[jj].astype(BF16), ffn_w2[jj].astype(BF16), n_lat=n_lat, tm=tm, tf=tf)
        else:
            n_rows = n_lat
            assert last, "expert layers other than the last would also need the context rows routed"
            nj = moe_w1.shape[-1] // tf
            tm_e = -(-tm // (nj * ROW_CHUNK)) * ROW_CHUNK * nj
            n_tiles = (2 * n_rows + N_EXPERTS * (tm_e - 1)) // tm_e + 1
            rw_top = _bf16_part(router_w[jj])
            rw_hi, rw_lo = rw_top.astype(BF16), (router_w[jj] - rw_top).astype(BF16)
            zeros = jnp.zeros((d, LANE - 2 * N_EXPERTS), BF16)
            rw_pad = jnp.stack([jnp.concatenate([rw_hi, rw_lo, zeros], axis=1),
                                jnp.concatenate([rw_hi, jnp.zeros_like(rw_lo), zeros], axis=1)])
            route, hp = _route(xs, mod, row2(pre_ffn_g[i]), rw_pad, n_lat=n_rows, tm=tm_r)
            dest, slot_src, tile_expert, n_used = _routing_slots(route, n_rows, tm_e, n_tiles)
            dest_t = dest.reshape(2, n_rows // tm_c, tm_c).transpose(1, 0, 2)
            yg = _ffn_moe(hp, slot_src, tile_expert, n_used, moe_w1[jj], moe_w3[jj], moe_w2[jj], tm=tm_e, tf=tf)
            xs = _combine(yg, dest_t, route, xs, mod, row2(post_ffn_g[i]), n_lat=n_rows, tm=tm_c)
    return xs[:n_lat].reshape(b, n_lat, d)
```
